```python
import math
import jax
import jax.numpy as jnp
from jax import lax
import numpy as np

D_MODEL = 1024
BATCH = 1
SEQ = 16384
DEPTH = 2

GRID_W = 64
CTX_LEN = 256

LRU_WIDTH = 512
LRU_BLOCKS = 8
LRU_BLOCK = LRU_WIDTH // LRU_BLOCKS
CONV_W = 4
LRU_C = 8.0
S5_WIDTH = 512
S5_GROUP = 16
S5_GROUPS = S5_WIDTH // S5_GROUP
S5_STATE = 64
DA_HEADS = 4
DA_HEAD_DIM = 64
DA_V_DIM = 2 * DA_HEAD_DIM
DA_QK_WIDTH = DA_HEADS * 2 * DA_HEAD_DIM
DA_WIDTH = DA_HEADS * DA_V_DIM
DA_SCALE = DA_HEAD_DIM ** -0.5
Q_BLOCK = 128
ROPE_BASE = 10000.0
ROPE_FREQS = DA_HEAD_DIM // 4
N_BRANCH = 3
BRANCH_WIDTH = 512
IN_SPLITS = [LRU_WIDTH, 2 * LRU_WIDTH, 2 * LRU_WIDTH + S5_WIDTH,
             2 * LRU_WIDTH + S5_WIDTH + DA_QK_WIDTH,
             2 * LRU_WIDTH + S5_WIDTH + 2 * DA_QK_WIDTH,
             2 * LRU_WIDTH + S5_WIDTH + 2 * DA_QK_WIDTH + DA_WIDTH]
IN_COLS = IN_SPLITS[-1] + N_BRANCH * D_MODEL
N_EXPERTS = 32
TOP_K = 4
D_FF = D_MODEL
SWIGLU_ALPHA = 1.702
SWIGLU_LIMIT = 7.0
MOE_BLOCK = 128
DN_ALPHA = (2 * DEPTH) ** 0.25
DN_BETA = (8 * DEPTH) ** -0.25
LN_EPS = 1e-5
RMS_EPS = 1e-6

kernel_name = "hybrid_rglru_s5_diffattn_moe_dit"


def layer_norm(x, g, b):
    xf = x.astype(jnp.float32)
    mu = jnp.mean(xf, -1, keepdims=True)
    var = jnp.mean(jnp.square(xf - mu), -1, keepdims=True)
    return ((xf - mu) * lax.rsqrt(var + LN_EPS) * g + b).astype(x.dtype)


def centred_dwconv(x, w, b):
    L = x.shape[1]
    left = CONV_W // 2
    xp = jnp.pad(x, ((0, 0), (left, CONV_W - 1 - left), (0, 0)))
    out = b
    for k in range(CONV_W):
        out = out + xp[:, k:k + L] * w[k]
    return out


def _lin_op(l, r):
    a1, b1 = l
    a2, b2 = r
    return a1 * a2, a2 * b1 + b2


def lin_scan(a, b, h0=None):
    A, H = lax.associative_scan(_lin_op, (a, b), axis=1)
    return H if h0 is None else H + A * h0[:, None]


def _cplx_op(l, r):
    a1r, a1i, b1r, b1i = l
    a2r, a2i, b2r, b2i = r
    return (a2r * a1r - a2i * a1i, a2r * a1i + a2i * a1r,
            a2r * b1r - a2i * b1i + b2r, a2r * b1i + a2i * b1r + b2i)


def cplx_scan(ar, ai, br, bi, h0=None):
    Ar, Ai, Hr, Hi = lax.associative_scan(_cplx_op, (ar, ai, br, bi), axis=1)
    if h0 is None:
        return Hr, Hi
    h0r, h0i = h0[0][:, None], h0[1][:, None]
    return Hr + Ar * h0r - Ai * h0i, Hi + Ar * h0i + Ai * h0r


def rope_2d(x, ang):
    xs = x.reshape(x.shape[:-1] + (2, 2, ROPE_FREQS))
    cos = jnp.cos(ang).astype(x.dtype)[None, :, None, None]
    sin = jnp.sin(ang).astype(x.dtype)[None, :, None, None]
    x1, x2 = xs[..., 0, :], xs[..., 1, :]
    return jnp.stack([x1 * cos - x2 * sin, x2 * cos + x1 * sin], axis=-2).reshape(x.shape)


def rglru_mixer(xa_c, xa_l, ga_c, ga_l, conv_w, conv_b, wr, br, wi, bi, lam, need_ctx):
    xc = centred_dwconv(xa_c, conv_w, conv_b)
    xl = centred_dwconv(xa_l, conv_w, conv_b)

    def coeffs(x, d):
        xb = x.reshape(x.shape[:-1] + (LRU_BLOCKS, LRU_BLOCK))
        r = jax.nn.sigmoid(jnp.einsum('blhi,hij->blhj', xb, wr[d]).reshape(x.shape) + br[d])
        i = jax.nn.sigmoid(jnp.einsum('blhi,hij->blhj', xb, wi[d]).reshape(x.shape) + bi[d])
        log_a = -LRU_C * r * jax.nn.softplus(-lam[d])
        return jnp.exp(log_a), jnp.sqrt(-jnp.expm1(2.0 * log_a)) * (i * x)

    def direction(d, seq_c, seq_l):
        h_c = lin_scan(*coeffs(seq_c, d))
        h_l = lin_scan(*coeffs(seq_l, d), h0=h_c[:, -1])
        return h_c, h_l

    hc_f, hl_f = direction(0, xc, xl)
    hc_b, hl_b = direction(1, xc[:, ::-1], xl[:, ::-1])
    y_l = (hl_f + hl_b[:, ::-1]) * jax.nn.gelu(ga_l)
    y_c = (hc_f + hc_b[:, ::-1]) * jax.nn.gelu(ga_c) if need_ctx else None
    return y_c, y_l


def s5_mixer(ub_c, ub_l, a_re, a_im, log_dt, b_re, b_im, c_re, c_im, d_skip, w_glu, b_glu, need_ctx):
    uc = ub_c.reshape(ub_c.shape[:-1] + (S5_GROUPS, S5_GROUP))
    ul = ub_l.reshape(ub_l.shape[:-1] + (S5_GROUPS, S5_GROUP))

    def direction(d, seq_c, seq_l):
        dt = jnp.exp(log_dt[d])[:, None]
        lr, li = a_re[d], a_im[d]
        ea = jnp.exp(lr * dt)
        ab_r, ab_i = ea * jnp.cos(li * dt), ea * jnp.sin(li * dt)
        den = lr * lr + li * li
        co_r = ((ab_r - 1.0) * lr + ab_i * li) / den
        co_i = (ab_i * lr - (ab_r - 1.0) * li) / den
        bb_r = co_r[..., None] * b_re[d] - co_i[..., None] * b_im[d]
        bb_i = co_r[..., None] * b_im[d] + co_i[..., None] * b_re[d]

        def run(u, h0):
            dr = jnp.einsum('blgc,gnc->blgn', u, bb_r)
            di = jnp.einsum('blgc,gnc->blgn', u, bb_i)
            return cplx_scan(jnp.broadcast_to(ab_r, dr.shape), jnp.broadcast_to(ab_i, dr.shape), dr, di, h0)

        def readout(h):
            return jnp.einsum('blgn,gcn->blgc', h[0], c_re[d]) - jnp.einsum('blgn,gcn->blgc', h[1], c_im[d])

        h_c = run(seq_c, None)
        h_l = run(seq_l, (h_c[0][:, -1], h_c[1][:, -1]))
        return (readout(h_c) if need_ctx else None), readout(h_l)

    yc_f, yl_f = direction(0, uc, ul)
    yc_b, yl_b = direction(1, uc[:, ::-1], ul[:, ::-1])

    def finish(y_f, y_b, u):
        y = y_f + y_b[:, ::-1] + d_skip.reshape(S5_GROUPS, S5_GROUP) * u
        y = y.reshape(y.shape[:2] + (S5_WIDTH,))
        a, g = jnp.split(jax.nn.gelu(y) @ w_glu + b_glu, 2, axis=-1)
        return a * jax.nn.sigmoid(g)

    y_l = finish(yl_f, yl_b, ul)
    y_c = finish(yc_f, yc_b, uc) if need_ctx else None
    return y_c, y_l


def diff_attention(q_c, k_c, v_c, q_l, k_l, v_l, lam_vec, subln_g, layer_idx, ang, need_ctx):
    split_qk = lambda t: t.reshape(t.shape[:-1] + (DA_HEADS, 2, DA_HEAD_DIM))
    split_v = lambda t: t.reshape(t.shape[:-1] + (DA_HEADS, DA_V_DIM))
    q_c, k_c, q_l, k_l = split_qk(q_c), split_qk(k_c), split_qk(q_l), split_qk(k_l)
    v_c, v_l = split_v(v_c), split_v(v_l)
    lam_init = 0.8 - 0.6 * math.exp(-0.3 * layer_idx)
    lv = lam_vec.astype(jnp.float32)
    lam = jnp.exp(jnp.sum(lv[0] * lv[1])) - jnp.exp(jnp.sum(lv[2] * lv[3])) + lam_init
    q_l, k_l = rope_2d(q_l, ang), rope_2d(k_l, ang)
    k_all = jnp.concatenate([k_c, k_l], axis=1)
    v_all = jnp.concatenate([v_c, v_l], axis=1)

    def attend(qb, kk, vv):
        s = jnp.einsum('bqhmd,bkhmd->bhmqk', qb, kk).astype(jnp.float32) * DA_SCALE
        p = jax.nn.softmax(s, axis=-1)
        w = (p[:, :, 0] - lam * p[:, :, 1]).astype(vv.dtype)
        return jnp.einsum('bhqk,bkhe->bqhe', w, vv)

    B, L = q_l.shape[:2]
    nb = L // Q_BLOCK
    qb = jnp.moveaxis(q_l.reshape((B, nb, Q_BLOCK) + q_l.shape[2:]), 1, 0)
    o_l = lax.map(lambda q: attend(q, k_all, v_all), qb)
    o_l = jnp.moveaxis(o_l, 0, 1).reshape(B, L, DA_HEADS, DA_V_DIM)

    def head_norm(o):
        of = o.astype(jnp.float32)
        of = of * lax.rsqrt(jnp.mean(of * of, -1, keepdims=True) + RMS_EPS) * subln_g * (1.0 - lam_init)
        return of.astype(o.dtype).reshape(o.shape[:2] + (DA_WIDTH,))

    y_l = head_norm(o_l)
    y_c = head_norm(attend(q_c, k_c, v_c)) if need_ctx else None
    return y_c, y_l


def token_mixer(u_c, u_l, layer_idx, need_ctx, ang, w_in, conv_w, conv_b, lru_wr, lru_br, lru_wi,
                lru_bi, lru_lam, s5_a_re, s5_a_im, s5_log_dt, s5_b_re, s5_b_im, s5_c_re, s5_c_im,
                s5_d, s5_w_glu, s5_b_glu, da_lam, da_subln_g, w_proj, b_gate, w_out):
    xa_c, ga_c, ub_c, q_c, k_c, v_c, gt_c = jnp.split(u_c @ w_in, IN_SPLITS, axis=-1)
    xa_l, ga_l, ub_l, q_l, k_l, v_l, gt_l = jnp.split(u_l @ w_in, IN_SPLITS, axis=-1)
    ya_c, ya_l = rglru_mixer(xa_c, xa_l, ga_c, ga_l, conv_w, conv_b, lru_wr, lru_br, lru_wi, lru_bi,
                             lru_lam, need_ctx)
    yb_c, yb_l = s5_mixer(ub_c, ub_l, s5_a_re, s5_a_im, s5_log_dt, s5_b_re, s5_b_im, s5_c_re, s5_c_im,
                          s5_d, s5_w_glu, s5_b_glu, need_ctx)
    yc_c, yc_l = diff_attention(q_c, k_c, v_c, q_l, k_l, v_l, da_lam, da_subln_g, layer_idx, ang, need_ctx)

    def merge(gt, ya, yb, yc):
        g = jax.nn.sigmoid(gt.reshape(gt.shape[:-1] + (N_BRANCH, D_MODEL)) + b_gate)
        z = (g[..., 0, :] * (ya @ w_proj[0]) + g[..., 1, :] * (yb @ w_proj[1])
             + g[..., 2, :] * (yc @ w_proj[2]))
        return z @ w_out

    m_l = merge(gt_l, ya_l, yb_l, yc_l)
    m_c = merge(gt_c, ya_c, yb_c, yc_c) if need_ctx else None
    return m_c, m_l


def moe(u, w_router, b_router, w_gu, b_gu, w_down, b_down):
    T = u.shape[0]
    n_assign = T * TOP_K
    n_slots = -(-n_assign // MOE_BLOCK) * MOE_BLOCK + N_EXPERTS * MOE_BLOCK
    logits = (u @ w_router + b_router).astype(jnp.float32)
    top_v, top_i = lax.top_k(logits, TOP_K)
    gate_w = jax.nn.softmax(top_v, axis=-1).astype(u.dtype)
    flat_e = top_i.reshape(-1)
    flat_t = jnp.arange(n_assign, dtype=jnp.int32) // TOP_K
    flat_w = gate_w.reshape(-1)
    order = jnp.argsort(flat_e)
    se = flat_e[order]
    counts = jnp.bincount(flat_e, length=N_EXPERTS)
    padded = (counts + MOE_BLOCK - 1) // MOE_BLOCK * MOE_BLOCK
    start = jnp.cumsum(counts) - counts
    pend = jnp.cumsum(padded)
    pstart = pend - padded
    dest = pstart[se] + jnp.arange(n_assign, dtype=jnp.int32) - start[se]
    slot_tok = jnp.zeros((n_slots,), jnp.int32).at[dest].set(flat_t[order])
    slot_w = jnp.zeros((n_slots,), u.dtype).at[dest].set(flat_w[order])
    nb = n_slots // MOE_BLOCK
    blk_e = jnp.clip(jnp.searchsorted(pend, jnp.arange(nb) * MOE_BLOCK, side='right'), 0, N_EXPERTS - 1)
    xs = u[slot_tok].reshape(nb, MOE_BLOCK, u.shape[-1])

    def expert_block(args):
        xb, e = args
        h = xb @ w_gu[e] + b_gu[e]
        hg, hl = h[:, :D_FF], h[:, D_FF:]
        hg = jnp.minimum(hg, SWIGLU_LIMIT)
        hl = jnp.clip(hl, -SWIGLU_LIMIT, SWIGLU_LIMIT)
        act = hg * jax.nn.sigmoid(SWIGLU_ALPHA * hg) * (hl + 1.0)
        return act @ w_down[e] + b_down[e]

    ys = lax.map(expert_block, (xs, blk_e)).reshape(n_slots, u.shape[-1])
    return jnp.zeros_like(u).at[slot_tok].add(ys * slot_w[:, None])


def setup_inputs(seed: int = 0) -> dict:
    key = jax.random.key(seed)
    ks = iter(jax.random.split(key, 48))
    f32 = jnp.float32

    def nrm(shape, scale):
        return scale * jax.random.normal(next(ks), shape, f32)

    a0 = jax.random.uniform(next(ks), (DEPTH, 2, LRU_WIDTH), f32, minval=0.9, maxval=0.999)
    s0 = a0 ** (1.0 / LRU_C)
    lru_lam = jnp.log(s0) - jnp.log1p(-s0)
    s5_a_re = -0.5 + nrm((DEPTH, 2, S5_GROUPS, S5_STATE), 0.01)
    s5_a_im = jnp.pi * jnp.arange(S5_STATE, dtype=f32) + nrm((DEPTH, 2, S5_GROUPS, S5_STATE), 0.01)
    s5_log_dt = jax.random.uniform(next(ks), (DEPTH, 2, S5_GROUPS), f32,
                                   minval=math.log(1e-3), maxval=math.log(1e-1))
    return {
        "x": nrm((BATCH, SEQ, D_MODEL), 1.0),
        "c": nrm((BATCH, D_MODEL), 1.0),
        "ctx": nrm((BATCH, CTX_LEN, D_MODEL), 1.0),
        "c_ctx": nrm((D_MODEL,), 1.0),
        "w_mod": nrm((DEPTH, D_MODEL, 6 * D_MODEL), 0.5 * D_MODEL ** -0.5),
        "b_mod": nrm((DEPTH, 6 * D_MODEL), 0.01),
        "w_in": nrm((DEPTH, D_MODEL, IN_COLS), D_MODEL ** -0.5),
        "conv_w": nrm((DEPTH, CONV_W, LRU_WIDTH), CONV_W ** -0.5),
        "conv_b": nrm((DEPTH, LRU_WIDTH), 0.01),
        "lru_wr": nrm((DEPTH, 2, LRU_BLOCKS, LRU_BLOCK, LRU_BLOCK), LRU_BLOCK ** -0.5),
        "lru_br": nrm((DEPTH, 2, LRU_WIDTH), 0.01),
        "lru_wi": nrm((DEPTH, 2, LRU_BLOCKS, LRU_BLOCK, LRU_BLOCK), LRU_BLOCK ** -0.5),
        "lru_bi": nrm((DEPTH, 2, LRU_WIDTH), 0.01),
        "lru_lam": lru_lam,
        "s5_a_re": s5_a_re,
        "s5_a_im": s5_a_im,
        "s5_log_dt": s5_log_dt,
        "s5_b_re": nrm((DEPTH, 2, S5_GROUPS, S5_STATE, S5_GROUP), (2 * S5_GROUP) ** -0.5),
        "s5_b_im": nrm((DEPTH, 2, S5_GROUPS, S5_STATE, S5_GROUP), (2 * S5_GROUP) ** -0.5),
        "s5_c_re": nrm((DEPTH, 2, S5_GROUPS, S5_GROUP, S5_STATE), S5_STATE ** -0.5),
        "s5_c_im": nrm((DEPTH, 2, S5_GROUPS, S5_GROUP, S5_STATE), S5_STATE ** -0.5),
        "s5_d": nrm((DEPTH, S5_WIDTH), 1.0),
        "s5_w_glu": nrm((DEPTH, S5_WIDTH, 2 * S5_WIDTH), S5_WIDTH ** -0.5),
        "s5_b_glu": nrm((DEPTH, 2 * S5_WIDTH), 0.01),
        "da_lam": nrm((DEPTH, 4, DA_HEAD_DIM), 0.1),
        "da_subln_g": 1.0 + nrm((DEPTH, DA_V_DIM), 0.01),
        "w_proj": nrm((DEPTH, N_BRANCH, BRANCH_WIDTH, D_MODEL), DN_BETA * BRANCH_WIDTH ** -0.5),
        "b_gate": nrm((DEPTH, N_BRANCH, D_MODEL), 0.01),
        "w_out": nrm((DEPTH, D_MODEL, D_MODEL), DN_BETA * D_MODEL ** -0.5),
        "ln_g": 1.0 + nrm((DEPTH, 2, D_MODEL), 0.01),
        "ln_b": nrm((DEPTH, 2, D_MODEL), 0.01),
        "w_router": nrm((DEPTH, D_MODEL, N_EXPERTS), D_MODEL ** -0.5),
        "b_router": nrm((DEPTH, N_EXPERTS), 0.01),
        "w_gu": nrm((DEPTH, N_EXPERTS, D_MODEL, 2 * D_FF), DN_BETA * D_MODEL ** -0.5),
        "b_gu": nrm((DEPTH, N_EXPERTS, 2 * D_FF), 0.01),
        "w_down": nrm((DEPTH, N_EXPERTS, D_FF, D_MODEL), DN_BETA * D_FF ** -0.5),
        "b_down": nrm((DEPTH, N_EXPERTS, D_MODEL), 0.01),
    }


def reference(x, c, ctx, c_ctx, w_mod, b_mod, w_in, conv_w, conv_b, lru_wr, lru_br, lru_wi, lru_bi,
              lru_lam, s5_a_re, s5_a_im, s5_log_dt, s5_b_re, s5_b_im, s5_c_re, s5_c_im, s5_d, s5_w_glu,
              s5_b_glu, da_lam, da_subln_g, w_proj, b_gate, w_out, ln_g, ln_b, w_router, b_router,
              w_gu, b_gu, w_down, b_down):
    B, L, D = x.shape
    ROWS = L // GRID_W
    rows = jnp.broadcast_to(jnp.arange(ROWS)[:, None], (ROWS, GRID_W)).reshape(-1)
    cols = jnp.broadcast_to(jnp.arange(GRID_W)[None, :], (ROWS, GRID_W)).reshape(-1)
    freqs = ROPE_BASE ** (-jnp.arange(ROPE_FREQS, dtype=jnp.float32) / ROPE_FREQS)
    ang = jnp.stack([rows, cols], axis=-1).astype(jnp.float32)[:, :, None] * freqs

    xl, xc = x, ctx
    for l in range(DEPTH):
        last = l == DEPTH - 1
        mod_l = jax.nn.silu(c) @ w_mod[l] + b_mod[l]
        mod_c = jax.nn.silu(c_ctx) @ w_mod[l] + b_mod[l]
        sh1_l, sc1_l, g1_l, sh2_l, sc2_l, g2_l = [m[:, None] for m in jnp.split(mod_l, 6, axis=-1)]
        sh1_c, sc1_c, g1_c, sh2_c, sc2_c, g2_c = jnp.split(mod_c, 6, axis=-1)

        u_c = xc * (1.0 + sc1_c) + sh1_c
        u_l = xl * (1.0 + sc1_l) + sh1_l
        m_c, m_l = token_mixer(
            u_c, u_l, l, not last, ang, w_in[l], conv_w[l], conv_b[l], lru_wr[l], lru_br[l],
            lru_wi[l], lru_bi[l], lru_lam[l], s5_a_re[l], s5_a_im[l], s5_log_dt[l], s5_b_re[l],
            s5_b_im[l], s5_c_re[l], s5_c_im[l], s5_d[l], s5_w_glu[l], s5_b_glu[l], da_lam[l],
            da_subln_g[l], w_proj[l], b_gate[l], w_out[l])
        xl = layer_norm(DN_ALPHA * xl + g1_l * m_l, ln_g[l, 0], ln_b[l, 0])
        if not last:
            xc = layer_norm(DN_ALPHA * xc + g1_c * m_c, ln_g[l, 0], ln_b[l, 0])

        v_l = xl * (1.0 + sc2_l) + sh2_l
        if not last:
            v_c = xc * (1.0 + sc2_c) + sh2_c
            tok = jnp.concatenate([v_c, v_l], axis=1).reshape(-1, D)
            f = moe(tok, w_router[l], b_router[l], w_gu[l], b_gu[l], w_down[l], b_down[l])
            f = f.reshape(B, CTX_LEN + L, D)
            xc = layer_norm(DN_ALPHA * xc + g2_c * f[:, :CTX_LEN], ln_g[l, 1], ln_b[l, 1])
            f_l = f[:, CTX_LEN:]
        else:
            f_l = moe(v_l.reshape(-1, D), w_router[l], b_router[l], w_gu[l], b_gu[l],
                      w_down[l], b_down[l]).reshape(B, L, D)
        xl = layer_norm(DN_ALPHA * xl + g2_l * f_l, ln_g[l, 1], ln_b[l, 1])
    return xl
```

```python
import functools
import math

import numpy as np
import jax
import jax.numpy as jnp
from jax import lax
from jax.experimental import pallas as pl
from jax.experimental.pallas import tpu as pltpu

F32 = jnp.float32
BF16 = jnp.bfloat16
HIGHEST = lax.Precision.HIGHEST

D_MODEL = 1024
GRID_W = 64
LRU_WIDTH = 512
LRU_BLOCKS = 8
CONV_W = 4
LRU_C = 8.0
S5_WIDTH = 512
S5_GROUP = 16
S5_GROUPS = S5_WIDTH // S5_GROUP
S5_STATE = 64
S5_N = S5_GROUPS * S5_STATE
DA_HEADS = 4
DA_HEAD_DIM = 64
DA_V_DIM = 128
DA_SCALE = DA_HEAD_DIM ** -0.5
ROPE_BASE = 10000.0
ROPE_FREQS = DA_HEAD_DIM // 4
N_BRANCH = 3
BW = 512
IN_COLS = 6 * BW + N_BRANCH * D_MODEL
N_EXPERTS = 32
TOP_K = 4
D_FF = D_MODEL
SWIGLU_ALPHA = 1.702
SWIGLU_LIMIT = 7.0
LN_EPS = 1e-5
RMS_EPS = 1e-6

SUBLANES = 8
LANES = 128
VMEM_LIMIT = 56 * 1024 * 1024

TM = 256
SEG = TM // SUBLANES
MOE_BLK = 256
TN_IN = 1536


def _cp(sem, vmem=VMEM_LIMIT):
    return pltpu.CompilerParams(dimension_semantics=sem, vmem_limit_bytes=vmem)


def _sigmoid(x):
    return 1.0 / (1.0 + jnp.exp(-x))


def _gelu_tanh(x):
    return 0.5 * x * (1.0 + jnp.tanh(math.sqrt(2.0 / math.pi) * (x + 0.044715 * (x * x * x))))


def _layer_norm(y, g, b):
    mu = jnp.mean(y, axis=-1, keepdims=True)
    yc = y - mu
    var = jnp.mean(yc * yc, axis=-1, keepdims=True)
    return yc * lax.rsqrt(var + LN_EPS) * g + b


def _mod_kernel(c_ref, w_ref, b_ref, o_ref):
    c = c_ref[...]
    s = c * _sigmoid(c)
    o_ref[0] = jnp.dot(s, w_ref[0], preferred_element_type=F32, precision=HIGHEST) + b_ref[0]


def _modulation(cc, w_mod, b_mod):
    depth = w_mod.shape[0]
    tn = 1536
    return pl.pallas_call(
        _mod_kernel,
        out_shape=jax.ShapeDtypeStruct((depth, SUBLANES, 6 * D_MODEL), F32),
        grid=(depth, 6 * D_MODEL // tn),
        in_specs=[
            pl.BlockSpec((SUBLANES, D_MODEL), lambda l, j: (0, 0)),
            pl.BlockSpec((1, D_MODEL, tn), lambda l, j: (l, 0, j)),
            pl.BlockSpec((1, 1, tn), lambda l, j: (l, 0, j)),
        ],
        out_specs=pl.BlockSpec((1, SUBLANES, tn), lambda l, j: (l, 0, j)),
        compiler_params=_cp(("arbitrary", "arbitrary")),
        name="modulation",
    )(cc, w_mod, b_mod.reshape(depth, 1, 6 * D_MODEL))


def _inproj_kernel(x_ref, mod_ref, w_ref, o_ref):
    md = mod_ref[0]
    u = x_ref[...] * (1.0 + md[1:2]) + md[0:1]
    o_ref[...] = jnp.dot(u.astype(BF16), w_ref[...], preferred_element_type=F32)


def _inproj(x, modv, w_bf16, n_lat_blocks):
    t = x.shape[0]
    return pl.pallas_call(
        _inproj_kernel,
        out_shape=jax.ShapeDtypeStruct((t, IN_COLS), F32),
        grid=(IN_COLS // TN_IN, t // TM),
        in_specs=[
            pl.BlockSpec((TM, D_MODEL), lambda j, i: (i, 0)),
            pl.BlockSpec((1, 6, D_MODEL), lambda j, i: (i // n_lat_blocks, 0, 0)),
            pl.BlockSpec((D_MODEL, TN_IN), lambda j, i: (0, j)),
        ],
        out_specs=pl.BlockSpec((TM, TN_IN), lambda j, i: (i, j)),
        compiler_params=_cp(("arbitrary", "arbitrary")),
        name="inproj",
    )(x, modv, w_bf16)


def _scan_block(s, nb, reverse):
    return (nb - s) if reverse else (s + nb) % (nb + 1)


def _lru_kernel(prev_ref, cur_ref, next_ref, cw_ref, cb_ref, wr_ref, br_ref, wi_ref, bi_ref,
                sp_ref, o_ref, a_s, b_s, h_s, *, nb, reverse):
    s = pl.program_id(0)
    blk = _scan_block(s, nb, reverse)

    @pl.when(s == 0)
    def _():
        h_s[...] = jnp.zeros_like(h_s)

    has_prev = jnp.logical_and(blk != 0, blk != nb)
    has_next = jnp.logical_and(blk != nb - 1, blk != nb)
    prev = jnp.where(has_prev, prev_ref[...], 0.0)
    nxt = jnp.where(has_next, next_ref[...], 0.0)
    ext = jnp.concatenate([prev, cur_ref[...], nxt], axis=0)
    cw = cw_ref[...]
    xc = cb_ref[...] + jnp.zeros((TM, LRU_WIDTH), F32)
    for k in range(CONV_W):
        off = SUBLANES - CONV_W // 2 + k
        xc = xc + ext[off:off + TM] * cw[k:k + 1]
    xb = xc.astype(BF16)
    r = _sigmoid(jnp.dot(xb, wr_ref[...], preferred_element_type=F32) + br_ref[...])
    i = _sigmoid(jnp.dot(xb, wi_ref[...], preferred_element_type=F32) + bi_ref[...])
    a = jnp.exp(-LRU_C * r * sp_ref[...])
    a_s[...] = a
    b_s[...] = jnp.sqrt(1.0 - a * a) * (i * xc)

    def step(n, h):
        t = (TM - 1 - n) if reverse else n
        h = a_s[pl.ds(t, 1), :] * h + b_s[pl.ds(t, 1), :]
        o_ref[pl.ds(t, 1), :] = h
        return h

    h_s[...] = lax.fori_loop(0, TM, step, h_s[...], unroll=8)


def _lru(hmat, nb, reverse, conv_w, conv_b, wr, br, wi, bi, sp):
    t = hmat.shape[0]
    nblk = nb + 1
    r8 = TM // SUBLANES
    last8 = t // SUBLANES - 1

    def bmap(s):
        return _scan_block(s, nb, reverse)

    vec = lambda: pl.BlockSpec((1, LRU_WIDTH), lambda s: (0, 0))
    mat = lambda: pl.BlockSpec((LRU_WIDTH, LRU_WIDTH), lambda s: (0, 0))
    return pl.pallas_call(
        functools.partial(_lru_kernel, nb=nb, reverse=reverse),
        out_shape=jax.ShapeDtypeStruct((t, LRU_WIDTH), F32),
        grid=(nblk,),
        in_specs=[
            pl.BlockSpec((SUBLANES, LRU_WIDTH), lambda s: (jnp.maximum(bmap(s) * r8 - 1, 0), 0)),
            pl.BlockSpec((TM, LRU_WIDTH), lambda s: (bmap(s), 0)),
            pl.BlockSpec((SUBLANES, LRU_WIDTH), lambda s: (jnp.minimum((bmap(s) + 1) * r8, last8), 0)),
            pl.BlockSpec((CONV_W, LRU_WIDTH), lambda s: (0, 0)),
            vec(), mat(), vec(), mat(), vec(), vec(),
        ],
        out_specs=pl.BlockSpec((TM, LRU_WIDTH), lambda s: (bmap(s), 0)),
        scratch_shapes=[pltpu.VMEM((TM, LRU_WIDTH), F32), pltpu.VMEM((TM, LRU_WIDTH), F32),
                        pltpu.VMEM((1, LRU_WIDTH), F32)],
        compiler_params=_cp(("arbitrary",)),
        name="lru_bwd" if reverse else "lru_fwd",
    )(hmat, hmat, hmat, conv_w, conv_b, wr, br, wi, bi, sp)


def _cmul(ar, ai, br, bi):
    return ar * br - ai * bi, ar * bi + ai * br


def _s5_kernel(u_ref, p_ref, pt_ref, bm_ref, cm_ref, a_ref, pw_ref, o_ref, x_s, st_s, c_s):
    s = pl.program_id(0)
    n = S5_N

    @pl.when(s == 0)
    def _():
        c_s[...] = jnp.zeros_like(c_s)

    up = jnp.dot(p_ref[...], u_ref[...].astype(BF16), preferred_element_type=F32).astype(BF16)
    x_s[...] = jnp.dot(up, bm_ref[...], preferred_element_type=F32)
    ar = a_ref[0:1, :]
    ai = a_ref[1:2, :]

    def local(m, carry):
        hr, hi = carry
        row = pl.multiple_of(m * SUBLANES, SUBLANES)
        pr, pi = _cmul(ar, ai, hr, hi)
        hr = pr + x_s[pl.ds(row, SUBLANES), 0:n]
        hi = pi + x_s[pl.ds(row, SUBLANES), n:2 * n]
        x_s[pl.ds(row, SUBLANES), 0:n] = hr
        x_s[pl.ds(row, SUBLANES), n:2 * n] = hi
        return hr, hi

    z = jnp.zeros((SUBLANES, n), F32)
    fr, fi = lax.fori_loop(0, SEG, local, (z, z))

    qr = pw_ref[SEG - 1:SEG, 0:n]
    qi = pw_ref[SEG - 1:SEG, n:2 * n]
    sr = c_s[0:1, 0:n]
    si = c_s[0:1, n:2 * n]
    for j in range(SUBLANES):
        st_s[j:j + 1, 0:n] = sr
        st_s[j:j + 1, n:2 * n] = si
        pr, pi = _cmul(qr, qi, sr, si)
        sr = pr + fr[j:j + 1]
        si = pi + fi[j:j + 1]
    c_s[0:1, 0:n] = sr
    c_s[0:1, n:2 * n] = si

    str_ = st_s[:, 0:n]
    sti = st_s[:, n:2 * n]

    def fix(m, _):
        row = pl.multiple_of(m * SUBLANES, SUBLANES)
        pr, pi = _cmul(pw_ref[pl.ds(m, 1), 0:n], pw_ref[pl.ds(m, 1), n:2 * n], str_, sti)
        x_s[pl.ds(row, SUBLANES), 0:n] = x_s[pl.ds(row, SUBLANES), 0:n] + pr
        x_s[pl.ds(row, SUBLANES), n:2 * n] = x_s[pl.ds(row, SUBLANES), n:2 * n] + pi
        return 0

    lax.fori_loop(0, SEG, fix, 0)

    yp = jnp.dot(x_s[...].astype(BF16), cm_ref[...], preferred_element_type=F32)
    y_hi = yp.astype(BF16)
    y_lo = (yp - y_hi.astype(F32)).astype(BF16)
    pt = pt_ref[...]
    o_ref[...] = (jnp.dot(pt, y_hi, preferred_element_type=F32)
                  + jnp.dot(pt, y_lo, preferred_element_type=F32))


def _s5_perm(reverse):
    p = np.zeros((TM, TM), np.float32)
    for m in range(SEG):
        for j in range(SUBLANES):
            pos = j * SEG + m
            p[m * SUBLANES + j, (TM - 1 - pos) if reverse else pos] = 1.0
    return p


def _s5(hmat, nb, reverse, bm_t, cm_t, a_row, pw):
    t = hmat.shape[0]
    p = _s5_perm(reverse)
    n2 = 2 * S5_N

    def bmap(s):
        return _scan_block(s, nb, reverse)

    return pl.pallas_call(
        _s5_kernel,
        out_shape=jax.ShapeDtypeStruct((t, S5_WIDTH), F32),
        grid=(nb + 1,),
        in_specs=[
            pl.BlockSpec((TM, S5_WIDTH), lambda s: (bmap(s), 2)),
            pl.BlockSpec((TM, TM), lambda s: (0, 0)),
            pl.BlockSpec((TM, TM), lambda s: (0, 0)),
            pl.BlockSpec((S5_WIDTH, n2), lambda s: (0, 0)),
            pl.BlockSpec((n2, S5_WIDTH), lambda s: (0, 0)),
            pl.BlockSpec((2, S5_N), lambda s: (0, 0)),
            pl.BlockSpec((SEG, n2), lambda s: (0, 0)),
        ],
        out_specs=pl.BlockSpec((TM, S5_WIDTH), lambda s: (bmap(s), 0)),
        scratch_shapes=[pltpu.VMEM((TM, n2), F32), pltpu.VMEM((SUBLANES, n2), F32),
                        pltpu.VMEM((1, n2), F32)],
        compiler_params=_cp(("arbitrary",)),
        name="s5_bwd" if reverse else "s5_fwd",
    )(hmat, jnp.asarray(p, BF16), jnp.asarray(p.T, BF16), bm_t, cm_t, a_row, pw)


def _s5_params(a_re, a_im, log_dt, b_re, b_im, c_re, c_im):
    g, p = a_re.shape
    dt = jnp.exp(log_dt)[:, None]
    lr, li = a_re, a_im
    ea = jnp.exp(lr * dt)
    ab_r, ab_i = ea * jnp.cos(li * dt), ea * jnp.sin(li * dt)
    den = lr * lr + li * li
    co_r = ((ab_r - 1.0) * lr + ab_i * li) / den
    co_i = (ab_i * lr - (ab_r - 1.0) * li) / den
    bb_r = co_r[..., None] * b_re - co_i[..., None] * b_im
    bb_i = co_r[..., None] * b_im + co_i[..., None] * b_re
    eye = jnp.eye(g, dtype=F32)
    bm_r = jnp.einsum('gpc,gh->gchp', bb_r, eye).reshape(g * S5_GROUP, g * p)
    bm_i = jnp.einsum('gpc,gh->gchp', bb_i, eye).reshape(g * S5_GROUP, g * p)
    bm_t = jnp.concatenate([bm_r, bm_i], axis=1).astype(BF16)
    cm_r = jnp.einsum('gcp,gh->gphc', c_re, eye).reshape(g * p, g * S5_GROUP)
    cm_i = jnp.einsum('gcp,gh->gphc', -c_im, eye).reshape(g * p, g * S5_GROUP)
    cm_t = jnp.concatenate([cm_r, cm_i], axis=0).astype(BF16)
    a_row = jnp.stack([ab_r.reshape(-1), ab_i.reshape(-1)], axis=0)
    k = jnp.arange(1, SEG + 1, dtype=F32)[:, None, None]
    ek = jnp.exp(k * (lr * dt))
    pw = jnp.concatenate([(ek * jnp.cos(k * (li * dt))).reshape(SEG, -1),
                          (ek * jnp.sin(k * (li * dt))).reshape(SEG, -1)], axis=1)
    return bm_t, cm_t, a_row, pw


def _qkv_kernel(q_ref, k_ref, v_ref, cos_ref, sin_ref, qa_ref, qb_ref, ko_ref, vo_ref):
    w = DA_HEADS * 2 * DA_HEAD_DIM
    cos = jnp.concatenate([cos_ref[...]] * (w // LANES), axis=1)
    sin = jnp.concatenate([sin_ref[...]] * (w // LANES), axis=1)
    lane = lax.broadcasted_iota(jnp.int32, (TM, w), 1)
    first = (lane % (2 * ROPE_FREQS)) < ROPE_FREQS

    def rope(x):
        swapped = jnp.where(first, pltpu.roll(x, w - ROPE_FREQS, 1), pltpu.roll(x, ROPE_FREQS, 1))
        return x * cos + swapped * sin

    q = rope(q_ref[...]) * DA_SCALE
    in_a = (lane % (2 * DA_HEAD_DIM)) < DA_HEAD_DIM
    qa_ref[...] = jnp.where(in_a, q, 0.0).astype(BF16)
    qb_ref[...] = jnp.where(in_a, 0.0, q).astype(BF16)
    ko_ref[...] = rope(k_ref[...]).astype(BF16)
    vo_ref[...] = v_ref[...].astype(BF16)


def _qkv_prep(hmat, cos_t, sin_t):
    t = hmat.shape[0]
    w = DA_HEADS * 2 * DA_HEAD_DIM
    col = lambda c: pl.BlockSpec((TM, w), lambda i: (i, c))
    tab = lambda: pl.BlockSpec((TM, LANES), lambda i: (i, 0))
    out = lambda: pl.BlockSpec((TM, w), lambda i: (i, 0))
    sds = jax.ShapeDtypeStruct((t, w), BF16)
    return pl.pallas_call(
        _qkv_kernel,
        out_shape=(sds, sds, sds, sds),
        grid=(t // TM,),
        in_specs=[col(3), col(4), col(5), tab(), tab()],
        out_specs=(out(), out(), out(), out()),
        compiler_params=_cp(("arbitrary",)),
        name="qkv_prep",
    )(hmat, hmat, hmat, cos_t, sin_t)


def _rope_tables(n_lat, n_ctx):
    pos = jnp.arange(n_lat)
    freqs = ROPE_BASE ** (-jnp.arange(ROPE_FREQS, dtype=F32) / ROPE_FREQS)
    ang_r = (pos // GRID_W).astype(F32)[:, None] * freqs
    ang_c = (pos % GRID_W).astype(F32)[:, None] * freqs
    cr, sr, cc, sc = jnp.cos(ang_r), jnp.sin(ang_r), jnp.cos(ang_c), jnp.sin(ang_c)
    cos64 = jnp.concatenate([cr, cr, cc, cc], axis=1)
    sin64 = jnp.concatenate([-sr, sr, -sc, sc], axis=1)
    cos_t = jnp.concatenate([cos64, cos64], axis=1)
    sin_t = jnp.concatenate([sin64, sin64], axis=1)
    cos_t = jnp.concatenate([cos_t, jnp.ones((n_ctx, LANES), F32)], axis=0)
    sin_t = jnp.concatenate([sin_t, jnp.zeros((n_ctx, LANES), F32)], axis=0)
    return cos_t, sin_t


def _attn_kernel(lam_ref, g_ref, qa_ref, qb_ref, k_ref, v_ref, *rest, lam_init, aliased):
    if aliased:
        rest = rest[1:]
    o_ref, m_s, l_s, acc_s = rest
    kj = pl.program_id(2)

    @pl.when(kj == 0)
    def _():
        m_s[...] = jnp.full_like(m_s, -jnp.inf)
        l_s[...] = jnp.zeros_like(l_s)
        acc_s[...] = jnp.zeros_like(acc_s)

    k = k_ref[...]
    v = v_ref[...]
    for i, q_ref in enumerate((qa_ref, qb_ref)):
        s = lax.dot_general(q_ref[...], k, (((1,), (1,)), ((), ())), preferred_element_type=F32)
        m_prev = m_s[i][:, 0:1]
        m_new = jnp.maximum(m_prev, jnp.max(s, axis=1, keepdims=True))
        alpha = jnp.exp(m_prev - m_new)
        p = jnp.exp(s - m_new)
        l_s[i] = jnp.broadcast_to(alpha * l_s[i][:, 0:1] + jnp.sum(p, axis=1, keepdims=True),
                                  l_s.shape[1:])
        acc_s[i] = alpha * acc_s[i] + jnp.dot(p.astype(BF16), v, preferred_element_type=F32)
        m_s[i] = jnp.broadcast_to(m_new, m_s.shape[1:])

    @pl.when(kj == pl.num_programs(2) - 1)
    def _():
        lv = lam_ref[...]
        lam = (jnp.exp(jnp.sum(lv[0:1] * lv[1:2], axis=1, keepdims=True))
               - jnp.exp(jnp.sum(lv[2:3] * lv[3:4], axis=1, keepdims=True)) + lam_init)
        o = acc_s[0] / l_s[0][:, 0:1] - lam * (acc_s[1] / l_s[1][:, 0:1])
        ms = jnp.mean(o * o, axis=1, keepdims=True)
        o_ref[...] = o * lax.rsqrt(ms + RMS_EPS) * g_ref[...] * (1.0 - lam_init)


def _attention(qa, qb, k, v, lam_vec, subln_g, lam_init, *, q0, nq, tq, k0, nk, tk, prev=None):
    t = qa.shape[0]
    aliased = prev is not None
    in_specs = [
        pl.BlockSpec((4, DA_HEAD_DIM), lambda h, i, j: (0, 0)),
        pl.BlockSpec((1, DA_V_DIM), lambda h, i, j: (0, 0)),
        pl.BlockSpec((tq, LANES), lambda h, i, j: (q0 + i, h)),
        pl.BlockSpec((tq, LANES), lambda h, i, j: (q0 + i, h)),
        pl.BlockSpec((tk, LANES), lambda h, i, j: (k0 + j, h)),
        pl.BlockSpec((tk, LANES), lambda h, i, j: (k0 + j, h)),
    ]
    args = [lam_vec, subln_g, qa, qb, k, v]
    if aliased:
        in_specs.append(pl.BlockSpec(memory_space=pl.ANY))
        args.append(prev)
    return pl.pallas_call(
        functools.partial(_attn_kernel, lam_init=lam_init, aliased=aliased),
        out_shape=jax.ShapeDtypeStruct((t, DA_HEADS * DA_V_DIM), F32),
        grid=(DA_HEADS, nq, nk),
        in_specs=in_specs,
        out_specs=pl.BlockSpec((tq, LANES), lambda h, i, j: (q0 + i, h)),
        scratch_shapes=[pltpu.VMEM((2, tq, LANES), F32), pltpu.VMEM((2, tq, LANES), F32),
                        pltpu.VMEM((2, tq, DA_V_DIM), F32)],
        input_output_aliases={6: 0} if aliased else {},
        compiler_params=_cp(("arbitrary", "arbitrary", "arbitrary")),
        name="attn_ctx" if aliased else "attn_lat",
    )(*args)


def _merge_kernel(x_ref, mod_ref, gt_ref, ga_ref, ub_ref, hf_ref, hb_ref, sf_ref, sb_ref, yc_ref,
                  sd_ref, wglu_ref, bglu_ref, wp_ref, bg_ref, wo_ref, lng_ref, lnb_ref,
                  x1_ref, v_ref, *, dn_alpha):
    md = mod_ref[0]
    ya = (hf_ref[...] + hb_ref[...]) * _gelu_tanh(ga_ref[...])
    y = sf_ref[...] + sb_ref[...] + sd_ref[...] * ub_ref[...]
    tg = jnp.dot(_gelu_tanh(y).astype(BF16), wglu_ref[...], preferred_element_type=F32) + bglu_ref[...]
    yb = tg[:, 0:S5_WIDTH] * _sigmoid(tg[:, S5_WIDTH:2 * S5_WIDTH])
    yc = yc_ref[...]
    bg = bg_ref[...]
    z = jnp.zeros((TM, D_MODEL), F32)
    for n, br in enumerate((ya, yb, yc)):
        gate = _sigmoid(gt_ref[:, n * D_MODEL:(n + 1) * D_MODEL] + bg[n:n + 1])
        z = z + gate * jnp.dot(br.astype(BF16), wp_ref[n], preferred_element_type=F32)
    m = jnp.dot(z.astype(BF16), wo_ref[...], preferred_element_type=F32)
    x1 = _layer_norm(dn_alpha * x_ref[...] + md[2:3] * m, lng_ref[...], lnb_ref[...])
    x1_ref[...] = x1
    v_ref[...] = x1 * (1.0 + md[4:5]) + md[3:4]


def _merge(x, modv, hmat, hf, hb, sf, sb, yc, s5_d, w_glu, b_glu, w_proj, b_gate, w_out, ln_g, ln_b,
           n_lat_blocks, dn_alpha):
    t = x.shape[0]
    row = lambda w, c=0: pl.BlockSpec((TM, w), lambda i: (i, c))
    const = lambda *shape: pl.BlockSpec(shape, lambda i: (0,) * len(shape))
    sds = jax.ShapeDtypeStruct((t, D_MODEL), F32)
    return pl.pallas_call(
        functools.partial(_merge_kernel, dn_alpha=dn_alpha),
        out_shape=(sds, sds),
        grid=(t // TM,),
        in_specs=[
            row(D_MODEL),
            pl.BlockSpec((1, 6, D_MODEL), lambda i: (i // n_lat_blocks, 0, 0)),
            row(N_BRANCH * D_MODEL, 1), row(BW, 1), row(BW, 2),
            row(BW), row(BW), row(BW), row(BW), row(BW),
            const(1, S5_WIDTH), const(S5_WIDTH, 2 * S5_WIDTH), const(1, 2 * S5_WIDTH),
            const(N_BRANCH, BW, D_MODEL), const(N_BRANCH, D_MODEL), const(D_MODEL, D_MODEL),
            const(1, D_MODEL), const(1, D_MODEL),
        ],
        out_specs=(row(D_MODEL), row(D_MODEL)),
        compiler_params=_cp(("arbitrary",)),
        name="merge",
    )(x, modv, hmat, hmat, hmat, hf, hb, sf, sb, yc, s5_d, w_glu, b_glu, w_proj, b_gate, w_out,
      ln_g, ln_b)


def _router_kernel(v_ref, w_ref, b_ref, tri_ref, ti_ref, gw_ref, rk_ref, cnt_ref, c_s):
    s = pl.program_id(0)

    @pl.when(s == 0)
    def _():
        c_s[...] = jnp.zeros_like(c_s)

    logits = jnp.dot(v_ref[...], w_ref[...], preferred_element_type=F32, precision=HIGHEST) + b_ref[...]
    lane = lax.broadcasted_iota(jnp.int32, (TM, N_EXPERTS), 1).astype(F32)
    work = logits
    vals, idxs = [], []
    hot = jnp.zeros((TM, N_EXPERTS), F32)
    for _ in range(TOP_K):
        mx = jnp.max(work, axis=1, keepdims=True)
        ix = jnp.min(jnp.where(work == mx, lane, float(N_EXPERTS)), axis=1, keepdims=True)
        sel = lane == ix
        hot = jnp.where(sel, 1.0, hot)
        work = jnp.where(sel, -jnp.inf, work)
        vals.append(mx)
        idxs.append(ix)
    es = [jnp.exp(vv - vals[0]) for vv in vals]
    den = es[0] + es[1] + es[2] + es[3]
    before = jnp.dot(tri_ref[...], hot.astype(BF16), preferred_element_type=F32) + c_s[0:1, :]
    col = lax.broadcasted_iota(jnp.int32, (TM, TOP_K), 1)
    ti = jnp.zeros((TM, TOP_K), F32)
    gw = jnp.zeros((TM, TOP_K), F32)
    rk = jnp.zeros((TM, TOP_K), F32)
    for kk in range(TOP_K):
        rank = jnp.sum(jnp.where(lane == idxs[kk], before, 0.0), axis=1, keepdims=True)
        ti = jnp.where(col == kk, idxs[kk], ti)
        gw = jnp.where(col == kk, es[kk] / den, gw)
        rk = jnp.where(col == kk, rank, rk)
    ti_ref[...] = ti.astype(jnp.int32)
    gw_ref[...] = gw
    rk_ref[...] = rk.astype(jnp.int32)
    tot = c_s[0:1, :] + jnp.sum(hot, axis=0, keepdims=True)
    c_s[...] = jnp.broadcast_to(tot, c_s.shape)
    cnt_ref[...] = jnp.broadcast_to(tot, cnt_ref.shape).astype(jnp.int32)


def _router(v, w_router, b_router):
    t = v.shape[0]
    tri = jnp.asarray(np.tril(np.ones((TM, TM), np.float32), -1), BF16)
    row4 = lambda: pl.BlockSpec((TM, TOP_K), lambda i: (i, 0))
    return pl.pallas_call(
        _router_kernel,
        out_shape=(jax.ShapeDtypeStruct((t, TOP_K), jnp.int32),
                   jax.ShapeDtypeStruct((t, TOP_K), F32),
                   jax.ShapeDtypeStruct((t, TOP_K), jnp.int32),
                   jax.ShapeDtypeStruct((SUBLANES, N_EXPERTS), jnp.int32)),
        grid=(t // TM,),
        in_specs=[
            pl.BlockSpec((TM, D_MODEL), lambda i: (i, 0)),
            pl.BlockSpec((D_MODEL, N_EXPERTS), lambda i: (0, 0)),
            pl.BlockSpec((1, N_EXPERTS), lambda i: (0, 0)),
            pl.BlockSpec((TM, TM), lambda i: (0, 0)),
        ],
        out_specs=(row4(), row4(), row4(), pl.BlockSpec((SUBLANES, N_EXPERTS), lambda i: (0, 0))),
        scratch_shapes=[pltpu.VMEM((SUBLANES, N_EXPERTS), F32)],
        compiler_params=_cp(("arbitrary",)),
        name="router",
    )(v, w_router, b_router, tri)


def _dispatch_kernel(dest_ref, v_hbm, xs_in, xs_hbm, sem):
    del xs_in
    base = pl.program_id(0) * TM

    def row_copy(r, kk):
        return pltpu.make_async_copy(v_hbm.at[pl.ds(base + r, 1)],
                                     xs_hbm.at[pl.ds(dest_ref[r * TOP_K + kk], 1)], sem)

    def start(r, _):
        for kk in range(TOP_K):
            row_copy(r, kk).start()
        return 0

    def wait(r, _):
        for kk in range(TOP_K):
            row_copy(r, kk).wait()
        return 0

    lax.fori_loop(0, TM, start, 0)
    lax.fori_loop(0, TM, wait, 0)


def _dispatch(dest_flat, v, n_slots):
    t = v.shape[0]
    xs0 = jnp.zeros((n_slots, D_MODEL), F32)
    return pl.pallas_call(
        _dispatch_kernel,
        out_shape=jax.ShapeDtypeStruct((n_slots, D_MODEL), F32),
        grid=(t // TM,),
        in_specs=[
            pl.BlockSpec((TM * TOP_K,), lambda i: (i,), memory_space=pltpu.SMEM),
            pl.BlockSpec(memory_space=pl.ANY),
            pl.BlockSpec(memory_space=pl.ANY),
        ],
        out_specs=pl.BlockSpec(memory_space=pl.ANY),
        scratch_shapes=[pltpu.SemaphoreType.DMA],
        input_output_aliases={2: 0},
        compiler_params=_cp(("arbitrary",)),
        name="moe_dispatch",
    )(dest_flat, v, xs0)


def _expert_kernel(be_ref, nu_ref, x_ref, wgu_ref, bgu_ref, wd_ref, bd_ref, o_ref):
    del be_ref

    @pl.when(pl.program_id(0) < nu_ref[0])
    def _():
        h = jnp.dot(x_ref[...].astype(BF16), wgu_ref[0], preferred_element_type=F32) + bgu_ref[0]
        hg = jnp.minimum(h[:, 0:D_FF], SWIGLU_LIMIT)
        hl = jnp.clip(h[:, D_FF:2 * D_FF], -SWIGLU_LIMIT, SWIGLU_LIMIT)
        act = hg * _sigmoid(SWIGLU_ALPHA * hg) * (hl + 1.0)
        o_ref[...] = jnp.dot(act.astype(BF16), wd_ref[0], preferred_element_type=F32) + bd_ref[0]

    @pl.when(pl.program_id(0) >= nu_ref[0])
    def _():
        o_ref[...] = jnp.zeros_like(o_ref)


def _experts(blk_e, n_used, xs, w_gu, b_gu, w_down, b_down):
    n_slots = xs.shape[0]
    nblk = n_slots // MOE_BLK
    grid_spec = pltpu.PrefetchScalarGridSpec(
        num_scalar_prefetch=2,
        grid=(nblk,),
        in_specs=[
            pl.BlockSpec((MOE_BLK, D_MODEL), lambda b, be, nu: (jnp.minimum(b, nu[0] - 1), 0)),
            pl.BlockSpec((1, D_MODEL, 2 * D_FF), lambda b, be, nu: (be[b], 0, 0)),
            pl.BlockSpec((1, 1, 2 * D_FF), lambda b, be, nu: (be[b], 0, 0)),
            pl.BlockSpec((1, D_FF, D_MODEL), lambda b, be, nu: (be[b], 0, 0)),
            pl.BlockSpec((1, 1, D_MODEL), lambda b, be, nu: (be[b], 0, 0)),
        ],
        out_specs=pl.BlockSpec((MOE_BLK, D_MODEL), lambda b, be, nu: (b, 0)),
    )
    return pl.pallas_call(
        _expert_kernel,
        out_shape=jax.ShapeDtypeStruct((n_slots, D_MODEL), F32),
        grid_spec=grid_spec,
        compiler_params=_cp(("arbitrary",)),
        name="moe_experts",
    )(blk_e, n_used, xs, w_gu, b_gu.reshape(N_EXPERTS, 1, 2 * D_FF), w_down,
      b_down.reshape(N_EXPERTS, 1, D_MODEL))


def _combine_kernel(dest_ref, ys_hbm, gw_ref, x_ref, mod_ref, lng_ref, lnb_ref, o_ref, g_s, sem,
                    *, dn_alpha):
    def row_copy(r, kk):
        return pltpu.make_async_copy(ys_hbm.at[pl.ds(dest_ref[r * TOP_K + kk], 1)],
                                     g_s.at[kk, pl.ds(r, 1)], sem)

    def start(r, _):
        for kk in range(TOP_K):
            row_copy(r, kk).start()
        return 0

    def wait(r, _):
        for kk in range(TOP_K):
            row_copy(r, kk).wait()
        return 0

    lax.fori_loop(0, TM, start, 0)
    lax.fori_loop(0, TM, wait, 0)
    gw = gw_ref[...]
    f = jnp.zeros((TM, D_MODEL), F32)
    for kk in range(TOP_K):
        f = f + gw[:, kk:kk + 1] * g_s[kk]
    md = mod_ref[0]
    o_ref[...] = _layer_norm(dn_alpha * x_ref[...] + md[5:6] * f, lng_ref[...], lnb_ref[...])


def _combine(dest_flat, ys, gate_w, x1, modv, ln_g, ln_b, n_lat_blocks, dn_alpha):
    t = x1.shape[0]
    return pl.pallas_call(
        functools.partial(_combine_kernel, dn_alpha=dn_alpha),
        out_shape=jax.ShapeDtypeStruct((t, D_MODEL), F32),
        grid=(t // TM,),
        in_specs=[
            pl.BlockSpec((TM * TOP_K,), lambda i: (i,), memory_space=pltpu.SMEM),
            pl.BlockSpec(memory_space=pl.ANY),
            pl.BlockSpec((TM, TOP_K), lambda i: (i, 0)),
            pl.BlockSpec((TM, D_MODEL), lambda i: (i, 0)),
            pl.BlockSpec((1, 6, D_MODEL), lambda i: (i // n_lat_blocks, 0, 0)),
            pl.BlockSpec((1, D_MODEL), lambda i: (0, 0)),
            pl.BlockSpec((1, D_MODEL), lambda i: (0, 0)),
        ],
        out_specs=pl.BlockSpec((TM, D_MODEL), lambda i: (i, 0)),
        scratch_shapes=[pltpu.VMEM((TOP_K, TM, D_MODEL), F32), pltpu.SemaphoreType.DMA],
        compiler_params=_cp(("arbitrary",)),
        name="moe_combine",
    )(dest_flat, ys, gate_w, x1, modv, ln_g, ln_b)


def _moe(v, x1, modv, w_router, b_router, w_gu, b_gu, w_down, b_down, ln_g, ln_b, n_lat_blocks, dn_alpha):
    t = v.shape[0]
    top_i, gate_w, rank, counts = _router(v, w_router, b_router.reshape(1, N_EXPERTS))
    counts = counts[0]
    padded = (counts + MOE_BLK - 1) // MOE_BLK * MOE_BLK
    pend = jnp.cumsum(padded)
    pstart = pend - padded
    dest = (pstart[top_i] + rank).reshape(-1).astype(jnp.int32)
    n_slots = t * TOP_K + N_EXPERTS * MOE_BLK
    nblk = n_slots // MOE_BLK
    blk_e = jnp.clip(jnp.searchsorted(pend, jnp.arange(nblk, dtype=jnp.int32) * MOE_BLK, side='right'),
                     0, N_EXPERTS - 1).astype(jnp.int32)
    n_used = (pend[-1:] // MOE_BLK).astype(jnp.int32)
    xs = _dispatch(dest, v, n_slots)
    ys = _experts(blk_e, n_used, xs, w_gu, b_gu, w_down, b_down)
    return _combine(dest, ys, gate_w, x1, modv, ln_g, ln_b, n_lat_blocks, dn_alpha)


def _block_diag(w):
    nbk, bs, _ = w.shape
    eye = jnp.eye(nbk, dtype=w.dtype)
    return jnp.einsum('hij,hg->higj', w, eye).reshape(nbk * bs, nbk * bs)


def _key_block(t):
    for tk in (1280, 640, 256):
        if t % tk == 0:
            return tk
    raise ValueError("token count must be a multiple of 256")


def kernel(x, c, ctx, c_ctx, w_mod, b_mod, w_in, conv_w, conv_b, lru_wr, lru_br, lru_wi, lru_bi, lru_lam, s5_a_re, s5_a_im, s5_log_dt, s5_b_re, s5_b_im, s5_c_re, s5_c_im, s5_d, s5_w_glu, s5_b_glu, da_lam, da_subln_g, w_proj, b_gate, w_out, ln_g, ln_b, w_router, b_router, w_gu, b_gu, w_down, b_down):
    bsz, n_lat, d = x.shape
    n_ctx = ctx.shape[1]
    depth = w_mod.shape[0]
    assert bsz == 1 and d == D_MODEL and n_ctx == TM and n_lat % TM == 0 and n_lat % GRID_W == 0
    nb = n_lat // TM
    t = n_lat + n_ctx
    dn_alpha = (2 * depth) ** 0.25

    cc = jnp.zeros((SUBLANES, D_MODEL), F32).at[0].set(c[0]).at[1].set(c_ctx)
    mod = _modulation(cc, w_mod, b_mod)
    cos_t, sin_t = _rope_tables(n_lat, n_ctx)
    tq = 512 if n_lat % 512 == 0 else TM
    tk = _key_block(t)

    xs = jnp.concatenate([x[0], ctx[0]], axis=0)
    for l in range(depth):
        modv = mod[l, 0:2].reshape(2, 6, D_MODEL)
        hmat = _inproj(xs, modv, w_in[l].astype(BF16), nb)

        sp = jax.nn.softplus(-lru_lam[l])
        hdir, sdir = [], []
        for dr in range(2):
            hdir.append(_lru(hmat, nb, dr == 1, conv_w[l], conv_b[l].reshape(1, -1),
                             _block_diag(lru_wr[l, dr]).astype(BF16), lru_br[l, dr].reshape(1, -1),
                             _block_diag(lru_wi[l, dr]).astype(BF16), lru_bi[l, dr].reshape(1, -1),
                             sp[dr].reshape(1, -1)))
            prm = _s5_params(s5_a_re[l, dr], s5_a_im[l, dr], s5_log_dt[l, dr], s5_b_re[l, dr],
                             s5_b_im[l, dr], s5_c_re[l, dr], s5_c_im[l, dr])
            sdir.append(_s5(hmat, nb, dr == 1, *prm))

        qa, qb, kk, vv = _qkv_prep(hmat, cos_t, sin_t)
        lam_init = 0.8 - 0.6 * math.exp(-0.3 * l)
        g_row = da_subln_g[l].reshape(1, DA_V_DIM)
        yc = _attention(qa, qb, kk, vv, da_lam[l], g_row, lam_init,
                        q0=0, nq=n_lat // tq, tq=tq, k0=0, nk=t // tk, tk=tk)
        yc = _attention(qa, qb, kk, vv, da_lam[l], g_row, lam_init,
                        q0=nb, nq=1, tq=TM, k0=nb, nk=1, tk=TM, prev=yc)

        x1, v = _merge(xs, modv, hmat, hdir[0], hdir[1], sdir[0], sdir[1], yc,
                       s5_d[l].reshape(1, -1), s5_w_glu[l].astype(BF16), s5_b_glu[l].reshape(1, -1),
                       w_proj[l].astype(BF16), b_gate[l], w_out[l].astype(BF16),
                       ln_g[l, 0].reshape(1, -1), ln_b[l, 0].reshape(1, -1), nb, dn_alpha)
        xs = _moe(v, x1, modv, w_router[l], b_router[l], w_gu[l].astype(BF16), b_gu[l],
                  w_down[l].astype(BF16), b_down[l], ln_g[l, 1].reshape(1, -1),
                  ln_b[l, 1].reshape(1, -1), nb, dn_alpha)
    return xs[:n_lat][None]
```

```python
import functools
import math

import numpy as np
import jax
import jax.numpy as jnp
from jax import lax
from jax.experimental import pallas as pl
from jax.experimental.pallas import tpu as pltpu

F32 = jnp.float32
BF16 = jnp.bfloat16
HIGHEST = lax.Precision.HIGHEST

D_MODEL = 1024
GRID_W = 64
LRU_WIDTH = 512
LRU_BLOCKS = 8
CONV_W = 4
LRU_C = 8.0
S5_WIDTH = 512
S5_GROUP = 16
S5_GROUPS = S5_WIDTH // S5_GROUP
S5_STATE = 64
S5_N = S5_GROUPS * S5_STATE
DA_HEADS = 4
DA_HEAD_DIM = 64
DA_V_DIM = 128
DA_SCALE = DA_HEAD_DIM ** -0.5
ROPE_BASE = 10000.0
ROPE_FREQS = DA_HEAD_DIM // 4
N_BRANCH = 3
BW = 512
IN_COLS = 6 * BW + N_BRANCH * D_MODEL
N_EXPERTS = 32
TOP_K = 4
D_FF = D_MODEL
SWIGLU_ALPHA = 1.702
SWIGLU_LIMIT = 7.0
LN_EPS = 1e-5
RMS_EPS = 1e-6

SUBLANES = 8
LANES = 128
VMEM_LIMIT = 56 * 1024 * 1024

TM = 256
SEG = TM // SUBLANES
LOG2_E = math.log2(math.e)
MOE_BLK = 256
TN_IN = 1536


def _cp(sem, vmem=VMEM_LIMIT):
    return pltpu.CompilerParams(dimension_semantics=sem, vmem_limit_bytes=vmem)


def _sigmoid(x):
    return 1.0 / (1.0 + jnp.exp(-x))


def _gelu_tanh(x):
    return 0.5 * x * (1.0 + jnp.tanh(math.sqrt(2.0 / math.pi) * (x + 0.044715 * (x * x * x))))


def _layer_norm(y, g, b):
    mu = jnp.mean(y, axis=-1, keepdims=True)
    yc = y - mu
    var = jnp.mean(yc * yc, axis=-1, keepdims=True)
    return yc * lax.rsqrt(var + LN_EPS) * g + b


def _mod_kernel(c_ref, w_ref, b_ref, o_ref):
    c = c_ref[...]
    s = c * _sigmoid(c)
    o_ref[0] = jnp.dot(s, w_ref[0], preferred_element_type=F32, precision=HIGHEST) + b_ref[0]


def _modulation(cc, w_mod, b_mod):
    depth = w_mod.shape[0]
    tn = 1536
    return pl.pallas_call(
        _mod_kernel,
        out_shape=jax.ShapeDtypeStruct((depth, SUBLANES, 6 * D_MODEL), F32),
        grid=(depth, 6 * D_MODEL // tn),
        in_specs=[
            pl.BlockSpec((SUBLANES, D_MODEL), lambda l, j: (0, 0)),
            pl.BlockSpec((1, D_MODEL, tn), lambda l, j: (l, 0, j)),
            pl.BlockSpec((1, 1, tn), lambda l, j: (l, 0, j)),
        ],
        out_specs=pl.BlockSpec((1, SUBLANES, tn), lambda l, j: (l, 0, j)),
        compiler_params=_cp(("arbitrary", "arbitrary")),
        name="modulation",
    )(cc, w_mod, b_mod.reshape(depth, 1, 6 * D_MODEL))


def _inproj_kernel(x_ref, mod_ref, w_ref, o_ref):
    md = mod_ref[0]
    u = x_ref[...] * (1.0 + md[1:2]) + md[0:1]
    o_ref[...] = jnp.dot(u.astype(BF16), w_ref[...], preferred_element_type=F32)


def _inproj(x, modv, w_bf16, n_lat_blocks):
    t = x.shape[0]
    return pl.pallas_call(
        _inproj_kernel,
        out_shape=jax.ShapeDtypeStruct((t, IN_COLS), F32),
        grid=(IN_COLS // TN_IN, t // TM),
        in_specs=[
            pl.BlockSpec((TM, D_MODEL), lambda j, i: (i, 0)),
            pl.BlockSpec((1, 6, D_MODEL), lambda j, i: (i // n_lat_blocks, 0, 0)),
            pl.BlockSpec((D_MODEL, TN_IN), lambda j, i: (0, j)),
        ],
        out_specs=pl.BlockSpec((TM, TN_IN), lambda j, i: (i, j)),
        compiler_params=_cp(("arbitrary", "arbitrary")),
        name="inproj",
    )(x, modv, w_bf16)


def _scan_block(s, nb, reverse):
    return (nb - s) if reverse else (s + nb) % (nb + 1)


def _lru_kernel(prev_ref, cur_ref, next_ref, cw_ref, cb_ref, wr_ref, br_ref, wi_ref, bi_ref,
                sp_ref, o_ref, a_s, b_s, h_s, *, nb, reverse):
    s = pl.program_id(0)
    blk = _scan_block(s, nb, reverse)

    @pl.when(s == 0)
    def _():
        h_s[...] = jnp.zeros_like(h_s)

    has_prev = jnp.logical_and(blk != 0, blk != nb)
    has_next = jnp.logical_and(blk != nb - 1, blk != nb)
    prev = jnp.where(has_prev, prev_ref[...], 0.0)
    nxt = jnp.where(has_next, next_ref[...], 0.0)
    ext = jnp.concatenate([prev, cur_ref[...], nxt], axis=0)
    cw = cw_ref[...]
    xc = cb_ref[...] + jnp.zeros((TM, LRU_WIDTH), F32)
    for k in range(CONV_W):
        off = SUBLANES - CONV_W // 2 + k
        xc = xc + ext[off:off + TM] * cw[k:k + 1]
    xb = xc.astype(BF16)
    r = _sigmoid(jnp.dot(xb, wr_ref[...], preferred_element_type=F32) + br_ref[...])
    i = _sigmoid(jnp.dot(xb, wi_ref[...], preferred_element_type=F32) + bi_ref[...])
    a = jnp.exp(-LRU_C * r * sp_ref[...])
    a_s[...] = a
    b_s[...] = jnp.sqrt(1.0 - a * a) * (i * xc)

    def step(n, h):
        t = (TM - 1 - n) if reverse else n
        h = a_s[pl.ds(t, 1), :] * h + b_s[pl.ds(t, 1), :]
        o_ref[pl.ds(t, 1), :] = h
        return h

    h_s[...] = lax.fori_loop(0, TM, step, h_s[...], unroll=8)


def _lru(hmat, nb, reverse, conv_w, conv_b, wr, br, wi, bi, sp):
    t = hmat.shape[0]
    nblk = nb + 1
    r8 = TM // SUBLANES
    last8 = t // SUBLANES - 1

    def bmap(s):
        return _scan_block(s, nb, reverse)

    vec = lambda: pl.BlockSpec((1, LRU_WIDTH), lambda s: (0, 0))
    mat = lambda: pl.BlockSpec((LRU_WIDTH, LRU_WIDTH), lambda s: (0, 0))
    return pl.pallas_call(
        functools.partial(_lru_kernel, nb=nb, reverse=reverse),
        out_shape=jax.ShapeDtypeStruct((t, LRU_WIDTH), F32),
        grid=(nblk,),
        in_specs=[
            pl.BlockSpec((SUBLANES, LRU_WIDTH), lambda s: (jnp.maximum(bmap(s) * r8 - 1, 0), 0)),
            pl.BlockSpec((TM, LRU_WIDTH), lambda s: (bmap(s), 0)),
            pl.BlockSpec((SUBLANES, LRU_WIDTH), lambda s: (jnp.minimum((bmap(s) + 1) * r8, last8), 0)),
            pl.BlockSpec((CONV_W, LRU_WIDTH), lambda s: (0, 0)),
            vec(), mat(), vec(), mat(), vec(), vec(),
        ],
        out_specs=pl.BlockSpec((TM, LRU_WIDTH), lambda s: (bmap(s), 0)),
        scratch_shapes=[pltpu.VMEM((TM, LRU_WIDTH), F32), pltpu.VMEM((TM, LRU_WIDTH), F32),
                        pltpu.VMEM((1, LRU_WIDTH), F32)],
        compiler_params=_cp(("arbitrary",)),
        name="lru_bwd" if reverse else "lru_fwd",
    )(hmat, hmat, hmat, conv_w, conv_b, wr, br, wi, bi, sp)


def _cmul(ar, ai, br, bi):
    return ar * br - ai * bi, ar * bi + ai * br


def _s5_kernel(u_ref, p_ref, pt_ref, bm_ref, cm_ref, a_ref, pw_ref, o_ref, x_s, st_s, c_s):
    s = pl.program_id(0)
    n = S5_N

    @pl.when(s == 0)
    def _():
        c_s[...] = jnp.zeros_like(c_s)

    up = jnp.dot(p_ref[...], u_ref[...].astype(BF16), preferred_element_type=F32).astype(BF16)
    x_s[...] = jnp.dot(up, bm_ref[...], preferred_element_type=F32)
    ar = a_ref[0:1, :]
    ai = a_ref[1:2, :]

    def local(m, carry):
        hr, hi = carry
        row = pl.multiple_of(m * SUBLANES, SUBLANES)
        pr, pi = _cmul(ar, ai, hr, hi)
        hr = pr + x_s[pl.ds(row, SUBLANES), 0:n]
        hi = pi + x_s[pl.ds(row, SUBLANES), n:2 * n]
        x_s[pl.ds(row, SUBLANES), 0:n] = hr
        x_s[pl.ds(row, SUBLANES), n:2 * n] = hi
        return hr, hi

    z = jnp.zeros((SUBLANES, n), F32)
    fr, fi = lax.fori_loop(0, SEG, local, (z, z))

    qr = pw_ref[SEG - 1:SEG, 0:n]
    qi = pw_ref[SEG - 1:SEG, n:2 * n]
    sr = c_s[0:1, 0:n]
    si = c_s[0:1, n:2 * n]
    for j in range(SUBLANES):
        st_s[j:j + 1, 0:n] = sr
        st_s[j:j + 1, n:2 * n] = si
        pr, pi = _cmul(qr, qi, sr, si)
        sr = pr + fr[j:j + 1]
        si = pi + fi[j:j + 1]
    c_s[0:1, 0:n] = sr
    c_s[0:1, n:2 * n] = si

    str_ = st_s[:, 0:n]
    sti = st_s[:, n:2 * n]

    def fix(m, _):
        row = pl.multiple_of(m * SUBLANES, SUBLANES)
        pr, pi = _cmul(pw_ref[pl.ds(m, 1), 0:n], pw_ref[pl.ds(m, 1), n:2 * n], str_, sti)
        x_s[pl.ds(row, SUBLANES), 0:n] = x_s[pl.ds(row, SUBLANES), 0:n] + pr
        x_s[pl.ds(row, SUBLANES), n:2 * n] = x_s[pl.ds(row, SUBLANES), n:2 * n] + pi
        return 0

    lax.fori_loop(0, SEG, fix, 0)

    yp = jnp.dot(x_s[...].astype(BF16), cm_ref[...], preferred_element_type=F32)
    y_hi = yp.astype(BF16)
    y_lo = (yp - y_hi.astype(F32)).astype(BF16)
    pt = pt_ref[...]
    o_ref[...] = (jnp.dot(pt, y_hi, preferred_element_type=F32)
                  + jnp.dot(pt, y_lo, preferred_element_type=F32))


def _s5_perm(reverse):
    p = np.zeros((TM, TM), np.float32)
    for m in range(SEG):
        for j in range(SUBLANES):
            pos = j * SEG + m
            p[m * SUBLANES + j, (TM - 1 - pos) if reverse else pos] = 1.0
    return p


def _s5(hmat, nb, reverse, bm_t, cm_t, a_row, pw):
    t = hmat.shape[0]
    p = _s5_perm(reverse)
    n2 = 2 * S5_N

    def bmap(s):
        return _scan_block(s, nb, reverse)

    return pl.pallas_call(
        _s5_kernel,
        out_shape=jax.ShapeDtypeStruct((t, S5_WIDTH), F32),
        grid=(nb + 1,),
        in_specs=[
            pl.BlockSpec((TM, S5_WIDTH), lambda s: (bmap(s), 2)),
            pl.BlockSpec((TM, TM), lambda s: (0, 0)),
            pl.BlockSpec((TM, TM), lambda s: (0, 0)),
            pl.BlockSpec((S5_WIDTH, n2), lambda s: (0, 0)),
            pl.BlockSpec((n2, S5_WIDTH), lambda s: (0, 0)),
            pl.BlockSpec((2, S5_N), lambda s: (0, 0)),
            pl.BlockSpec((SEG, n2), lambda s: (0, 0)),
        ],
        out_specs=pl.BlockSpec((TM, S5_WIDTH), lambda s: (bmap(s), 0)),
        scratch_shapes=[pltpu.VMEM((TM, n2), F32), pltpu.VMEM((SUBLANES, n2), F32),
                        pltpu.VMEM((1, n2), F32)],
        compiler_params=_cp(("arbitrary",)),
        name="s5_bwd" if reverse else "s5_fwd",
    )(hmat, jnp.asarray(p, BF16), jnp.asarray(p.T, BF16), bm_t, cm_t, a_row, pw)


def _s5_params(a_re, a_im, log_dt, b_re, b_im, c_re, c_im):
    g, p = a_re.shape
    dt = jnp.exp(log_dt)[:, None]
    lr, li = a_re, a_im
    ea = jnp.exp(lr * dt)
    ab_r, ab_i = ea * jnp.cos(li * dt), ea * jnp.sin(li * dt)
    den = lr * lr + li * li
    co_r = ((ab_r - 1.0) * lr + ab_i * li) / den
    co_i = (ab_i * lr - (ab_r - 1.0) * li) / den
    bb_r = co_r[..., None] * b_re - co_i[..., None] * b_im
    bb_i = co_r[..., None] * b_im + co_i[..., None] * b_re
    eye = jnp.eye(g, dtype=F32)
    bm_r = jnp.einsum('gpc,gh->gchp', bb_r, eye).reshape(g * S5_GROUP, g * p)
    bm_i = jnp.einsum('gpc,gh->gchp', bb_i, eye).reshape(g * S5_GROUP, g * p)
    bm_t = jnp.concatenate([bm_r, bm_i], axis=1).astype(BF16)
    cm_r = jnp.einsum('gcp,gh->gphc', c_re, eye).reshape(g * p, g * S5_GROUP)
    cm_i = jnp.einsum('gcp,gh->gphc', -c_im, eye).reshape(g * p, g * S5_GROUP)
    cm_t = jnp.concatenate([cm_r, cm_i], axis=0).astype(BF16)
    a_row = jnp.stack([ab_r.reshape(-1), ab_i.reshape(-1)], axis=0)
    k = jnp.arange(1, SEG + 1, dtype=F32)[:, None, None]
    ek = jnp.exp(k * (lr * dt))
    pw = jnp.concatenate([(ek * jnp.cos(k * (li * dt))).reshape(SEG, -1),
                          (ek * jnp.sin(k * (li * dt))).reshape(SEG, -1)], axis=1)
    return bm_t, cm_t, a_row, pw


def _qkv_kernel(q_ref, k_ref, v_ref, cos_ref, sin_ref, qa_ref, qb_ref, ko_ref, vo_ref):
    w = DA_HEADS * 2 * DA_HEAD_DIM
    cos = jnp.concatenate([cos_ref[...]] * (w // LANES), axis=1)
    sin = jnp.concatenate([sin_ref[...]] * (w // LANES), axis=1)
    lane = lax.broadcasted_iota(jnp.int32, (TM, w), 1)
    first = (lane % (2 * ROPE_FREQS)) < ROPE_FREQS

    def rope(x):
        swapped = jnp.where(first, pltpu.roll(x, w - ROPE_FREQS, 1), pltpu.roll(x, ROPE_FREQS, 1))
        return x * cos + swapped * sin

    q = rope(q_ref[...]) * (DA_SCALE * LOG2_E)
    in_a = (lane % (2 * DA_HEAD_DIM)) < DA_HEAD_DIM
    qa_ref[...] = jnp.where(in_a, q, 0.0).astype(BF16)
    qb_ref[...] = jnp.where(in_a, 0.0, q).astype(BF16)
    ko_ref[...] = rope(k_ref[...]).astype(BF16)
    v = v_ref[...].astype(BF16)
    ones = jnp.ones((TM, DA_V_DIM), BF16)
    vo_ref[...] = jnp.concatenate(
        [blk for h in range(DA_HEADS) for blk in (v[:, h * DA_V_DIM:(h + 1) * DA_V_DIM], ones)], axis=1)


def _qkv_prep(hmat, cos_t, sin_t):
    t = hmat.shape[0]
    w = DA_HEADS * 2 * DA_HEAD_DIM
    col = lambda c: pl.BlockSpec((TM, w), lambda i: (i, c))
    tab = lambda: pl.BlockSpec((TM, LANES), lambda i: (i, 0))
    out = lambda: pl.BlockSpec((TM, w), lambda i: (i, 0))
    sds = jax.ShapeDtypeStruct((t, w), BF16)
    return pl.pallas_call(
        _qkv_kernel,
        out_shape=(sds, sds, sds, jax.ShapeDtypeStruct((t, 2 * w), BF16)),
        grid=(t // TM,),
        in_specs=[col(3), col(4), col(5), tab(), tab()],
        out_specs=(out(), out(), out(), pl.BlockSpec((TM, 2 * w), lambda i: (i, 0))),
        compiler_params=_cp(("arbitrary",)),
        name="qkv_prep",
    )(hmat, hmat, hmat, cos_t, sin_t)


def _rope_tables(n_lat, n_ctx):
    pos = jnp.arange(n_lat)
    freqs = ROPE_BASE ** (-jnp.arange(ROPE_FREQS, dtype=F32) / ROPE_FREQS)
    ang_r = (pos // GRID_W).astype(F32)[:, None] * freqs
    ang_c = (pos % GRID_W).astype(F32)[:, None] * freqs
    cr, sr, cc, sc = jnp.cos(ang_r), jnp.sin(ang_r), jnp.cos(ang_c), jnp.sin(ang_c)
    cos64 = jnp.concatenate([cr, cr, cc, cc], axis=1)
    sin64 = jnp.concatenate([-sr, sr, -sc, sc], axis=1)
    cos_t = jnp.concatenate([cos64, cos64], axis=1)
    sin_t = jnp.concatenate([sin64, sin64], axis=1)
    cos_t = jnp.concatenate([cos_t, jnp.ones((n_ctx, LANES), F32)], axis=0)
    sin_t = jnp.concatenate([sin_t, jnp.zeros((n_ctx, LANES), F32)], axis=0)
    return cos_t, sin_t


def _attn_kernel(lam_ref, g_ref, qa_ref, qb_ref, k_ref, v_ref, prev_ref, o_ref, m_s, acc_s, a_s, s_s,
                 *, lam_init):
    del prev_ref
    kj = pl.program_id(2)

    @pl.when(jnp.logical_and(jnp.logical_and(pl.program_id(0) == 0, pl.program_id(1) == 0), kj == 0))
    def _():
        s_s[...] = jnp.zeros_like(s_s)
        acc_s[...] = jnp.zeros_like(acc_s)
        a_s[...] = jnp.zeros_like(a_s)

    @pl.when(kj == 0)
    def _():
        m_s[...] = jnp.full_like(m_s, -jnp.inf)

    def step(cur):
        k = k_ref[...]
        v = v_ref[...]
        for i, q_ref in enumerate((qa_ref, qb_ref)):
            m_old = m_s[i]
            p = jnp.exp2(s_s[1 - cur, i] - m_old[:, 0:1]).astype(BF16)
            alpha = a_s[i]
            acc_s[i] = (jnp.concatenate([alpha, alpha], axis=1) * acc_s[i]
                        + jnp.dot(p, v, preferred_element_type=F32))
            s = lax.dot_general(q_ref[...], k, (((1,), (1,)), ((), ())), preferred_element_type=F32)
            s_s[cur, i] = s
            m_new = jnp.maximum(m_old, jnp.max(s, axis=1, keepdims=True))
            a_s[i] = jnp.exp2(m_old - m_new)
            m_s[i] = m_new

    for parity in range(2):
        pl.when(kj % 2 == parity)(functools.partial(step, parity))

    @pl.when(kj == 0)
    def _():
        acc_s[...] = jnp.zeros_like(acc_s)

    @pl.when(kj == pl.num_programs(2) - 1)
    def _():
        lv = lam_ref[...]
        lam = (jnp.exp(jnp.sum(lv[0:1] * lv[1:2], axis=1, keepdims=True))
               - jnp.exp(jnp.sum(lv[2:3] * lv[3:4], axis=1, keepdims=True)) + lam_init)
        a0, a1 = acc_s[0], acc_s[1]
        o = (a0[:, 0:DA_V_DIM] / a0[:, DA_V_DIM:2 * DA_V_DIM]
             - lam * (a1[:, 0:DA_V_DIM] / a1[:, DA_V_DIM:2 * DA_V_DIM]))
        ms = jnp.mean(o * o, axis=1, keepdims=True)
        o_ref[...] = o * lax.rsqrt(ms + RMS_EPS) * g_ref[...] * (1.0 - lam_init)


def _attention(qa, qb, k, v, lam_vec, subln_g, lam_init, prev, *, q0, nq, tq, k0, nk, tk, name):
    t = qa.shape[0]
    return pl.pallas_call(
        functools.partial(_attn_kernel, lam_init=lam_init),
        out_shape=jax.ShapeDtypeStruct((t, DA_HEADS * DA_V_DIM), F32),
        grid=(DA_HEADS, nq, nk + 1),
        in_specs=[
            pl.BlockSpec((4, DA_HEAD_DIM), lambda h, i, j: (0, 0)),
            pl.BlockSpec((1, DA_V_DIM), lambda h, i, j: (0, 0)),
            pl.BlockSpec((tq, LANES), lambda h, i, j: (q0 + i, h)),
            pl.BlockSpec((tq, LANES), lambda h, i, j: (q0 + i, h)),
            pl.BlockSpec((tk, LANES), lambda h, i, j: (k0 + jnp.minimum(j, nk - 1), h)),
            pl.BlockSpec((tk, 2 * DA_V_DIM), lambda h, i, j: (k0 + jnp.maximum(j - 1, 0), h)),
            pl.BlockSpec(memory_space=pl.ANY),
        ],
        out_specs=pl.BlockSpec((tq, LANES), lambda h, i, j: (q0 + i, h)),
        scratch_shapes=[pltpu.VMEM((2, tq, LANES), F32), pltpu.VMEM((2, tq, 2 * DA_V_DIM), F32),
                        pltpu.VMEM((2, tq, LANES), F32), pltpu.VMEM((2, 2, tq, tk), F32)],
        input_output_aliases={6: 0},
        compiler_params=_cp(("arbitrary", "arbitrary", "arbitrary")),
        name=name,
    )(lam_vec, subln_g, qa, qb, k, v, prev)


def _merge_kernel(x_ref, mod_ref, gt_ref, ga_ref, ub_ref, hf_ref, hb_ref, sf_ref, sb_ref, yc_ref,
                  sd_ref, wglu_ref, bglu_ref, wp_ref, bg_ref, wo_ref, lng_ref, lnb_ref,
                  x1_ref, v_ref, *, dn_alpha):
    md = mod_ref[0]
    ya = (hf_ref[...] + hb_ref[...]) * _gelu_tanh(ga_ref[...])
    y = sf_ref[...] + sb_ref[...] + sd_ref[...] * ub_ref[...]
    tg = jnp.dot(_gelu_tanh(y).astype(BF16), wglu_ref[...], preferred_element_type=F32) + bglu_ref[...]
    yb = tg[:, 0:S5_WIDTH] * _sigmoid(tg[:, S5_WIDTH:2 * S5_WIDTH])
    yc = yc_ref[...]
    bg = bg_ref[...]
    z = jnp.zeros((TM, D_MODEL), F32)
    for n, br in enumerate((ya, yb, yc)):
        gate = _sigmoid(gt_ref[:, n * D_MODEL:(n + 1) * D_MODEL] + bg[n:n + 1])
        z = z + gate * jnp.dot(br.astype(BF16), wp_ref[n], preferred_element_type=F32)
    m = jnp.dot(z.astype(BF16), wo_ref[...], preferred_element_type=F32)
    x1 = _layer_norm(dn_alpha * x_ref[...] + md[2:3] * m, lng_ref[...], lnb_ref[...])
    x1_ref[...] = x1
    v_ref[...] = x1 * (1.0 + md[4:5]) + md[3:4]


def _merge(x, modv, hmat, hf, hb, sf, sb, yc, s5_d, w_glu, b_glu, w_proj, b_gate, w_out, ln_g, ln_b,
           n_lat_blocks, dn_alpha):
    t = x.shape[0]
    row = lambda w, c=0: pl.BlockSpec((TM, w), lambda i: (i, c))
    const = lambda *shape: pl.BlockSpec(shape, lambda i: (0,) * len(shape))
    sds = jax.ShapeDtypeStruct((t, D_MODEL), F32)
    return pl.pallas_call(
        functools.partial(_merge_kernel, dn_alpha=dn_alpha),
        out_shape=(sds, sds),
        grid=(t // TM,),
        in_specs=[
            row(D_MODEL),
            pl.BlockSpec((1, 6, D_MODEL), lambda i: (i // n_lat_blocks, 0, 0)),
            row(N_BRANCH * D_MODEL, 1), row(BW, 1), row(BW, 2),
            row(BW), row(BW), row(BW), row(BW), row(BW),
            const(1, S5_WIDTH), const(S5_WIDTH, 2 * S5_WIDTH), const(1, 2 * S5_WIDTH),
            const(N_BRANCH, BW, D_MODEL), const(N_BRANCH, D_MODEL), const(D_MODEL, D_MODEL),
            const(1, D_MODEL), const(1, D_MODEL),
        ],
        out_specs=(row(D_MODEL), row(D_MODEL)),
        compiler_params=_cp(("arbitrary",)),
        name="merge",
    )(x, modv, hmat, hmat, hmat, hf, hb, sf, sb, yc, s5_d, w_glu, b_glu, w_proj, b_gate, w_out,
      ln_g, ln_b)


def _router_kernel(v_ref, w_ref, b_ref, tri_ref, ti_ref, gw_ref, rk_ref, cnt_ref, c_s):
    s = pl.program_id(0)

    @pl.when(s == 0)
    def _():
        c_s[...] = jnp.zeros_like(c_s)

    logits = jnp.dot(v_ref[...], w_ref[...], preferred_element_type=F32, precision=HIGHEST) + b_ref[...]
    lane = lax.broadcasted_iota(jnp.int32, (TM, N_EXPERTS), 1).astype(F32)
    work = logits
    vals, idxs = [], []
    hot = jnp.zeros((TM, N_EXPERTS), F32)
    for _ in range(TOP_K):
        mx = jnp.max(work, axis=1, keepdims=True)
        ix = jnp.min(jnp.where(work == mx, lane, float(N_EXPERTS)), axis=1, keepdims=True)
        sel = lane == ix
        hot = jnp.where(sel, 1.0, hot)
        work = jnp.where(sel, -jnp.inf, work)
        vals.append(mx)
        idxs.append(ix)
    es = [jnp.exp(vv - vals[0]) for vv in vals]
    den = es[0] + es[1] + es[2] + es[3]
    before = jnp.dot(tri_ref[...], hot.astype(BF16), preferred_element_type=F32) + c_s[0:1, :]
    col = lax.broadcasted_iota(jnp.int32, (TM, TOP_K), 1)
    ti = jnp.zeros((TM, TOP_K), F32)
    gw = jnp.zeros((TM, TOP_K), F32)
    rk = jnp.zeros((TM, TOP_K), F32)
    for kk in range(TOP_K):
        rank = jnp.sum(jnp.where(lane == idxs[kk], before, 0.0), axis=1, keepdims=True)
        ti = jnp.where(col == kk, idxs[kk], ti)
        gw = jnp.where(col == kk, es[kk] / den, gw)
        rk = jnp.where(col == kk, rank, rk)
    ti_ref[...] = ti.astype(jnp.int32)
    gw_ref[...] = gw
    rk_ref[...] = rk.astype(jnp.int32)
    tot = c_s[0:1, :] + jnp.sum(hot, axis=0, keepdims=True)
    c_s[...] = jnp.broadcast_to(tot, c_s.shape)
    cnt_ref[...] = jnp.broadcast_to(tot, cnt_ref.shape).astype(jnp.int32)


def _router(v, w_router, b_router):
    t = v.shape[0]
    tri = jnp.asarray(np.tril(np.ones((TM, TM), np.float32), -1), BF16)
    row4 = lambda: pl.BlockSpec((TM, TOP_K), lambda i: (i, 0))
    return pl.pallas_call(
        _router_kernel,
        out_shape=(jax.ShapeDtypeStruct((t, TOP_K), jnp.int32),
                   jax.ShapeDtypeStruct((t, TOP_K), F32),
                   jax.ShapeDtypeStruct((t, TOP_K), jnp.int32),
                   jax.ShapeDtypeStruct((SUBLANES, N_EXPERTS), jnp.int32)),
        grid=(t // TM,),
        in_specs=[
            pl.BlockSpec((TM, D_MODEL), lambda i: (i, 0)),
            pl.BlockSpec((D_MODEL, N_EXPERTS), lambda i: (0, 0)),
            pl.BlockSpec((1, N_EXPERTS), lambda i: (0, 0)),
            pl.BlockSpec((TM, TM), lambda i: (0, 0)),
        ],
        out_specs=(row4(), row4(), row4(), pl.BlockSpec((SUBLANES, N_EXPERTS), lambda i: (0, 0))),
        scratch_shapes=[pltpu.VMEM((SUBLANES, N_EXPERTS), F32)],
        compiler_params=_cp(("arbitrary",)),
        name="router",
    )(v, w_router, b_router, tri)


def _dispatch_kernel(dest_ref, v_ref, xs_in, xs_hbm, sem):
    del xs_in

    def row_copy(r, kk):
        return pltpu.make_async_copy(v_ref.at[pl.ds(r, 1)],
                                     xs_hbm.at[pl.ds(dest_ref[r * TOP_K + kk], 1)], sem)

    def start(r, _):
        for kk in range(TOP_K):
            row_copy(r, kk).start()
        return 0

    def wait(r, _):
        for kk in range(TOP_K):
            row_copy(r, kk).wait()
        return 0

    lax.fori_loop(0, TM, start, 0)
    lax.fori_loop(0, TM, wait, 0)


def _dispatch(dest_flat, v, n_slots):
    t = v.shape[0]
    xs0 = jnp.zeros((n_slots, D_MODEL), F32)
    return pl.pallas_call(
        _dispatch_kernel,
        out_shape=jax.ShapeDtypeStruct((n_slots, D_MODEL), F32),
        grid=(t // TM,),
        in_specs=[
            pl.BlockSpec((TM * TOP_K,), lambda i: (i,), memory_space=pltpu.SMEM),
            pl.BlockSpec((TM, D_MODEL), lambda i: (i, 0)),
            pl.BlockSpec(memory_space=pl.ANY),
        ],
        out_specs=pl.BlockSpec(memory_space=pl.ANY),
        scratch_shapes=[pltpu.SemaphoreType.DMA],
        input_output_aliases={2: 0},
        compiler_params=_cp(("arbitrary",)),
        name="moe_dispatch",
    )(dest_flat, v, xs0)


def _expert_kernel(be_ref, nu_ref, x_ref, wgu_ref, bgu_ref, wd_ref, bd_ref, o_ref):
    del be_ref

    @pl.when(pl.program_id(0) < nu_ref[0])
    def _():
        h = jnp.dot(x_ref[...].astype(BF16), wgu_ref[0], preferred_element_type=F32) + bgu_ref[0]
        hg = jnp.minimum(h[:, 0:D_FF], SWIGLU_LIMIT)
        hl = jnp.clip(h[:, D_FF:2 * D_FF], -SWIGLU_LIMIT, SWIGLU_LIMIT)
        act = hg * _sigmoid(SWIGLU_ALPHA * hg) * (hl + 1.0)
        o_ref[...] = jnp.dot(act.astype(BF16), wd_ref[0], preferred_element_type=F32) + bd_ref[0]

    @pl.when(pl.program_id(0) >= nu_ref[0])
    def _():
        o_ref[...] = jnp.zeros_like(o_ref)


def _experts(blk_e, n_used, xs, w_gu, b_gu, w_down, b_down):
    n_slots = xs.shape[0]
    nblk = n_slots // MOE_BLK
    grid_spec = pltpu.PrefetchScalarGridSpec(
        num_scalar_prefetch=2,
        grid=(nblk,),
        in_specs=[
            pl.BlockSpec((MOE_BLK, D_MODEL), lambda b, be, nu: (jnp.minimum(b, nu[0] - 1), 0)),
            pl.BlockSpec((1, D_MODEL, 2 * D_FF), lambda b, be, nu: (be[b], 0, 0)),
            pl.BlockSpec((1, 1, 2 * D_FF), lambda b, be, nu: (be[b], 0, 0)),
            pl.BlockSpec((1, D_FF, D_MODEL), lambda b, be, nu: (be[b], 0, 0)),
            pl.BlockSpec((1, 1, D_MODEL), lambda b, be, nu: (be[b], 0, 0)),
        ],
        out_specs=pl.BlockSpec((MOE_BLK, D_MODEL), lambda b, be, nu: (b, 0)),
    )
    return pl.pallas_call(
        _expert_kernel,
        out_shape=jax.ShapeDtypeStruct((n_slots, D_MODEL), F32),
        grid_spec=grid_spec,
        compiler_params=_cp(("arbitrary",)),
        name="moe_experts",
    )(blk_e, n_used, xs, w_gu, b_gu.reshape(N_EXPERTS, 1, 2 * D_FF), w_down,
      b_down.reshape(N_EXPERTS, 1, D_MODEL))


def _combine_kernel(dest_ref, ys_hbm, gw_ref, x_ref, mod_ref, lng_ref, lnb_ref, o_ref, g_s, sem,
                    *, dn_alpha):
    def row_copy(r, kk):
        return pltpu.make_async_copy(ys_hbm.at[pl.ds(dest_ref[r * TOP_K + kk], 1)],
                                     g_s.at[kk, pl.ds(r, 1)], sem)

    def start(r, _):
        for kk in range(TOP_K):
            row_copy(r, kk).start()
        return 0

    def wait(r, _):
        for kk in range(TOP_K):
            row_copy(r, kk).wait()
        return 0

    lax.fori_loop(0, TM, start, 0)
    lax.fori_loop(0, TM, wait, 0)
    gw = gw_ref[...]
    f = jnp.zeros((TM, D_MODEL), F32)
    for kk in range(TOP_K):
        f = f + gw[:, kk:kk + 1] * g_s[kk]
    md = mod_ref[0]
    o_ref[...] = _layer_norm(dn_alpha * x_ref[...] + md[5:6] * f, lng_ref[...], lnb_ref[...])


def _combine(dest_flat, ys, gate_w, x1, modv, ln_g, ln_b, n_lat_blocks, dn_alpha):
    t = x1.shape[0]
    return pl.pallas_call(
        functools.partial(_combine_kernel, dn_alpha=dn_alpha),
        out_shape=jax.ShapeDtypeStruct((t, D_MODEL), F32),
        grid=(t // TM,),
        in_specs=[
            pl.BlockSpec((TM * TOP_K,), lambda i: (i,), memory_space=pltpu.SMEM),
            pl.BlockSpec(memory_space=pl.ANY),
            pl.BlockSpec((TM, TOP_K), lambda i: (i, 0)),
            pl.BlockSpec((TM, D_MODEL), lambda i: (i, 0)),
            pl.BlockSpec((1, 6, D_MODEL), lambda i: (i // n_lat_blocks, 0, 0)),
            pl.BlockSpec((1, D_MODEL), lambda i: (0, 0)),
            pl.BlockSpec((1, D_MODEL), lambda i: (0, 0)),
        ],
        out_specs=pl.BlockSpec((TM, D_MODEL), lambda i: (i, 0)),
        scratch_shapes=[pltpu.VMEM((TOP_K, TM, D_MODEL), F32), pltpu.SemaphoreType.DMA],
        compiler_params=_cp(("arbitrary",)),
        name="moe_combine",
    )(dest_flat, ys, gate_w, x1, modv, ln_g, ln_b)


def _moe(v, x1, modv, w_router, b_router, w_gu, b_gu, w_down, b_down, ln_g, ln_b, n_lat_blocks, dn_alpha):
    t = v.shape[0]
    top_i, gate_w, rank, counts = _router(v, w_router, b_router.reshape(1, N_EXPERTS))
    counts = counts[0]
    padded = (counts + MOE_BLK - 1) // MOE_BLK * MOE_BLK
    pend = jnp.cumsum(padded)
    pstart = pend - padded
    dest = (pstart[top_i] + rank).reshape(-1).astype(jnp.int32)
    n_slots = t * TOP_K + N_EXPERTS * MOE_BLK
    nblk = n_slots // MOE_BLK
    starts = jnp.arange(nblk, dtype=jnp.int32) * MOE_BLK
    blk_e = jnp.minimum(jnp.sum((pend[None, :] <= starts[:, None]).astype(jnp.int32), axis=1),
                        N_EXPERTS - 1).astype(jnp.int32)
    n_used = (pend[-1:] // MOE_BLK).astype(jnp.int32)
    xs = _dispatch(dest, v, n_slots)
    ys = _experts(blk_e, n_used, xs, w_gu, b_gu, w_down, b_down)
    return _combine(dest, ys, gate_w, x1, modv, ln_g, ln_b, n_lat_blocks, dn_alpha)


def _block_diag(w):
    nbk, bs, _ = w.shape
    eye = jnp.eye(nbk, dtype=w.dtype)
    return jnp.einsum('hij,hg->higj', w, eye).reshape(nbk * bs, nbk * bs)


def _key_block(t):
    for tk in (1280, 640, 256):
        if t % tk == 0:
            return tk
    raise ValueError("token count must be a multiple of 256")


def kernel(x, c, ctx, c_ctx, w_mod, b_mod, w_in, conv_w, conv_b, lru_wr, lru_br, lru_wi, lru_bi, lru_lam, s5_a_re, s5_a_im, s5_log_dt, s5_b_re, s5_b_im, s5_c_re, s5_c_im, s5_d, s5_w_glu, s5_b_glu, da_lam, da_subln_g, w_proj, b_gate, w_out, ln_g, ln_b, w_router, b_router, w_gu, b_gu, w_down, b_down):
    bsz, n_lat, d = x.shape
    n_ctx = ctx.shape[1]
    depth = w_mod.shape[0]
    assert bsz == 1 and d == D_MODEL and n_ctx == TM and n_lat % TM == 0 and n_lat % GRID_W == 0
    nb = n_lat // TM
    t = n_lat + n_ctx
    dn_alpha = (2 * depth) ** 0.25

    cc = jnp.zeros((SUBLANES, D_MODEL), F32).at[0].set(c[0]).at[1].set(c_ctx)
    mod = _modulation(cc, w_mod, b_mod)
    cos_t, sin_t = _rope_tables(n_lat, n_ctx)
    tq = 512 if n_lat % 512 == 0 else TM
    tk = _key_block(t)

    xs = jnp.concatenate([x[0], ctx[0]], axis=0)
    for l in range(depth):
        modv = mod[l, 0:2].reshape(2, 6, D_MODEL)
        hmat = _inproj(xs, modv, w_in[l].astype(BF16), nb)

        sp = jax.nn.softplus(-lru_lam[l])
        hdir, sdir = [], []
        for dr in range(2):
            hdir.append(_lru(hmat, nb, dr == 1, conv_w[l], conv_b[l].reshape(1, -1),
                             _block_diag(lru_wr[l, dr]).astype(BF16), lru_br[l, dr].reshape(1, -1),
                             _block_diag(lru_wi[l, dr]).astype(BF16), lru_bi[l, dr].reshape(1, -1),
                             sp[dr].reshape(1, -1)))
            prm = _s5_params(s5_a_re[l, dr], s5_a_im[l, dr], s5_log_dt[l, dr], s5_b_re[l, dr],
                             s5_b_im[l, dr], s5_c_re[l, dr], s5_c_im[l, dr])
            sdir.append(_s5(hmat, nb, dr == 1, *prm))

        qa, qb, kk, vv = _qkv_prep(hmat, cos_t, sin_t)
        lam_init = 0.8 - 0.6 * math.exp(-0.3 * l)
        g_row = da_subln_g[l].reshape(1, DA_V_DIM)
        yc = jnp.zeros((t, DA_HEADS * DA_V_DIM), F32)
        yc = _attention(qa, qb, kk, vv, da_lam[l], g_row, lam_init, yc,
                        q0=0, nq=n_lat // tq, tq=tq, k0=0, nk=t // tk, tk=tk, name="attn_lat")
        yc = _attention(qa, qb, kk, vv, da_lam[l], g_row, lam_init, yc,
                        q0=nb, nq=1, tq=TM, k0=nb, nk=1, tk=TM, name="attn_ctx")

        x1, v = _merge(xs, modv, hmat, hdir[0], hdir[1], sdir[0], sdir[1], yc,
                       s5_d[l].reshape(1, -1), s5_w_glu[l].astype(BF16), s5_b_glu[l].reshape(1, -1),
                       w_proj[l].astype(BF16), b_gate[l], w_out[l].astype(BF16),
                       ln_g[l, 0].reshape(1, -1), ln_b[l, 0].reshape(1, -1), nb, dn_alpha)
        xs = _moe(v, x1, modv, w_router[l], b_router[l], w_gu[l].astype(BF16), b_gu[l],
                  w_down[l].astype(BF16), b_down[l], ln_g[l, 1].reshape(1, -1),
                  ln_b[l, 1].reshape(1, -1), nb, dn_alpha)
    return xs[:n_lat][None]
```

```python
import functools
import math

import numpy as np
import jax
import jax.numpy as jnp
from jax import lax
from jax.experimental import pallas as pl
from jax.experimental.pallas import tpu as pltpu

F32 = jnp.float32
BF16 = jnp.bfloat16
HIGHEST = lax.Precision.HIGHEST

D_MODEL = 1024
GRID_W = 64
LRU_WIDTH = 512
LRU_BLOCKS = 8
CONV_W = 4
LRU_C = 8.0
S5_WIDTH = 512
S5_GROUP = 16
S5_GROUPS = S5_WIDTH // S5_GROUP
S5_STATE = 64
S5_N = S5_GROUPS * S5_STATE
DA_HEADS = 4
DA_HEAD_DIM = 64
DA_V_DIM = 128
DA_SCALE = DA_HEAD_DIM ** -0.5
ROPE_BASE = 10000.0
ROPE_FREQS = DA_HEAD_DIM // 4
N_BRANCH = 3
BW = 512
IN_COLS = 6 * BW + N_BRANCH * D_MODEL
N_EXPERTS = 32
TOP_K = 4
D_FF = D_MODEL
SWIGLU_ALPHA = 1.702
SWIGLU_LIMIT = 7.0
LN_EPS = 1e-5
RMS_EPS = 1e-6

SUBLANES = 8
LANES = 128
VMEM_LIMIT = 56 * 1024 * 1024

TM = 256
HALO = 16
SEG = TM // SUBLANES
S5_CHUNKS = S5_WIDTH // LANES
LOG2_E = math.log2(math.e)
MOE_BLK = 256
TN_IN = 3072


def _cp(sem, vmem=VMEM_LIMIT):
    return pltpu.CompilerParams(dimension_semantics=sem, vmem_limit_bytes=vmem)


def _sigmoid(x):
    return 1.0 / (1.0 + jnp.exp(-x))


def _gelu_tanh(x):
    return 0.5 * x * (1.0 + jnp.tanh(math.sqrt(2.0 / math.pi) * (x + 0.044715 * (x * x * x))))


def _layer_norm(y, g, b):
    mu = jnp.mean(y, axis=-1, keepdims=True)
    yc = y - mu
    var = jnp.mean(yc * yc, axis=-1, keepdims=True)
    return yc * lax.rsqrt(var + LN_EPS) * g + b


def _mod_kernel(c_ref, w_ref, b_ref, o_ref):
    c = c_ref[...]
    s = c * _sigmoid(c)
    o_ref[0] = jnp.dot(s, w_ref[0], preferred_element_type=F32, precision=HIGHEST) + b_ref[0]


def _modulation(cc, w_mod, b_mod):
    depth = w_mod.shape[0]
    tn = 1536
    return pl.pallas_call(
        _mod_kernel,
        out_shape=jax.ShapeDtypeStruct((depth, SUBLANES, 6 * D_MODEL), F32),
        grid=(depth, 6 * D_MODEL // tn),
        in_specs=[
            pl.BlockSpec((SUBLANES, D_MODEL), lambda l, j: (0, 0)),
            pl.BlockSpec((1, D_MODEL, tn), lambda l, j: (l, 0, j)),
            pl.BlockSpec((1, 1, tn), lambda l, j: (l, 0, j)),
        ],
        out_specs=pl.BlockSpec((1, SUBLANES, tn), lambda l, j: (l, 0, j)),
        compiler_params=_cp(("arbitrary", "arbitrary")),
        name="modulation",
    )(cc, w_mod, b_mod.reshape(depth, 1, 6 * D_MODEL))


def _inproj_kernel(x_ref, mod_ref, w_ref, o_ref):
    md = mod_ref[0]
    u = x_ref[...] * (1.0 + md[1:2]) + md[0:1]
    o_ref[...] = jnp.dot(u.astype(BF16), w_ref[...], preferred_element_type=F32).astype(BF16)


def _inproj(x, modv, w_bf16, n_lat_blocks):
    t = x.shape[0]
    return pl.pallas_call(
        _inproj_kernel,
        out_shape=jax.ShapeDtypeStruct((t, IN_COLS), BF16),
        grid=(IN_COLS // TN_IN, t // TM),
        in_specs=[
            pl.BlockSpec((TM, D_MODEL), lambda j, i: (i, 0)),
            pl.BlockSpec((1, 6, D_MODEL), lambda j, i: (i // n_lat_blocks, 0, 0)),
            pl.BlockSpec((D_MODEL, TN_IN), lambda j, i: (0, j)),
        ],
        out_specs=pl.BlockSpec((TM, TN_IN), lambda j, i: (i, j)),
        compiler_params=_cp(("arbitrary", "arbitrary")),
        name="inproj",
    )(x, modv, w_bf16)


def _scan_block(s, nb, reverse):
    return (nb - s) if reverse else (s + nb) % (nb + 1)


def _lru_kernel(prev_ref, cur_ref, next_ref, cw_ref, cb_ref, wr_ref, br_ref, wi_ref, bi_ref,
                sp_ref, o_ref, a_s, b_s, h_s, *, nb, reverse):
    s = pl.program_id(0)
    blk = _scan_block(s, nb, reverse)

    @pl.when(s == 0)
    def _():
        h_s[...] = jnp.zeros_like(h_s)

    has_prev = jnp.logical_and(blk != 0, blk != nb)
    has_next = jnp.logical_and(blk != nb - 1, blk != nb)
    prev = jnp.where(has_prev, prev_ref[...].astype(F32), 0.0)
    nxt = jnp.where(has_next, next_ref[...].astype(F32), 0.0)
    ext = jnp.concatenate([prev, cur_ref[...].astype(F32), nxt], axis=0)
    cw = cw_ref[...]
    xc = cb_ref[...] + jnp.zeros((TM, LRU_WIDTH), F32)
    for k in range(CONV_W):
        off = HALO - CONV_W // 2 + k
        xc = xc + ext[off:off + TM] * cw[k:k + 1]
    xb = xc.astype(BF16)
    r = _sigmoid(jnp.dot(xb, wr_ref[...], preferred_element_type=F32) + br_ref[...])
    i = _sigmoid(jnp.dot(xb, wi_ref[...], preferred_element_type=F32) + bi_ref[...])
    a = jnp.exp(-LRU_C * r * sp_ref[...])
    a_s[...] = a
    b_s[...] = jnp.sqrt(1.0 - a * a) * (i * xc)

    def step(n, h):
        t = (TM - 1 - n) if reverse else n
        h = a_s[pl.ds(t, 1), :] * h + b_s[pl.ds(t, 1), :]
        o_ref[pl.ds(t, 1), :] = h
        return h

    h_s[...] = lax.fori_loop(0, TM, step, h_s[...], unroll=8)


def _lru(hmat, nb, reverse, conv_w, conv_b, wr, br, wi, bi, sp):
    t = hmat.shape[0]
    nblk = nb + 1
    rh = TM // HALO
    last_h = t // HALO - 1

    def bmap(s):
        return _scan_block(s, nb, reverse)

    vec = lambda: pl.BlockSpec((1, LRU_WIDTH), lambda s: (0, 0))
    mat = lambda: pl.BlockSpec((LRU_WIDTH, LRU_WIDTH), lambda s: (0, 0))
    return pl.pallas_call(
        functools.partial(_lru_kernel, nb=nb, reverse=reverse),
        out_shape=jax.ShapeDtypeStruct((t, LRU_WIDTH), F32),
        grid=(nblk,),
        in_specs=[
            pl.BlockSpec((HALO, LRU_WIDTH), lambda s: (jnp.maximum(bmap(s) * rh - 1, 0), 0)),
            pl.BlockSpec((TM, LRU_WIDTH), lambda s: (bmap(s), 0)),
            pl.BlockSpec((HALO, LRU_WIDTH), lambda s: (jnp.minimum((bmap(s) + 1) * rh, last_h), 0)),
            pl.BlockSpec((CONV_W, LRU_WIDTH), lambda s: (0, 0)),
            vec(), mat(), vec(), mat(), vec(), vec(),
        ],
        out_specs=pl.BlockSpec((TM, LRU_WIDTH), lambda s: (bmap(s), 0)),
        scratch_shapes=[pltpu.VMEM((TM, LRU_WIDTH), F32), pltpu.VMEM((TM, LRU_WIDTH), F32),
                        pltpu.VMEM((1, LRU_WIDTH), F32)],
        compiler_params=_cp(("arbitrary",)),
        name="lru_bwd" if reverse else "lru_fwd",
    )(hmat, hmat, hmat, conv_w, conv_b, wr, br, wi, bi, sp)


def _cmul(ar, ai, br, bi):
    return ar * br - ai * bi, ar * bi + ai * br


def _s5_kernel(u_ref, p_ref, pt_ref, bm_ref, cm_ref, a_ref, pw_ref, o_ref, x_s, st_s, c_s):
    s = pl.program_id(0)
    n = S5_N

    @pl.when(s == 0)
    def _():
        c_s[...] = jnp.zeros_like(c_s)

    up = jnp.dot(p_ref[...], u_ref[...], preferred_element_type=F32).astype(BF16)
    nc = n // S5_CHUNKS
    for c in range(S5_CHUNKS):
        xc = jnp.dot(up[:, c * LANES:(c + 1) * LANES], bm_ref[c], preferred_element_type=F32)
        x_s[:, c * nc:(c + 1) * nc] = xc[:, 0:nc]
        x_s[:, n + c * nc:n + (c + 1) * nc] = xc[:, nc:2 * nc]
    ar = a_ref[0:1, :]
    ai = a_ref[1:2, :]

    def local(m, carry):
        hr, hi = carry
        row = pl.multiple_of(m * SUBLANES, SUBLANES)
        pr, pi = _cmul(ar, ai, hr, hi)
        hr = pr + x_s[pl.ds(row, SUBLANES), 0:n]
        hi = pi + x_s[pl.ds(row, SUBLANES), n:2 * n]
        x_s[pl.ds(row, SUBLANES), 0:n] = hr
        x_s[pl.ds(row, SUBLANES), n:2 * n] = hi
        return hr, hi

    z = jnp.zeros((SUBLANES, n), F32)
    fr, fi = lax.fori_loop(0, SEG, local, (z, z))

    qr = pw_ref[SEG - 1:SEG, 0:n]
    qi = pw_ref[SEG - 1:SEG, n:2 * n]
    sr = c_s[0:1, 0:n]
    si = c_s[0:1, n:2 * n]
    for j in range(SUBLANES):
        st_s[j:j + 1, 0:n] = sr
        st_s[j:j + 1, n:2 * n] = si
        pr, pi = _cmul(qr, qi, sr, si)
        sr = pr + fr[j:j + 1]
        si = pi + fi[j:j + 1]
    c_s[0:1, 0:n] = sr
    c_s[0:1, n:2 * n] = si

    str_ = st_s[:, 0:n]
    sti = st_s[:, n:2 * n]

    def fix(m, _):
        row = pl.multiple_of(m * SUBLANES, SUBLANES)
        pr, pi = _cmul(pw_ref[pl.ds(m, 1), 0:n], pw_ref[pl.ds(m, 1), n:2 * n], str_, sti)
        x_s[pl.ds(row, SUBLANES), 0:n] = x_s[pl.ds(row, SUBLANES), 0:n] + pr
        x_s[pl.ds(row, SUBLANES), n:2 * n] = x_s[pl.ds(row, SUBLANES), n:2 * n] + pi
        return 0

    lax.fori_loop(0, SEG, fix, 0)

    yp = jnp.concatenate(
        [jnp.dot(x_s[:, c * nc:(c + 1) * nc].astype(BF16), cm_ref[0, c], preferred_element_type=F32)
         + jnp.dot(x_s[:, n + c * nc:n + (c + 1) * nc].astype(BF16), cm_ref[1, c], preferred_element_type=F32)
         for c in range(S5_CHUNKS)], axis=1)
    y_hi = yp.astype(BF16)
    y_lo = (yp - y_hi.astype(F32)).astype(BF16)
    pt = pt_ref[...]
    o_ref[...] = (jnp.dot(pt, y_hi, preferred_element_type=F32)
                  + jnp.dot(pt, y_lo, preferred_element_type=F32))


def _s5_perm(reverse):
    p = np.zeros((TM, TM), np.float32)
    for m in range(SEG):
        for j in range(SUBLANES):
            pos = j * SEG + m
            p[m * SUBLANES + j, (TM - 1 - pos) if reverse else pos] = 1.0
    return p


def _s5(hmat, nb, reverse, bm_t, cm_t, a_row, pw):
    t = hmat.shape[0]
    p = _s5_perm(reverse)
    n2 = 2 * S5_N

    def bmap(s):
        return _scan_block(s, nb, reverse)

    return pl.pallas_call(
        _s5_kernel,
        out_shape=jax.ShapeDtypeStruct((t, S5_WIDTH), F32),
        grid=(nb + 1,),
        in_specs=[
            pl.BlockSpec((TM, S5_WIDTH), lambda s: (bmap(s), 2)),
            pl.BlockSpec((TM, TM), lambda s: (0, 0)),
            pl.BlockSpec((TM, TM), lambda s: (0, 0)),
            pl.BlockSpec((S5_CHUNKS, LANES, n2 // S5_CHUNKS), lambda s: (0, 0, 0)),
            pl.BlockSpec((2, S5_CHUNKS, S5_N // S5_CHUNKS, LANES), lambda s: (0, 0, 0, 0)),
            pl.BlockSpec((2, S5_N), lambda s: (0, 0)),
            pl.BlockSpec((SEG, n2), lambda s: (0, 0)),
        ],
        out_specs=pl.BlockSpec((TM, S5_WIDTH), lambda s: (bmap(s), 0)),
        scratch_shapes=[pltpu.VMEM((TM, n2), F32), pltpu.VMEM((SUBLANES, n2), F32),
                        pltpu.VMEM((1, n2), F32)],
        compiler_params=_cp(("arbitrary",)),
        name="s5_bwd" if reverse else "s5_fwd",
    )(hmat, jnp.asarray(p, BF16), jnp.asarray(p.T, BF16), bm_t, cm_t, a_row, pw)


def _s5_params(a_re, a_im, log_dt, b_re, b_im, c_re, c_im):
    g, p = a_re.shape
    dt = jnp.exp(log_dt)[:, None]
    lr, li = a_re, a_im
    ea = jnp.exp(lr * dt)
    ab_r, ab_i = ea * jnp.cos(li * dt), ea * jnp.sin(li * dt)
    den = lr * lr + li * li
    co_r = ((ab_r - 1.0) * lr + ab_i * li) / den
    co_i = (ab_i * lr - (ab_r - 1.0) * li) / den
    bb_r = co_r[..., None] * b_re - co_i[..., None] * b_im
    bb_i = co_r[..., None] * b_im + co_i[..., None] * b_re
    gc = g // S5_CHUNKS
    eye = jnp.eye(gc, dtype=F32)

    def chunked(w, spec):
        return jnp.einsum(spec, w.reshape((S5_CHUNKS, gc) + w.shape[1:]), eye)

    bm_r = chunked(bb_r, 'kgpc,gh->kgchp').reshape(S5_CHUNKS, gc * S5_GROUP, gc * p)
    bm_i = chunked(bb_i, 'kgpc,gh->kgchp').reshape(S5_CHUNKS, gc * S5_GROUP, gc * p)
    bm_t = jnp.concatenate([bm_r, bm_i], axis=2).astype(BF16)
    cm_r = chunked(c_re, 'kgcp,gh->kgphc').reshape(S5_CHUNKS, gc * p, gc * S5_GROUP)
    cm_i = chunked(-c_im, 'kgcp,gh->kgphc').reshape(S5_CHUNKS, gc * p, gc * S5_GROUP)
    cm_t = jnp.stack([cm_r, cm_i], axis=0).astype(BF16)
    a_row = jnp.stack([ab_r.reshape(-1), ab_i.reshape(-1)], axis=0)
    k = jnp.arange(1, SEG + 1, dtype=F32)[:, None, None]
    ek = jnp.exp(k * (lr * dt))
    pw = jnp.concatenate([(ek * jnp.cos(k * (li * dt))).reshape(SEG, -1),
                          (ek * jnp.sin(k * (li * dt))).reshape(SEG, -1)], axis=1)
    return bm_t, cm_t, a_row, pw


def _qkv_kernel(q_ref, k_ref, v_ref, cos_ref, sin_ref, qa_ref, qb_ref, ko_ref, vo_ref):
    w = DA_HEADS * 2 * DA_HEAD_DIM
    cos = jnp.concatenate([cos_ref[...]] * (w // LANES), axis=1)
    sin = jnp.concatenate([sin_ref[...]] * (w // LANES), axis=1)
    lane = lax.broadcasted_iota(jnp.int32, (TM, w), 1)
    first = (lane % (2 * ROPE_FREQS)) < ROPE_FREQS

    def rope(x):
        swapped = jnp.where(first, pltpu.roll(x, w - ROPE_FREQS, 1), pltpu.roll(x, ROPE_FREQS, 1))
        return x * cos + swapped * sin

    q = rope(q_ref[...].astype(F32)) * (DA_SCALE * LOG2_E)
    in_a = (lane % (2 * DA_HEAD_DIM)) < DA_HEAD_DIM
    qa_ref[...] = jnp.where(in_a, q, 0.0).astype(BF16)
    qb_ref[...] = jnp.where(in_a, 0.0, q).astype(BF16)
    ko_ref[...] = rope(k_ref[...].astype(F32)).astype(BF16)
    v = v_ref[...]
    ones = jnp.ones((TM, DA_V_DIM), BF16)
    vo_ref[...] = jnp.concatenate(
        [blk for h in range(DA_HEADS) for blk in (v[:, h * DA_V_DIM:(h + 1) * DA_V_DIM], ones)], axis=1)


def _qkv_prep(hmat, cos_t, sin_t):
    t = hmat.shape[0]
    w = DA_HEADS * 2 * DA_HEAD_DIM
    col = lambda c: pl.BlockSpec((TM, w), lambda i: (i, c))
    tab = lambda: pl.BlockSpec((TM, LANES), lambda i: (i, 0))
    out = lambda: pl.BlockSpec((TM, w), lambda i: (i, 0))
    sds = jax.ShapeDtypeStruct((t, w), BF16)
    return pl.pallas_call(
        _qkv_kernel,
        out_shape=(sds, sds, sds, jax.ShapeDtypeStruct((t, 2 * w), BF16)),
        grid=(t // TM,),
        in_specs=[col(3), col(4), col(5), tab(), tab()],
        out_specs=(out(), out(), out(), pl.BlockSpec((TM, 2 * w), lambda i: (i, 0))),
        compiler_params=_cp(("arbitrary",)),
        name="qkv_prep",
    )(hmat, hmat, hmat, cos_t, sin_t)


def _rope_tables(n_lat, n_ctx):
    pos = jnp.arange(n_lat)
    freqs = ROPE_BASE ** (-jnp.arange(ROPE_FREQS, dtype=F32) / ROPE_FREQS)
    ang_r = (pos // GRID_W).astype(F32)[:, None] * freqs
    ang_c = (pos % GRID_W).astype(F32)[:, None] * freqs
    cr, sr, cc, sc = jnp.cos(ang_r), jnp.sin(ang_r), jnp.cos(ang_c), jnp.sin(ang_c)
    cos64 = jnp.concatenate([cr, cr, cc, cc], axis=1)
    sin64 = jnp.concatenate([-sr, sr, -sc, sc], axis=1)
    cos_t = jnp.concatenate([cos64, cos64], axis=1)
    sin_t = jnp.concatenate([sin64, sin64], axis=1)
    cos_t = jnp.concatenate([cos_t, jnp.ones((n_ctx, LANES), F32)], axis=0)
    sin_t = jnp.concatenate([sin_t, jnp.zeros((n_ctx, LANES), F32)], axis=0)
    return cos_t, sin_t


def _attn_kernel(lam_ref, g_ref, qa_ref, qb_ref, k_ref, v_ref, prev_ref, o_ref, m_s, acc_s, a_s, s_s,
                 *, lam_init, nk):
    del prev_ref
    g = pl.program_id(1)
    kj = g % nk

    @pl.when(g == 0)
    def _():
        s_s[...] = jnp.zeros_like(s_s)
        acc_s[...] = jnp.zeros_like(acc_s)
        a_s[...] = jnp.zeros_like(a_s)
        m_s[...] = jnp.zeros_like(m_s)

    def step(cur):
        k = k_ref[...]
        v = v_ref[...]
        for i, q_ref in enumerate((qa_ref, qb_ref)):
            m_old = m_s[i]
            p = jnp.exp2(s_s[1 - cur, i] - m_old[:, 0:1]).astype(BF16)
            alpha = a_s[i]
            acc_s[i] = (jnp.concatenate([alpha, alpha], axis=1) * acc_s[i]
                        + jnp.dot(p, v, preferred_element_type=F32))
            s = lax.dot_general(q_ref[...], k, (((1,), (1,)), ((), ())), preferred_element_type=F32)
            s_s[cur, i] = s
            m_base = jnp.where(kj == 0, -jnp.inf, m_old)
            m_new = jnp.maximum(m_base, jnp.max(s, axis=1, keepdims=True))
            a_s[i] = jnp.exp2(m_base - m_new)
            m_s[i] = m_new

    def finish():
        lv = lam_ref[...]
        lam = (jnp.exp(jnp.sum(lv[0:1] * lv[1:2], axis=1, keepdims=True))
               - jnp.exp(jnp.sum(lv[2:3] * lv[3:4], axis=1, keepdims=True)) + lam_init)
        a0, a1 = acc_s[0], acc_s[1]
        o = (a0[:, 0:DA_V_DIM] / a0[:, DA_V_DIM:2 * DA_V_DIM]
             - lam * (a1[:, 0:DA_V_DIM] / a1[:, DA_V_DIM:2 * DA_V_DIM]))
        ms = jnp.mean(o * o, axis=1, keepdims=True)
        o_ref[...] = o * lax.rsqrt(ms + RMS_EPS) * g_ref[...] * (1.0 - lam_init)

    for parity in range(2):
        pl.when(g % 2 == parity)(functools.partial(step, parity))
    pl.when(jnp.logical_and(kj == 0, g > 0))(finish)


def _attention(qa, qb, k, v, lam_vec, subln_g, lam_init, prev, *, q0, nq, tq, k0, nk, tk, name):
    t = qa.shape[0]
    n_pairs = nq * nk

    def qmap(h, g):
        return (q0 + jnp.minimum(g // nk, nq - 1), h)

    return pl.pallas_call(
        functools.partial(_attn_kernel, lam_init=lam_init, nk=nk),
        out_shape=jax.ShapeDtypeStruct((t, DA_HEADS * DA_V_DIM), F32),
        grid=(DA_HEADS, n_pairs + 1),
        in_specs=[
            pl.BlockSpec((4, DA_HEAD_DIM), lambda h, g: (0, 0)),
            pl.BlockSpec((1, DA_V_DIM), lambda h, g: (0, 0)),
            pl.BlockSpec((tq, LANES), qmap),
            pl.BlockSpec((tq, LANES), qmap),
            pl.BlockSpec((tk, LANES), lambda h, g: (k0 + g % nk, h)),
            pl.BlockSpec((tk, 2 * DA_V_DIM), lambda h, g: (k0 + (g + nk - 1) % nk, h)),
            pl.BlockSpec(memory_space=pl.ANY),
        ],
        out_specs=pl.BlockSpec((tq, LANES), lambda h, g: (q0 + jnp.maximum(g - 1, 0) // nk, h)),
        scratch_shapes=[pltpu.VMEM((2, tq, LANES), F32), pltpu.VMEM((2, tq, 2 * DA_V_DIM), F32),
                        pltpu.VMEM((2, tq, LANES), F32), pltpu.VMEM((2, 2, tq, tk), F32)],
        input_output_aliases={6: 0},
        compiler_params=_cp(("arbitrary", "arbitrary")),
        name=name,
    )(lam_vec, subln_g, qa, qb, k, v, prev)


def _merge_kernel(x_ref, mod_ref, gt_ref, ga_ref, ub_ref, hf_ref, hb_ref, sf_ref, sb_ref, yc_ref,
                  sd_ref, wglu_ref, bglu_ref, wp_ref, bg_ref, wo_ref, lng_ref, lnb_ref,
                  x1_ref, v_ref, *, dn_alpha):
    md = mod_ref[0]
    ya = (hf_ref[...] + hb_ref[...]) * _gelu_tanh(ga_ref[...].astype(F32))
    y = sf_ref[...] + sb_ref[...] + sd_ref[...] * ub_ref[...].astype(F32)
    tg = jnp.dot(_gelu_tanh(y).astype(BF16), wglu_ref[...], preferred_element_type=F32) + bglu_ref[...]
    yb = tg[:, 0:S5_WIDTH] * _sigmoid(tg[:, S5_WIDTH:2 * S5_WIDTH])
    yc = yc_ref[...]
    bg = bg_ref[...]
    z = jnp.zeros((TM, D_MODEL), F32)
    for n, br in enumerate((ya, yb, yc)):
        gate = _sigmoid(gt_ref[:, n * D_MODEL:(n + 1) * D_MODEL].astype(F32) + bg[n:n + 1])
        z = z + gate * jnp.dot(br.astype(BF16), wp_ref[n], preferred_element_type=F32)
    m = jnp.dot(z.astype(BF16), wo_ref[...], preferred_element_type=F32)
    x1 = _layer_norm(dn_alpha * x_ref[...] + md[2:3] * m, lng_ref[...], lnb_ref[...])
    x1_ref[...] = x1
    v_ref[...] = x1 * (1.0 + md[4:5]) + md[3:4]


def _merge(x, modv, hmat, hf, hb, sf, sb, yc, s5_d, w_glu, b_glu, w_proj, b_gate, w_out, ln_g, ln_b,
           n_lat_blocks, dn_alpha):
    t = x.shape[0]
    row = lambda w, c=0: pl.BlockSpec((TM, w), lambda i: (i, c))
    const = lambda *shape: pl.BlockSpec(shape, lambda i: (0,) * len(shape))
    sds = jax.ShapeDtypeStruct((t, D_MODEL), F32)
    return pl.pallas_call(
        functools.partial(_merge_kernel, dn_alpha=dn_alpha),
        out_shape=(sds, sds),
        grid=(t // TM,),
        in_specs=[
            row(D_MODEL),
            pl.BlockSpec((1, 6, D_MODEL), lambda i: (i // n_lat_blocks, 0, 0)),
            row(N_BRANCH * D_MODEL, 1), row(BW, 1), row(BW, 2),
            row(BW), row(BW), row(BW), row(BW), row(BW),
            const(1, S5_WIDTH), const(S5_WIDTH, 2 * S5_WIDTH), const(1, 2 * S5_WIDTH),
            const(N_BRANCH, BW, D_MODEL), const(N_BRANCH, D_MODEL), const(D_MODEL, D_MODEL),
            const(1, D_MODEL), const(1, D_MODEL),
        ],
        out_specs=(row(D_MODEL), row(D_MODEL)),
        compiler_params=_cp(("arbitrary",)),
        name="merge",
    )(x, modv, hmat, hmat, hmat, hf, hb, sf, sb, yc, s5_d, w_glu, b_glu, w_proj, b_gate, w_out,
      ln_g, ln_b)


def _router_kernel(v_ref, w_ref, b_ref, tri_ref, ti_ref, gw_ref, rk_ref, cnt_ref, c_s):
    s = pl.program_id(0)

    @pl.when(s == 0)
    def _():
        c_s[...] = jnp.zeros_like(c_s)

    logits = jnp.dot(v_ref[...], w_ref[...], preferred_element_type=F32, precision=HIGHEST) + b_ref[...]
    lane = lax.broadcasted_iota(jnp.int32, (TM, N_EXPERTS), 1).astype(F32)
    work = logits
    vals, idxs = [], []
    hot = jnp.zeros((TM, N_EXPERTS), F32)
    for _ in range(TOP_K):
        mx = jnp.max(work, axis=1, keepdims=True)
        ix = jnp.min(jnp.where(work == mx, lane, float(N_EXPERTS)), axis=1, keepdims=True)
        sel = lane == ix
        hot = jnp.where(sel, 1.0, hot)
        work = jnp.where(sel, -jnp.inf, work)
        vals.append(mx)
        idxs.append(ix)
    es = [jnp.exp(vv - vals[0]) for vv in vals]
    den = es[0] + es[1] + es[2] + es[3]
    before = jnp.dot(tri_ref[...], hot.astype(BF16), preferred_element_type=F32) + c_s[0:1, :]
    col = lax.broadcasted_iota(jnp.int32, (TM, TOP_K), 1)
    ti = jnp.zeros((TM, TOP_K), F32)
    gw = jnp.zeros((TM, TOP_K), F32)
    rk = jnp.zeros((TM, TOP_K), F32)
    for kk in range(TOP_K):
        rank = jnp.sum(jnp.where(lane == idxs[kk], before, 0.0), axis=1, keepdims=True)
        ti = jnp.where(col == kk, idxs[kk], ti)
        gw = jnp.where(col == kk, es[kk] / den, gw)
        rk = jnp.where(col == kk, rank, rk)
    ti_ref[...] = ti.astype(jnp.int32)
    gw_ref[...] = gw
    rk_ref[...] = rk.astype(jnp.int32)
    tot = c_s[0:1, :] + jnp.sum(hot, axis=0, keepdims=True)
    c_s[...] = jnp.broadcast_to(tot, c_s.shape)
    cnt_ref[...] = jnp.broadcast_to(tot, cnt_ref.shape).astype(jnp.int32)


def _router(v, w_router, b_router):
    t = v.shape[0]
    tri = jnp.asarray(np.tril(np.ones((TM, TM), np.float32), -1), BF16)
    row4 = lambda: pl.BlockSpec((TM, TOP_K), lambda i: (i, 0))
    return pl.pallas_call(
        _router_kernel,
        out_shape=(jax.ShapeDtypeStruct((t, TOP_K), jnp.int32),
                   jax.ShapeDtypeStruct((t, TOP_K), F32),
                   jax.ShapeDtypeStruct((t, TOP_K), jnp.int32),
                   jax.ShapeDtypeStruct((SUBLANES, N_EXPERTS), jnp.int32)),
        grid=(t // TM,),
        in_specs=[
            pl.BlockSpec((TM, D_MODEL), lambda i: (i, 0)),
            pl.BlockSpec((D_MODEL, N_EXPERTS), lambda i: (0, 0)),
            pl.BlockSpec((1, N_EXPERTS), lambda i: (0, 0)),
            pl.BlockSpec((TM, TM), lambda i: (0, 0)),
        ],
        out_specs=(row4(), row4(), row4(), pl.BlockSpec((SUBLANES, N_EXPERTS), lambda i: (0, 0))),
        scratch_shapes=[pltpu.VMEM((SUBLANES, N_EXPERTS), F32)],
        compiler_params=_cp(("arbitrary",)),
        name="router",
    )(v, w_router, b_router, tri)


def _dispatch_kernel(dest_ref, v_ref, xs_in, xs_hbm, sem):
    del xs_in

    def row_copy(r, kk):
        return pltpu.make_async_copy(v_ref.at[pl.ds(r, 1)],
                                     xs_hbm.at[pl.ds(dest_ref[r * TOP_K + kk], 1)], sem)

    def start(r, _):
        for kk in range(TOP_K):
            row_copy(r, kk).start()
        return 0

    def wait(r, _):
        for kk in range(TOP_K):
            row_copy(r, kk).wait()
        return 0

    lax.fori_loop(0, TM, start, 0, unroll=8)
    lax.fori_loop(0, TM, wait, 0, unroll=8)


def _dispatch(dest_flat, v, n_slots):
    t = v.shape[0]
    xs0 = jnp.zeros((n_slots, D_MODEL), F32)
    return pl.pallas_call(
        _dispatch_kernel,
        out_shape=jax.ShapeDtypeStruct((n_slots, D_MODEL), F32),
        grid=(t // TM,),
        in_specs=[
            pl.BlockSpec((TM * TOP_K,), lambda i: (i,), memory_space=pltpu.SMEM),
            pl.BlockSpec((TM, D_MODEL), lambda i: (i, 0)),
            pl.BlockSpec(memory_space=pl.ANY),
        ],
        out_specs=pl.BlockSpec(memory_space=pl.ANY),
        scratch_shapes=[pltpu.SemaphoreType.DMA],
        input_output_aliases={2: 0},
        compiler_params=_cp(("arbitrary",)),
        name="moe_dispatch",
    )(dest_flat, v, xs0)


def _expert_kernel(be_ref, nu_ref, x_ref, wgu_ref, bgu_ref, wd_ref, bd_ref, o_ref, wgu_s, wd_s):
    b = pl.program_id(0)
    new_expert = jnp.logical_or(b == 0, be_ref[b] != be_ref[jnp.maximum(b - 1, 0)])

    @pl.when(jnp.logical_and(new_expert, b < nu_ref[0]))
    def _():
        wgu_s[...] = wgu_ref[0].astype(BF16)
        wd_s[...] = wd_ref[0].astype(BF16)

    @pl.when(b < nu_ref[0])
    def _():
        h = jnp.dot(x_ref[...].astype(BF16), wgu_s[...], preferred_element_type=F32) + bgu_ref[0]
        hg = jnp.minimum(h[:, 0:D_FF], SWIGLU_LIMIT)
        hl = jnp.clip(h[:, D_FF:2 * D_FF], -SWIGLU_LIMIT, SWIGLU_LIMIT)
        act = hg * _sigmoid(SWIGLU_ALPHA * hg) * (hl + 1.0)
        o_ref[...] = jnp.dot(act.astype(BF16), wd_s[...], preferred_element_type=F32) + bd_ref[0]

    @pl.when(pl.program_id(0) >= nu_ref[0])
    def _():
        o_ref[...] = jnp.zeros_like(o_ref)


def _experts(blk_e, n_used, xs, w_gu, b_gu, w_down, b_down):
    n_slots = xs.shape[0]
    nblk = n_slots // MOE_BLK
    grid_spec = pltpu.PrefetchScalarGridSpec(
        num_scalar_prefetch=2,
        grid=(nblk,),
        in_specs=[
            pl.BlockSpec((MOE_BLK, D_MODEL), lambda b, be, nu: (jnp.minimum(b, nu[0] - 1), 0)),
            pl.BlockSpec((1, D_MODEL, 2 * D_FF), lambda b, be, nu: (be[b], 0, 0)),
            pl.BlockSpec((1, 1, 2 * D_FF), lambda b, be, nu: (be[b], 0, 0)),
            pl.BlockSpec((1, D_FF, D_MODEL), lambda b, be, nu: (be[b], 0, 0)),
            pl.BlockSpec((1, 1, D_MODEL), lambda b, be, nu: (be[b], 0, 0)),
        ],
        out_specs=pl.BlockSpec((MOE_BLK, D_MODEL), lambda b, be, nu: (b, 0)),
        scratch_shapes=[pltpu.VMEM((D_MODEL, 2 * D_FF), BF16), pltpu.VMEM((D_FF, D_MODEL), BF16)],
    )
    return pl.pallas_call(
        _expert_kernel,
        out_shape=jax.ShapeDtypeStruct((n_slots, D_MODEL), F32),
        grid_spec=grid_spec,
        compiler_params=_cp(("arbitrary",)),
        name="moe_experts",
    )(blk_e, n_used, xs, w_gu, b_gu.reshape(N_EXPERTS, 1, 2 * D_FF), w_down,
      b_down.reshape(N_EXPERTS, 1, D_MODEL))


def _combine_kernel(dest_ref, ys_hbm, gw_ref, x_ref, mod_ref, lng_ref, lnb_ref, o_ref, g_s, sem,
                    *, dn_alpha):
    def row_copy(r, kk):
        return pltpu.make_async_copy(ys_hbm.at[pl.ds(dest_ref[r * TOP_K + kk], 1)],
                                     g_s.at[kk, pl.ds(r, 1)], sem)

    def start(r, _):
        for kk in range(TOP_K):
            row_copy(r, kk).start()
        return 0

    def wait(r, _):
        for kk in range(TOP_K):
            row_copy(r, kk).wait()
        return 0

    lax.fori_loop(0, TM, start, 0, unroll=8)
    lax.fori_loop(0, TM, wait, 0, unroll=8)
    gw = gw_ref[...]
    f = jnp.zeros((TM, D_MODEL), F32)
    for kk in range(TOP_K):
        f = f + gw[:, kk:kk + 1] * g_s[kk]
    md = mod_ref[0]
    o_ref[...] = _layer_norm(dn_alpha * x_ref[...] + md[5:6] * f, lng_ref[...], lnb_ref[...])


def _combine(dest_flat, ys, gate_w, x1, modv, ln_g, ln_b, n_lat_blocks, dn_alpha):
    t = x1.shape[0]
    return pl.pallas_call(
        functools.partial(_combine_kernel, dn_alpha=dn_alpha),
        out_shape=jax.ShapeDtypeStruct((t, D_MODEL), F32),
        grid=(t // TM,),
        in_specs=[
            pl.BlockSpec((TM * TOP_K,), lambda i: (i,), memory_space=pltpu.SMEM),
            pl.BlockSpec(memory_space=pl.ANY),
            pl.BlockSpec((TM, TOP_K), lambda i: (i, 0)),
            pl.BlockSpec((TM, D_MODEL), lambda i: (i, 0)),
            pl.BlockSpec((1, 6, D_MODEL), lambda i: (i // n_lat_blocks, 0, 0)),
            pl.BlockSpec((1, D_MODEL), lambda i: (0, 0)),
            pl.BlockSpec((1, D_MODEL), lambda i: (0, 0)),
        ],
        out_specs=pl.BlockSpec((TM, D_MODEL), lambda i: (i, 0)),
        scratch_shapes=[pltpu.VMEM((TOP_K, TM, D_MODEL), F32), pltpu.SemaphoreType.DMA],
        compiler_params=_cp(("arbitrary",)),
        name="moe_combine",
    )(dest_flat, ys, gate_w, x1, modv, ln_g, ln_b)


def _moe(v, x1, modv, w_router, b_router, w_gu, b_gu, w_down, b_down, ln_g, ln_b, n_lat_blocks, dn_alpha):
    t = v.shape[0]
    top_i, gate_w, rank, counts = _router(v, w_router, b_router.reshape(1, N_EXPERTS))
    counts = counts[0]
    padded = (counts + MOE_BLK - 1) // MOE_BLK * MOE_BLK
    pend = jnp.cumsum(padded)
    pstart = pend - padded
    dest = (pstart[top_i] + rank).reshape(-1).astype(jnp.int32)
    n_slots = t * TOP_K + N_EXPERTS * MOE_BLK
    nblk = n_slots // MOE_BLK
    starts = jnp.arange(nblk, dtype=jnp.int32) * MOE_BLK
    blk_e = jnp.minimum(jnp.sum((pend[None, :] <= starts[:, None]).astype(jnp.int32), axis=1),
                        N_EXPERTS - 1).astype(jnp.int32)
    n_used = (pend[-1:] // MOE_BLK).astype(jnp.int32)
    xs = _dispatch(dest, v, n_slots)
    ys = _experts(blk_e, n_used, xs, w_gu, b_gu, w_down, b_down)
    return _combine(dest, ys, gate_w, x1, modv, ln_g, ln_b, n_lat_blocks, dn_alpha)


def _block_diag(w):
    nbk, bs, _ = w.shape
    eye = jnp.eye(nbk, dtype=w.dtype)
    return jnp.einsum('hij,hg->higj', w, eye).reshape(nbk * bs, nbk * bs)


def _query_block(n_lat):
    for tq in (1024, 512):
        if n_lat % tq == 0:
            return tq
    return TM


def _key_block(t):
    for tk in (1280, 640, 256):
        if t % tk == 0:
            return tk
    raise ValueError("token count must be a multiple of 256")


def kernel(x, c, ctx, c_ctx, w_mod, b_mod, w_in, conv_w, conv_b, lru_wr, lru_br, lru_wi, lru_bi, lru_lam, s5_a_re, s5_a_im, s5_log_dt, s5_b_re, s5_b_im, s5_c_re, s5_c_im, s5_d, s5_w_glu, s5_b_glu, da_lam, da_subln_g, w_proj, b_gate, w_out, ln_g, ln_b, w_router, b_router, w_gu, b_gu, w_down, b_down):
    bsz, n_lat, d = x.shape
    n_ctx = ctx.shape[1]
    depth = w_mod.shape[0]
    assert bsz == 1 and d == D_MODEL and n_ctx == TM and n_lat % TM == 0 and n_lat % GRID_W == 0
    nb = n_lat // TM
    t = n_lat + n_ctx
    dn_alpha = (2 * depth) ** 0.25

    cc = jnp.zeros((SUBLANES, D_MODEL), F32).at[0].set(c[0]).at[1].set(c_ctx)
    mod = _modulation(cc, w_mod, b_mod)
    cos_t, sin_t = _rope_tables(n_lat, n_ctx)
    tq = _query_block(n_lat)
    tk = _key_block(t)

    xs = jnp.concatenate([x[0], ctx[0]], axis=0)
    for l in range(depth):
        modv = mod[l, 0:2].reshape(2, 6, D_MODEL)
        hmat = _inproj(xs, modv, w_in[l].astype(BF16), nb)

        sp = jax.nn.softplus(-lru_lam[l])
        hdir, sdir = [], []
        for dr in range(2):
            hdir.append(_lru(hmat, nb, dr == 1, conv_w[l], conv_b[l].reshape(1, -1),
                             _block_diag(lru_wr[l, dr]).astype(BF16), lru_br[l, dr].reshape(1, -1),
                             _block_diag(lru_wi[l, dr]).astype(BF16), lru_bi[l, dr].reshape(1, -1),
                             sp[dr].reshape(1, -1)))
            prm = _s5_params(s5_a_re[l, dr], s5_a_im[l, dr], s5_log_dt[l, dr], s5_b_re[l, dr],
                             s5_b_im[l, dr], s5_c_re[l, dr], s5_c_im[l, dr])
            sdir.append(_s5(hmat, nb, dr == 1, *prm))

        qa, qb, kk, vv = _qkv_prep(hmat, cos_t, sin_t)
        lam_init = 0.8 - 0.6 * math.exp(-0.3 * l)
        g_row = da_subln_g[l].reshape(1, DA_V_DIM)
        yc = jnp.zeros((t, DA_HEADS * DA_V_DIM), F32)
        yc = _attention(qa, qb, kk, vv, da_lam[l], g_row, lam_init, yc,
                        q0=0, nq=n_lat // tq, tq=tq, k0=0, nk=t // tk, tk=tk, name="attn_lat")
        yc = _attention(qa, qb, kk, vv, da_lam[l], g_row, lam_init, yc,
                        q0=nb, nq=1, tq=TM, k0=nb, nk=1, tk=TM, name="attn_ctx")

        x1, v = _merge(xs, modv, hmat, hdir[0], hdir[1], sdir[0], sdir[1], yc,
                       s5_d[l].reshape(1, -1), s5_w_glu[l].astype(BF16), s5_b_glu[l].reshape(1, -1),
                       w_proj[l].astype(BF16), b_gate[l], w_out[l].astype(BF16),
                       ln_g[l, 0].reshape(1, -1), ln_b[l, 0].reshape(1, -1), nb, dn_alpha)
        xs = _moe(v, x1, modv, w_router[l], b_router[l], w_gu[l], b_gu[l], w_down[l], b_down[l],
                  ln_g[l, 1].reshape(1, -1), ln_b[l, 1].reshape(1, -1), nb, dn_alpha)
    return xs[:n_lat][None]
```

```python
import functools
import math

import numpy as np
import jax
import jax.numpy as jnp
from jax import lax
from jax.experimental import pallas as pl
from jax.experimental.pallas import tpu as pltpu

F32 = jnp.float32
BF16 = jnp.bfloat16
HIGHEST = lax.Precision.HIGHEST

D_MODEL = 1024
GRID_W = 64
LRU_WIDTH = 512
LRU_BLOCKS = 8
CONV_W = 4
LRU_C = 8.0
S5_WIDTH = 512
S5_GROUP = 16
S5_GROUPS = S5_WIDTH // S5_GROUP
S5_STATE = 64
S5_N = S5_GROUPS * S5_STATE
DA_HEADS = 4
DA_HEAD_DIM = 64
DA_V_DIM = 128
DA_SCALE = DA_HEAD_DIM ** -0.5
ROPE_BASE = 10000.0
ROPE_FREQS = DA_HEAD_DIM // 4
N_BRANCH = 3
BW = 512
IN_COLS = 6 * BW + N_BRANCH * D_MODEL
N_EXPERTS = 32
TOP_K = 4
D_FF = D_MODEL
SWIGLU_ALPHA = 1.702
SWIGLU_LIMIT = 7.0
LN_EPS = 1e-5
RMS_EPS = 1e-6

SUBLANES = 8
LANES = 128
VMEM_LIMIT = 56 * 1024 * 1024

TM = 256
HALO = 16
SEG = TM // SUBLANES
S5_CHUNKS = S5_WIDTH // LANES
S5_SCAN_CHUNKS = 2
LOG2_E = math.log2(math.e)
MOE_BLK = 256
TN_IN = 3072


def _cp(sem, vmem=VMEM_LIMIT):
    return pltpu.CompilerParams(dimension_semantics=sem, vmem_limit_bytes=vmem)


def _sigmoid(x):
    return 1.0 / (1.0 + jnp.exp(-x))


def _gelu_tanh(x):
    return 0.5 * x * (1.0 + jnp.tanh(math.sqrt(2.0 / math.pi) * (x + 0.044715 * (x * x * x))))


def _layer_norm(y, g, b):
    mu = jnp.mean(y, axis=-1, keepdims=True)
    yc = y - mu
    var = jnp.mean(yc * yc, axis=-1, keepdims=True)
    return yc * lax.rsqrt(var + LN_EPS) * g + b


def _mod_kernel(c_ref, w_ref, b_ref, o_ref):
    c = c_ref[...]
    s = c * _sigmoid(c)
    o_ref[0] = jnp.dot(s, w_ref[0], preferred_element_type=F32, precision=HIGHEST) + b_ref[0]


def _modulation(cc, w_mod, b_mod):
    depth = w_mod.shape[0]
    tn = 1536
    return pl.pallas_call(
        _mod_kernel,
        out_shape=jax.ShapeDtypeStruct((depth, SUBLANES, 6 * D_MODEL), F32),
        grid=(depth, 6 * D_MODEL // tn),
        in_specs=[
            pl.BlockSpec((SUBLANES, D_MODEL), lambda l, j: (0, 0)),
            pl.BlockSpec((1, D_MODEL, tn), lambda l, j: (l, 0, j)),
            pl.BlockSpec((1, 1, tn), lambda l, j: (l, 0, j)),
        ],
        out_specs=pl.BlockSpec((1, SUBLANES, tn), lambda l, j: (l, 0, j)),
        compiler_params=_cp(("arbitrary", "arbitrary")),
        name="modulation",
    )(cc, w_mod, b_mod.reshape(depth, 1, 6 * D_MODEL))


def _inproj_kernel(x_ref, mod_ref, w_ref, o_ref):
    md = mod_ref[0]
    u = x_ref[...] * (1.0 + md[1:2]) + md[0:1]
    o_ref[...] = jnp.dot(u.astype(BF16), w_ref[...], preferred_element_type=F32).astype(BF16)


def _inproj(x, modv, w_bf16, n_lat_blocks):
    t = x.shape[0]
    return pl.pallas_call(
        _inproj_kernel,
        out_shape=jax.ShapeDtypeStruct((t, IN_COLS), BF16),
        grid=(IN_COLS // TN_IN, t // TM),
        in_specs=[
            pl.BlockSpec((TM, D_MODEL), lambda j, i: (i, 0)),
            pl.BlockSpec((1, 6, D_MODEL), lambda j, i: (i // n_lat_blocks, 0, 0)),
            pl.BlockSpec((D_MODEL, TN_IN), lambda j, i: (0, j)),
        ],
        out_specs=pl.BlockSpec((TM, TN_IN), lambda j, i: (i, j)),
        compiler_params=_cp(("arbitrary", "arbitrary")),
        name="inproj",
    )(x, modv, w_bf16)


def _scan_block(s, nb, reverse):
    return (nb - s) if reverse else (s + nb) % (nb + 1)


def _lru_kernel(prev_ref, cur_ref, next_ref, cw_ref, cb_ref, wr_ref, br_ref, wi_ref, bi_ref,
                sp_ref, o_ref, a_s, b_s, h_s, *, nb, reverse):
    s = pl.program_id(0)
    blk = _scan_block(s, nb, reverse)

    @pl.when(s == 0)
    def _():
        h_s[...] = jnp.zeros_like(h_s)

    has_prev = jnp.logical_and(blk != 0, blk != nb)
    has_next = jnp.logical_and(blk != nb - 1, blk != nb)
    prev = jnp.where(has_prev, prev_ref[...].astype(F32), 0.0)
    nxt = jnp.where(has_next, next_ref[...].astype(F32), 0.0)
    ext = jnp.concatenate([prev, cur_ref[...].astype(F32), nxt], axis=0)
    cw = cw_ref[...]
    xc = cb_ref[...] + jnp.zeros((TM, LRU_WIDTH), F32)
    for k in range(CONV_W):
        off = HALO - CONV_W // 2 + k
        xc = xc + ext[off:off + TM] * cw[k:k + 1]
    xb = xc.astype(BF16)
    r = _sigmoid(jnp.dot(xb, wr_ref[...], preferred_element_type=F32) + br_ref[...])
    i = _sigmoid(jnp.dot(xb, wi_ref[...], preferred_element_type=F32) + bi_ref[...])
    a = jnp.exp(-LRU_C * r * sp_ref[...])
    a_s[...] = a
    b_s[...] = jnp.sqrt(1.0 - a * a) * (i * xc)

    def step(n, h):
        t = (TM - 1 - n) if reverse else n
        h = a_s[pl.ds(t, 1), :] * h + b_s[pl.ds(t, 1), :]
        o_ref[pl.ds(t, 1), :] = h
        return h

    h_s[...] = lax.fori_loop(0, TM, step, h_s[...], unroll=8)


def _lru(hmat, nb, reverse, conv_w, conv_b, wr, br, wi, bi, sp):
    t = hmat.shape[0]
    nblk = nb + 1
    rh = TM // HALO
    last_h = t // HALO - 1

    def bmap(s):
        return _scan_block(s, nb, reverse)

    vec = lambda: pl.BlockSpec((1, LRU_WIDTH), lambda s: (0, 0))
    mat = lambda: pl.BlockSpec((LRU_WIDTH, LRU_WIDTH), lambda s: (0, 0))
    return pl.pallas_call(
        functools.partial(_lru_kernel, nb=nb, reverse=reverse),
        out_shape=jax.ShapeDtypeStruct((t, LRU_WIDTH), F32),
        grid=(nblk,),
        in_specs=[
            pl.BlockSpec((HALO, LRU_WIDTH), lambda s: (jnp.maximum(bmap(s) * rh - 1, 0), 0)),
            pl.BlockSpec((TM, LRU_WIDTH), lambda s: (bmap(s), 0)),
            pl.BlockSpec((HALO, LRU_WIDTH), lambda s: (jnp.minimum((bmap(s) + 1) * rh, last_h), 0)),
            pl.BlockSpec((CONV_W, LRU_WIDTH), lambda s: (0, 0)),
            vec(), mat(), vec(), mat(), vec(), vec(),
        ],
        out_specs=pl.BlockSpec((TM, LRU_WIDTH), lambda s: (bmap(s), 0)),
        scratch_shapes=[pltpu.VMEM((TM, LRU_WIDTH), F32), pltpu.VMEM((TM, LRU_WIDTH), F32),
                        pltpu.VMEM((1, LRU_WIDTH), F32)],
        compiler_params=_cp(("arbitrary",)),
        name="lru_bwd" if reverse else "lru_fwd",
    )(hmat, hmat, hmat, conv_w, conv_b, wr, br, wi, bi, sp)


def _cmul(ar, ai, br, bi):
    return ar * br - ai * bi, ar * bi + ai * br


def _s5_kernel(u_ref, p_ref, pt_ref, bm_ref, cm_ref, a_ref, pw_ref, o_ref, x_s, st_s, c_s):
    s = pl.program_id(0)
    n = S5_N

    @pl.when(s == 0)
    def _():
        c_s[...] = jnp.zeros_like(c_s)

    up = jnp.dot(p_ref[...], u_ref[...], preferred_element_type=F32).astype(BF16)
    nc = n // S5_CHUNKS
    for c in range(S5_CHUNKS):
        xc = jnp.dot(up[:, c * LANES:(c + 1) * LANES], bm_ref[c], preferred_element_type=F32)
        x_s[:, c * nc:(c + 1) * nc] = xc[:, 0:nc]
        x_s[:, n + c * nc:n + (c + 1) * nc] = xc[:, nc:2 * nc]
    ns = n // S5_SCAN_CHUNKS
    for c in range(S5_SCAN_CHUNKS):
        re = slice(c * ns, (c + 1) * ns)
        im = slice(n + c * ns, n + (c + 1) * ns)
        ar = jnp.broadcast_to(a_ref[0:1, re], (SUBLANES, ns))
        ai = jnp.broadcast_to(a_ref[1:2, re], (SUBLANES, ns))

        def local(m, carry, re=re, im=im, ar=ar, ai=ai):
            hr, hi = carry
            row = pl.multiple_of(m * SUBLANES, SUBLANES)
            pr, pi = _cmul(ar, ai, hr, hi)
            hr = pr + x_s[pl.ds(row, SUBLANES), re]
            hi = pi + x_s[pl.ds(row, SUBLANES), im]
            x_s[pl.ds(row, SUBLANES), re] = hr
            x_s[pl.ds(row, SUBLANES), im] = hi
            return hr, hi

        z = jnp.zeros((SUBLANES, ns), F32)
        fr, fi = lax.fori_loop(0, SEG, local, (z, z), unroll=4)

        qr = pw_ref[TM - 1:TM, re]
        qi = pw_ref[TM - 1:TM, im]
        sr = c_s[0:1, re]
        si = c_s[0:1, im]
        for j in range(SUBLANES):
            st_s[j:j + 1, re] = sr
            st_s[j:j + 1, im] = si
            pr, pi = _cmul(qr, qi, sr, si)
            sr = pr + fr[j:j + 1]
            si = pi + fi[j:j + 1]
        c_s[0:1, re] = sr
        c_s[0:1, im] = si

        str_ = st_s[:, re]
        sti = st_s[:, im]

        def fix(m, _, re=re, im=im, str_=str_, sti=sti):
            row = pl.multiple_of(m * SUBLANES, SUBLANES)
            pr, pi = _cmul(pw_ref[pl.ds(row, SUBLANES), re], pw_ref[pl.ds(row, SUBLANES), im], str_, sti)
            x_s[pl.ds(row, SUBLANES), re] = x_s[pl.ds(row, SUBLANES), re] + pr
            x_s[pl.ds(row, SUBLANES), im] = x_s[pl.ds(row, SUBLANES), im] + pi
            return 0

        lax.fori_loop(0, SEG, fix, 0, unroll=2)

    yp = jnp.concatenate(
        [jnp.dot(x_s[:, c * nc:(c + 1) * nc].astype(BF16), cm_ref[0, c], preferred_element_type=F32)
         + jnp.dot(x_s[:, n + c * nc:n + (c + 1) * nc].astype(BF16), cm_ref[1, c], preferred_element_type=F32)
         for c in range(S5_CHUNKS)], axis=1)
    y_hi = yp.astype(BF16)
    y_lo = (yp - y_hi.astype(F32)).astype(BF16)
    pt = pt_ref[...]
    o_ref[...] = (jnp.dot(pt, y_hi, preferred_element_type=F32)
                  + jnp.dot(pt, y_lo, preferred_element_type=F32))


def _s5_perm(reverse):
    p = np.zeros((TM, TM), np.float32)
    for m in range(SEG):
        for j in range(SUBLANES):
            pos = j * SEG + m
            p[m * SUBLANES + j, (TM - 1 - pos) if reverse else pos] = 1.0
    return p


def _s5(hmat, nb, reverse, bm_t, cm_t, a_row, pw):
    t = hmat.shape[0]
    p = _s5_perm(reverse)
    n2 = 2 * S5_N

    def bmap(s):
        return _scan_block(s, nb, reverse)

    return pl.pallas_call(
        _s5_kernel,
        out_shape=jax.ShapeDtypeStruct((t, S5_WIDTH), F32),
        grid=(nb + 1,),
        in_specs=[
            pl.BlockSpec((TM, S5_WIDTH), lambda s: (bmap(s), 2)),
            pl.BlockSpec((TM, TM), lambda s: (0, 0)),
            pl.BlockSpec((TM, TM), lambda s: (0, 0)),
            pl.BlockSpec((S5_CHUNKS, LANES, n2 // S5_CHUNKS), lambda s: (0, 0, 0)),
            pl.BlockSpec((2, S5_CHUNKS, S5_N // S5_CHUNKS, LANES), lambda s: (0, 0, 0, 0)),
            pl.BlockSpec((2, S5_N), lambda s: (0, 0)),
            pl.BlockSpec((TM, n2), lambda s: (0, 0)),
        ],
        out_specs=pl.BlockSpec((TM, S5_WIDTH), lambda s: (bmap(s), 0)),
        scratch_shapes=[pltpu.VMEM((TM, n2), F32), pltpu.VMEM((SUBLANES, n2), F32),
                        pltpu.VMEM((1, n2), F32)],
        compiler_params=_cp(("arbitrary",)),
        name="s5_bwd" if reverse else "s5_fwd",
    )(hmat, jnp.asarray(p, BF16), jnp.asarray(p.T, BF16), bm_t, cm_t, a_row, pw)


def _s5_params(a_re, a_im, log_dt, b_re, b_im, c_re, c_im):
    g, p = a_re.shape
    dt = jnp.exp(log_dt)[:, None]
    lr, li = a_re, a_im
    ea = jnp.exp(lr * dt)
    ab_r, ab_i = ea * jnp.cos(li * dt), ea * jnp.sin(li * dt)
    den = lr * lr + li * li
    co_r = ((ab_r - 1.0) * lr + ab_i * li) / den
    co_i = (ab_i * lr - (ab_r - 1.0) * li) / den
    bb_r = co_r[..., None] * b_re - co_i[..., None] * b_im
    bb_i = co_r[..., None] * b_im + co_i[..., None] * b_re
    gc = g // S5_CHUNKS
    eye = jnp.eye(gc, dtype=F32)

    def chunked(w, spec):
        return jnp.einsum(spec, w.reshape((S5_CHUNKS, gc) + w.shape[1:]), eye)

    bm_r = chunked(bb_r, 'kgpc,gh->kgchp').reshape(S5_CHUNKS, gc * S5_GROUP, gc * p)
    bm_i = chunked(bb_i, 'kgpc,gh->kgchp').reshape(S5_CHUNKS, gc * S5_GROUP, gc * p)
    bm_t = jnp.concatenate([bm_r, bm_i], axis=2).astype(BF16)
    cm_r = chunked(c_re, 'kgcp,gh->kgphc').reshape(S5_CHUNKS, gc * p, gc * S5_GROUP)
    cm_i = chunked(-c_im, 'kgcp,gh->kgphc').reshape(S5_CHUNKS, gc * p, gc * S5_GROUP)
    cm_t = jnp.stack([cm_r, cm_i], axis=0).astype(BF16)
    a_row = jnp.stack([ab_r.reshape(-1), ab_i.reshape(-1)], axis=0)
    k = jnp.arange(1, SEG + 1, dtype=F32)[:, None, None]
    ek = jnp.exp(k * (lr * dt))
    pw = jnp.concatenate([(ek * jnp.cos(k * (li * dt))).reshape(SEG, -1),
                          (ek * jnp.sin(k * (li * dt))).reshape(SEG, -1)], axis=1)
    return bm_t, cm_t, a_row, jnp.repeat(pw, SUBLANES, axis=0)


def _qkv_kernel(q_ref, k_ref, v_ref, cos_ref, sin_ref, qa_ref, qb_ref, ko_ref, vo_ref):
    w = DA_HEADS * 2 * DA_HEAD_DIM
    cos = jnp.concatenate([cos_ref[...]] * (w // LANES), axis=1)
    sin = jnp.concatenate([sin_ref[...]] * (w // LANES), axis=1)
    lane = lax.broadcasted_iota(jnp.int32, (TM, w), 1)
    first = (lane % (2 * ROPE_FREQS)) < ROPE_FREQS

    def rope(x):
        swapped = jnp.where(first, pltpu.roll(x, w - ROPE_FREQS, 1), pltpu.roll(x, ROPE_FREQS, 1))
        return x * cos + swapped * sin

    q = rope(q_ref[...].astype(F32)) * (DA_SCALE * LOG2_E)
    in_a = (lane % (2 * DA_HEAD_DIM)) < DA_HEAD_DIM
    qa_ref[...] = jnp.where(in_a, q, 0.0).astype(BF16)
    qb_ref[...] = jnp.where(in_a, 0.0, q).astype(BF16)
    ko_ref[...] = rope(k_ref[...].astype(F32)).astype(BF16)
    v = v_ref[...]
    ones = jnp.ones((TM, DA_V_DIM), BF16)
    vo_ref[...] = jnp.concatenate(
        [blk for h in range(DA_HEADS) for blk in (v[:, h * DA_V_DIM:(h + 1) * DA_V_DIM], ones)], axis=1)


def _qkv_prep(hmat, cos_t, sin_t):
    t = hmat.shape[0]
    w = DA_HEADS * 2 * DA_HEAD_DIM
    col = lambda c: pl.BlockSpec((TM, w), lambda i: (i, c))
    tab = lambda: pl.BlockSpec((TM, LANES), lambda i: (i, 0))
    out = lambda: pl.BlockSpec((TM, w), lambda i: (i, 0))
    sds = jax.ShapeDtypeStruct((t, w), BF16)
    return pl.pallas_call(
        _qkv_kernel,
        out_shape=(sds, sds, sds, jax.ShapeDtypeStruct((t, 2 * w), BF16)),
        grid=(t // TM,),
        in_specs=[col(3), col(4), col(5), tab(), tab()],
        out_specs=(out(), out(), out(), pl.BlockSpec((TM, 2 * w), lambda i: (i, 0))),
        compiler_params=_cp(("arbitrary",)),
        name="qkv_prep",
    )(hmat, hmat, hmat, cos_t, sin_t)


def _rope_tables(n_lat, n_ctx):
    pos = jnp.arange(n_lat)
    freqs = ROPE_BASE ** (-jnp.arange(ROPE_FREQS, dtype=F32) / ROPE_FREQS)
    ang_r = (pos // GRID_W).astype(F32)[:, None] * freqs
    ang_c = (pos % GRID_W).astype(F32)[:, None] * freqs
    cr, sr, cc, sc = jnp.cos(ang_r), jnp.sin(ang_r), jnp.cos(ang_c), jnp.sin(ang_c)
    cos64 = jnp.concatenate([cr, cr, cc, cc], axis=1)
    sin64 = jnp.concatenate([-sr, sr, -sc, sc], axis=1)
    cos_t = jnp.concatenate([cos64, cos64], axis=1)
    sin_t = jnp.concatenate([sin64, sin64], axis=1)
    cos_t = jnp.concatenate([cos_t, jnp.ones((n_ctx, LANES), F32)], axis=0)
    sin_t = jnp.concatenate([sin_t, jnp.zeros((n_ctx, LANES), F32)], axis=0)
    return cos_t, sin_t


def _attn_kernel(lam_ref, g_ref, qa_ref, qb_ref, k_ref, v_ref, prev_ref, o_ref, m_s, acc_s, a_s, s_s,
                 *, lam_init, nk):
    del prev_ref
    g = pl.program_id(1)
    kj = g % nk

    @pl.when(g == 0)
    def _():
        s_s[...] = jnp.zeros_like(s_s)
        acc_s[...] = jnp.zeros_like(acc_s)
        a_s[...] = jnp.zeros_like(a_s)
        m_s[...] = jnp.zeros_like(m_s)

    def step(cur):
        k = k_ref[...]
        v = v_ref[...]
        for i, q_ref in enumerate((qa_ref, qb_ref)):
            m_old = m_s[i]
            p = jnp.exp2(s_s[1 - cur, i] - m_old[:, 0:1]).astype(BF16)
            alpha = a_s[i]
            acc_s[i] = (jnp.concatenate([alpha, alpha], axis=1) * acc_s[i]
                        + jnp.dot(p, v, preferred_element_type=F32))
            s = lax.dot_general(q_ref[...], k, (((1,), (1,)), ((), ())), preferred_element_type=F32)
            s_s[cur, i] = s
            m_base = jnp.where(kj == 0, -jnp.inf, m_old)
            m_new = jnp.maximum(m_base, jnp.max(s, axis=1, keepdims=True))
            a_s[i] = jnp.exp2(m_base - m_new)
            m_s[i] = m_new

    def finish():
        lv = lam_ref[...]
        lam = (jnp.exp(jnp.sum(lv[0:1] * lv[1:2], axis=1, keepdims=True))
               - jnp.exp(jnp.sum(lv[2:3] * lv[3:4], axis=1, keepdims=True)) + lam_init)
        a0, a1 = acc_s[0], acc_s[1]
        o = (a0[:, 0:DA_V_DIM] / a0[:, DA_V_DIM:2 * DA_V_DIM]
             - lam * (a1[:, 0:DA_V_DIM] / a1[:, DA_V_DIM:2 * DA_V_DIM]))
        ms = jnp.mean(o * o, axis=1, keepdims=True)
        o_ref[...] = o * lax.rsqrt(ms + RMS_EPS) * g_ref[...] * (1.0 - lam_init)

    for parity in range(2):
        pl.when(g % 2 == parity)(functools.partial(step, parity))
    pl.when(jnp.logical_and(kj == 0, g > 0))(finish)


def _attention(qa, qb, k, v, lam_vec, subln_g, lam_init, prev, *, q0, nq, tq, k0, nk, tk, name):
    t = qa.shape[0]
    n_pairs = nq * nk

    def qmap(h, g):
        return (q0 + jnp.minimum(g // nk, nq - 1), h)

    return pl.pallas_call(
        functools.partial(_attn_kernel, lam_init=lam_init, nk=nk),
        out_shape=jax.ShapeDtypeStruct((t, DA_HEADS * DA_V_DIM), F32),
        grid=(DA_HEADS, n_pairs + 1),
        in_specs=[
            pl.BlockSpec((4, DA_HEAD_DIM), lambda h, g: (0, 0)),
            pl.BlockSpec((1, DA_V_DIM), lambda h, g: (0, 0)),
            pl.BlockSpec((tq, LANES), qmap),
            pl.BlockSpec((tq, LANES), qmap),
            pl.BlockSpec((tk, LANES), lambda h, g: (k0 + g % nk, h)),
            pl.BlockSpec((tk, 2 * DA_V_DIM), lambda h, g: (k0 + (g + nk - 1) % nk, h)),
            pl.BlockSpec(memory_space=pl.ANY),
        ],
        out_specs=pl.BlockSpec((tq, LANES), lambda h, g: (q0 + jnp.maximum(g - 1, 0) // nk, h)),
        scratch_shapes=[pltpu.VMEM((2, tq, LANES), F32), pltpu.VMEM((2, tq, 2 * DA_V_DIM), F32),
                        pltpu.VMEM((2, tq, LANES), F32), pltpu.VMEM((2, 2, tq, tk), F32)],
        input_output_aliases={6: 0},
        compiler_params=_cp(("arbitrary", "arbitrary")),
        name=name,
    )(lam_vec, subln_g, qa, qb, k, v, prev)


def _merge_kernel(x_ref, mod_ref, gt_ref, ga_ref, ub_ref, hf_ref, hb_ref, sf_ref, sb_ref, yc_ref,
                  sd_ref, wglu_ref, bglu_ref, wp_ref, bg_ref, wo_ref, lng_ref, lnb_ref,
                  x1_ref, v_ref, *, dn_alpha):
    md = mod_ref[0]
    ya = (hf_ref[...] + hb_ref[...]) * _gelu_tanh(ga_ref[...].astype(F32))
    y = sf_ref[...] + sb_ref[...] + sd_ref[...] * ub_ref[...].astype(F32)
    tg = jnp.dot(_gelu_tanh(y).astype(BF16), wglu_ref[...], preferred_element_type=F32) + bglu_ref[...]
    yb = tg[:, 0:S5_WIDTH] * _sigmoid(tg[:, S5_WIDTH:2 * S5_WIDTH])
    yc = yc_ref[...]
    bg = bg_ref[...]
    z = jnp.zeros((TM, D_MODEL), F32)
    for n, br in enumerate((ya, yb, yc)):
        gate = _sigmoid(gt_ref[:, n * D_MODEL:(n + 1) * D_MODEL].astype(F32) + bg[n:n + 1])
        z = z + gate * jnp.dot(br.astype(BF16), wp_ref[n], preferred_element_type=F32)
    m = jnp.dot(z.astype(BF16), wo_ref[...], preferred_element_type=F32)
    x1 = _layer_norm(dn_alpha * x_ref[...] + md[2:3] * m, lng_ref[...], lnb_ref[...])
    x1_ref[...] = x1
    v_ref[...] = x1 * (1.0 + md[4:5]) + md[3:4]


def _merge(x, modv, hmat, hf, hb, sf, sb, yc, s5_d, w_glu, b_glu, w_proj, b_gate, w_out, ln_g, ln_b,
           n_lat_blocks, dn_alpha):
    t = x.shape[0]
    row = lambda w, c=0: pl.BlockSpec((TM, w), lambda i: (i, c))
    const = lambda *shape: pl.BlockSpec(shape, lambda i: (0,) * len(shape))
    sds = jax.ShapeDtypeStruct((t, D_MODEL), F32)
    return pl.pallas_call(
        functools.partial(_merge_kernel, dn_alpha=dn_alpha),
        out_shape=(sds, sds),
        grid=(t // TM,),
        in_specs=[
            row(D_MODEL),
            pl.BlockSpec((1, 6, D_MODEL), lambda i: (i // n_lat_blocks, 0, 0)),
            row(N_BRANCH * D_MODEL, 1), row(BW, 1), row(BW, 2),
            row(BW), row(BW), row(BW), row(BW), row(BW),
            const(1, S5_WIDTH), const(S5_WIDTH, 2 * S5_WIDTH), const(1, 2 * S5_WIDTH),
            const(N_BRANCH, BW, D_MODEL), const(N_BRANCH, D_MODEL), const(D_MODEL, D_MODEL),
            const(1, D_MODEL), const(1, D_MODEL),
        ],
        out_specs=(row(D_MODEL), row(D_MODEL)),
        compiler_params=_cp(("arbitrary",)),
        name="merge",
    )(x, modv, hmat, hmat, hmat, hf, hb, sf, sb, yc, s5_d, w_glu, b_glu, w_proj, b_gate, w_out,
      ln_g, ln_b)


def _router_kernel(v_ref, w_ref, b_ref, tri_ref, ti_ref, gw_ref, rk_ref, cnt_ref, c_s):
    s = pl.program_id(0)

    @pl.when(s == 0)
    def _():
        c_s[...] = jnp.zeros_like(c_s)

    logits = jnp.dot(v_ref[...], w_ref[...], preferred_element_type=F32, precision=HIGHEST) + b_ref[...]
    lane = lax.broadcasted_iota(jnp.int32, (TM, N_EXPERTS), 1).astype(F32)
    work = logits
    vals, idxs = [], []
    hot = jnp.zeros((TM, N_EXPERTS), F32)
    for _ in range(TOP_K):
        mx = jnp.max(work, axis=1, keepdims=True)
        ix = jnp.min(jnp.where(work == mx, lane, float(N_EXPERTS)), axis=1, keepdims=True)
        sel = lane == ix
        hot = jnp.where(sel, 1.0, hot)
        work = jnp.where(sel, -jnp.inf, work)
        vals.append(mx)
        idxs.append(ix)
    es = [jnp.exp(vv - vals[0]) for vv in vals]
    den = es[0] + es[1] + es[2] + es[3]
    before = jnp.dot(tri_ref[...], hot.astype(BF16), preferred_element_type=F32) + c_s[0:1, :]
    col = lax.broadcasted_iota(jnp.int32, (TM, TOP_K), 1)
    ti = jnp.zeros((TM, TOP_K), F32)
    gw = jnp.zeros((TM, TOP_K), F32)
    rk = jnp.zeros((TM, TOP_K), F32)
    for kk in range(TOP_K):
        rank = jnp.sum(jnp.where(lane == idxs[kk], before, 0.0), axis=1, keepdims=True)
        ti = jnp.where(col == kk, idxs[kk], ti)
        gw = jnp.where(col == kk, es[kk] / den, gw)
        rk = jnp.where(col == kk, rank, rk)
    ti_ref[...] = ti.astype(jnp.int32)
    gw_ref[...] = gw
    rk_ref[...] = rk.astype(jnp.int32)
    tot = c_s[0:1, :] + jnp.sum(hot, axis=0, keepdims=True)
    c_s[...] = jnp.broadcast_to(tot, c_s.shape)
    cnt_ref[...] = jnp.broadcast_to(tot, cnt_ref.shape).astype(jnp.int32)


def _router(v, w_router, b_router):
    t = v.shape[0]
    tri = jnp.asarray(np.tril(np.ones((TM, TM), np.float32), -1), BF16)
    row4 = lambda: pl.BlockSpec((TM, TOP_K), lambda i: (i, 0))
    return pl.pallas_call(
        _router_kernel,
        out_shape=(jax.ShapeDtypeStruct((t, TOP_K), jnp.int32),
                   jax.ShapeDtypeStruct((t, TOP_K), F32),
                   jax.ShapeDtypeStruct((t, TOP_K), jnp.int32),
                   jax.ShapeDtypeStruct((SUBLANES, N_EXPERTS), jnp.int32)),
        grid=(t // TM,),
        in_specs=[
            pl.BlockSpec((TM, D_MODEL), lambda i: (i, 0)),
            pl.BlockSpec((D_MODEL, N_EXPERTS), lambda i: (0, 0)),
            pl.BlockSpec((1, N_EXPERTS), lambda i: (0, 0)),
            pl.BlockSpec((TM, TM), lambda i: (0, 0)),
        ],
        out_specs=(row4(), row4(), row4(), pl.BlockSpec((SUBLANES, N_EXPERTS), lambda i: (0, 0))),
        scratch_shapes=[pltpu.VMEM((SUBLANES, N_EXPERTS), F32)],
        compiler_params=_cp(("arbitrary",)),
        name="router",
    )(v, w_router, b_router, tri)


def _dispatch_kernel(dest_ref, v_ref, xs_in, xs_hbm, sem):
    del xs_in

    def row_copy(r, kk):
        return pltpu.make_async_copy(v_ref.at[pl.ds(r, 1)],
                                     xs_hbm.at[pl.ds(dest_ref[r * TOP_K + kk], 1)], sem)

    def start(r, _):
        for kk in range(TOP_K):
            row_copy(r, kk).start()
        return 0

    def wait(r, _):
        for kk in range(TOP_K):
            row_copy(r, kk).wait()
        return 0

    lax.fori_loop(0, TM, start, 0, unroll=8)
    lax.fori_loop(0, TM, wait, 0, unroll=8)


def _dispatch(dest_flat, v, n_slots):
    t = v.shape[0]
    xs0 = jnp.zeros((n_slots, D_MODEL), F32)
    return pl.pallas_call(
        _dispatch_kernel,
        out_shape=jax.ShapeDtypeStruct((n_slots, D_MODEL), F32),
        grid=(t // TM,),
        in_specs=[
            pl.BlockSpec((TM * TOP_K,), lambda i: (i,), memory_space=pltpu.SMEM),
            pl.BlockSpec((TM, D_MODEL), lambda i: (i, 0)),
            pl.BlockSpec(memory_space=pl.ANY),
        ],
        out_specs=pl.BlockSpec(memory_space=pl.ANY),
        scratch_shapes=[pltpu.SemaphoreType.DMA],
        input_output_aliases={2: 0},
        compiler_params=_cp(("arbitrary",)),
        name="moe_dispatch",
    )(dest_flat, v, xs0)


def _expert_kernel(be_ref, nu_ref, x_ref, wgu_ref, bgu_ref, wd_ref, bd_ref, o_ref, wgu_s, wd_s):
    b = pl.program_id(0)
    new_expert = jnp.logical_or(b == 0, be_ref[b] != be_ref[jnp.maximum(b - 1, 0)])

    @pl.when(jnp.logical_and(new_expert, b < nu_ref[0]))
    def _():
        wgu_s[...] = wgu_ref[0, 0].astype(BF16)
        wd_s[...] = wd_ref[0, 0].astype(BF16)

    @pl.when(b < nu_ref[0])
    def _():
        h = jnp.dot(x_ref[...].astype(BF16), wgu_s[...], preferred_element_type=F32) + bgu_ref[0]
        hg = jnp.minimum(h[:, 0:D_FF], SWIGLU_LIMIT)
        hl = jnp.clip(h[:, D_FF:2 * D_FF], -SWIGLU_LIMIT, SWIGLU_LIMIT)
        act = hg * _sigmoid(SWIGLU_ALPHA * hg) * (hl + 1.0)
        o_ref[...] = jnp.dot(act.astype(BF16), wd_s[...], preferred_element_type=F32) + bd_ref[0]

    @pl.when(pl.program_id(0) >= nu_ref[0])
    def _():
        o_ref[...] = jnp.zeros_like(o_ref)


def _experts(blk_e, n_used, xs, layer, w_gu, b_gu, w_down, b_down):
    n_slots = xs.shape[0]
    nblk = n_slots // MOE_BLK
    grid_spec = pltpu.PrefetchScalarGridSpec(
        num_scalar_prefetch=2,
        grid=(nblk,),
        in_specs=[
            pl.BlockSpec((MOE_BLK, D_MODEL), lambda b, be, nu: (jnp.minimum(b, nu[0] - 1), 0)),
            pl.BlockSpec((1, 1, D_MODEL, 2 * D_FF), lambda b, be, nu: (layer, be[b], 0, 0)),
            pl.BlockSpec((1, 1, 2 * D_FF), lambda b, be, nu: (be[b], 0, 0)),
            pl.BlockSpec((1, 1, D_FF, D_MODEL), lambda b, be, nu: (layer, be[b], 0, 0)),
            pl.BlockSpec((1, 1, D_MODEL), lambda b, be, nu: (be[b], 0, 0)),
        ],
        out_specs=pl.BlockSpec((MOE_BLK, D_MODEL), lambda b, be, nu: (b, 0)),
        scratch_shapes=[pltpu.VMEM((D_MODEL, 2 * D_FF), BF16), pltpu.VMEM((D_FF, D_MODEL), BF16)],
    )
    return pl.pallas_call(
        _expert_kernel,
        out_shape=jax.ShapeDtypeStruct((n_slots, D_MODEL), F32),
        grid_spec=grid_spec,
        compiler_params=_cp(("arbitrary",)),
        name="moe_experts",
    )(blk_e, n_used, xs, w_gu, b_gu.reshape(N_EXPERTS, 1, 2 * D_FF), w_down,
      b_down.reshape(N_EXPERTS, 1, D_MODEL))


def _combine_kernel(dest_ref, ys_hbm, gw_ref, x_ref, mod_ref, lng_ref, lnb_ref, o_ref, g_s, sem,
                    *, dn_alpha):
    def row_copy(r, kk):
        return pltpu.make_async_copy(ys_hbm.at[pl.ds(dest_ref[r * TOP_K + kk], 1)],
                                     g_s.at[kk, pl.ds(r, 1)], sem)

    def start(r, _):
        for kk in range(TOP_K):
            row_copy(r, kk).start()
        return 0

    def wait(r, _):
        for kk in range(TOP_K):
            row_copy(r, kk).wait()
        return 0

    lax.fori_loop(0, TM, start, 0, unroll=8)
    lax.fori_loop(0, TM, wait, 0, unroll=8)
    gw = gw_ref[...]
    f = jnp.zeros((TM, D_MODEL), F32)
    for kk in range(TOP_K):
        f = f + gw[:, kk:kk + 1] * g_s[kk]
    md = mod_ref[0]
    o_ref[...] = _layer_norm(dn_alpha * x_ref[...] + md[5:6] * f, lng_ref[...], lnb_ref[...])


def _combine(dest_flat, ys, gate_w, x1, modv, ln_g, ln_b, n_lat_blocks, dn_alpha):
    t = x1.shape[0]
    return pl.pallas_call(
        functools.partial(_combine_kernel, dn_alpha=dn_alpha),
        out_shape=jax.ShapeDtypeStruct((t, D_MODEL), F32),
        grid=(t // TM,),
        in_specs=[
            pl.BlockSpec((TM * TOP_K,), lambda i: (i,), memory_space=pltpu.SMEM),
            pl.BlockSpec(memory_space=pl.ANY),
            pl.BlockSpec((TM, TOP_K), lambda i: (i, 0)),
            pl.BlockSpec((TM, D_MODEL), lambda i: (i, 0)),
            pl.BlockSpec((1, 6, D_MODEL), lambda i: (i // n_lat_blocks, 0, 0)),
            pl.BlockSpec((1, D_MODEL), lambda i: (0, 0)),
            pl.BlockSpec((1, D_MODEL), lambda i: (0, 0)),
        ],
        out_specs=pl.BlockSpec((TM, D_MODEL), lambda i: (i, 0)),
        scratch_shapes=[pltpu.VMEM((TOP_K, TM, D_MODEL), F32), pltpu.SemaphoreType.DMA],
        compiler_params=_cp(("arbitrary",)),
        name="moe_combine",
    )(dest_flat, ys, gate_w, x1, modv, ln_g, ln_b)


def _moe(v, x1, modv, layer, w_router, b_router, w_gu, b_gu, w_down, b_down, ln_g, ln_b, n_lat_blocks,
         dn_alpha):
    t = v.shape[0]
    top_i, gate_w, rank, counts = _router(v, w_router, b_router.reshape(1, N_EXPERTS))
    counts = counts[0]
    padded = (counts + MOE_BLK - 1) // MOE_BLK * MOE_BLK
    pend = jnp.cumsum(padded)
    pstart = pend - padded
    dest = (pstart[top_i] + rank).reshape(-1).astype(jnp.int32)
    n_slots = t * TOP_K + N_EXPERTS * MOE_BLK
    nblk = n_slots // MOE_BLK
    starts = jnp.arange(nblk, dtype=jnp.int32) * MOE_BLK
    blk_e = jnp.minimum(jnp.sum((pend[None, :] <= starts[:, None]).astype(jnp.int32), axis=1),
                        N_EXPERTS - 1).astype(jnp.int32)
    n_used = (pend[-1:] // MOE_BLK).astype(jnp.int32)
    xs = _dispatch(dest, v, n_slots)
    ys = _experts(blk_e, n_used, xs, layer, w_gu, b_gu, w_down, b_down)
    return _combine(dest, ys, gate_w, x1, modv, ln_g, ln_b, n_lat_blocks, dn_alpha)


def _block_diag(w):
    nbk, bs, _ = w.shape
    eye = jnp.eye(nbk, dtype=w.dtype)
    return jnp.einsum('hij,hg->higj', w, eye).reshape(nbk * bs, nbk * bs)


def _query_block(n_lat):
    for tq in (1024, 512):
        if n_lat % tq == 0:
            return tq
    return TM


def _key_block(t):
    for tk in (1280, 640, 256):
        if t % tk == 0:
            return tk
    raise ValueError("token count must be a multiple of 256")


def kernel(x, c, ctx, c_ctx, w_mod, b_mod, w_in, conv_w, conv_b, lru_wr, lru_br, lru_wi, lru_bi, lru_lam, s5_a_re, s5_a_im, s5_log_dt, s5_b_re, s5_b_im, s5_c_re, s5_c_im, s5_d, s5_w_glu, s5_b_glu, da_lam, da_subln_g, w_proj, b_gate, w_out, ln_g, ln_b, w_router, b_router, w_gu, b_gu, w_down, b_down):
    bsz, n_lat, d = x.shape
    n_ctx = ctx.shape[1]
    depth = w_mod.shape[0]
    assert bsz == 1 and d == D_MODEL and n_ctx == TM and n_lat % TM == 0 and n_lat % GRID_W == 0
    nb = n_lat // TM
    t = n_lat + n_ctx
    dn_alpha = (2 * depth) ** 0.25

    cc = jnp.zeros((SUBLANES, D_MODEL), F32).at[0].set(c[0]).at[1].set(c_ctx)
    mod = _modulation(cc, w_mod, b_mod)
    cos_t, sin_t = _rope_tables(n_lat, n_ctx)
    tq = _query_block(n_lat)
    tk = _key_block(t)

    xs = jnp.concatenate([x[0], ctx[0]], axis=0)
    for l in range(depth):
        modv = mod[l, 0:2].reshape(2, 6, D_MODEL)
        hmat = _inproj(xs, modv, w_in[l].astype(BF16), nb)

        sp = jax.nn.softplus(-lru_lam[l])
        hdir, sdir = [], []
        for dr in range(2):
            hdir.append(_lru(hmat, nb, dr == 1, conv_w[l], conv_b[l].reshape(1, -1),
                             _block_diag(lru_wr[l, dr]).astype(BF16), lru_br[l, dr].reshape(1, -1),
                             _block_diag(lru_wi[l, dr]).astype(BF16), lru_bi[l, dr].reshape(1, -1),
                             sp[dr].reshape(1, -1)))
            prm = _s5_params(s5_a_re[l, dr], s5_a_im[l, dr], s5_log_dt[l, dr], s5_b_re[l, dr],
                             s5_b_im[l, dr], s5_c_re[l, dr], s5_c_im[l, dr])
            sdir.append(_s5(hmat, nb, dr == 1, *prm))

        qa, qb, kk, vv = _qkv_prep(hmat, cos_t, sin_t)
        lam_init = 0.8 - 0.6 * math.exp(-0.3 * l)
        g_row = da_subln_g[l].reshape(1, DA_V_DIM)
        yc = jnp.zeros((t, DA_HEADS * DA_V_DIM), F32)
        yc = _attention(qa, qb, kk, vv, da_lam[l], g_row, lam_init, yc,
                        q0=0, nq=n_lat // tq, tq=tq, k0=0, nk=t // tk, tk=tk, name="attn_lat")
        yc = _attention(qa, qb, kk, vv, da_lam[l], g_row, lam_init, yc,
                        q0=nb, nq=1, tq=TM, k0=nb, nk=1, tk=TM, name="attn_ctx")

        x1, v = _merge(xs, modv, hmat, hdir[0], hdir[1], sdir[0], sdir[1], yc,
                       s5_d[l].reshape(1, -1), s5_w_glu[l].astype(BF16), s5_b_glu[l].reshape(1, -1),
                       w_proj[l].astype(BF16), b_gate[l], w_out[l].astype(BF16),
                       ln_g[l, 0].reshape(1, -1), ln_b[l, 0].reshape(1, -1), nb, dn_alpha)
        xs = _moe(v, x1, modv, l, w_router[l], b_router[l], w_gu, b_gu[l], w_down, b_down[l],
                  ln_g[l, 1].reshape(1, -1), ln_b[l, 1].reshape(1, -1), nb, dn_alpha)
    return xs[:n_lat][None]
```

```python
import functools
import math

import numpy as np
import jax
import jax.numpy as jnp
from jax import lax
from jax.experimental import pallas as pl
from jax.experimental.pallas import tpu as pltpu

F32 = jnp.float32
BF16 = jnp.bfloat16
HIGHEST = lax.Precision.HIGHEST

D_MODEL = 1024
GRID_W = 64
LRU_WIDTH = 512
LRU_BLOCKS = 8
CONV_W = 4
LRU_C = 8.0
S5_WIDTH = 512
S5_GROUP = 16
S5_GROUPS = S5_WIDTH // S5_GROUP
S5_STATE = 64
S5_N = S5_GROUPS * S5_STATE
DA_HEADS = 4
DA_HEAD_DIM = 64
DA_V_DIM = 128
DA_SCALE = DA_HEAD_DIM ** -0.5
ROPE_BASE = 10000.0
ROPE_FREQS = DA_HEAD_DIM // 4
N_BRANCH = 3
BW = 512
IN_COLS = 6 * BW + N_BRANCH * D_MODEL
N_EXPERTS = 32
TOP_K = 4
D_FF = D_MODEL
SWIGLU_ALPHA = 1.702
SWIGLU_LIMIT = 7.0
LN_EPS = 1e-5
RMS_EPS = 1e-6

SUBLANES = 8
LANES = 128
VMEM_LIMIT = 56 * 1024 * 1024

TM = 256
HALO = 16
SEG = TM // SUBLANES
S5_CHUNKS = S5_WIDTH // LANES
S5_SCAN_CHUNKS = 2
LOG2_E = math.log2(math.e)
VT_ROWS = DA_V_DIM + 16
MOE_BLK = 512
MOE_SUB = 256
TN_IN = 3072


def _cp(sem, vmem=VMEM_LIMIT):
    return pltpu.CompilerParams(dimension_semantics=sem, vmem_limit_bytes=vmem)


def _sigmoid(x):
    return 1.0 / (1.0 + jnp.exp(-x))


def _gelu_tanh(x):
    return 0.5 * x * (1.0 + jnp.tanh(math.sqrt(2.0 / math.pi) * (x + 0.044715 * (x * x * x))))


def _layer_norm(y, g, b):
    mu = jnp.mean(y, axis=-1, keepdims=True)
    yc = y - mu
    var = jnp.mean(yc * yc, axis=-1, keepdims=True)
    return yc * lax.rsqrt(var + LN_EPS) * g + b


def _mod_kernel(c_ref, w_ref, b_ref, o_ref):
    c = c_ref[...]
    s = c * _sigmoid(c)
    o_ref[0] = jnp.dot(s, w_ref[0], preferred_element_type=F32, precision=HIGHEST) + b_ref[0]


def _modulation(cc, w_mod, b_mod):
    depth = w_mod.shape[0]
    tn = 1536
    return pl.pallas_call(
        _mod_kernel,
        out_shape=jax.ShapeDtypeStruct((depth, SUBLANES, 6 * D_MODEL), F32),
        grid=(depth, 6 * D_MODEL // tn),
        in_specs=[
            pl.BlockSpec((SUBLANES, D_MODEL), lambda l, j: (0, 0)),
            pl.BlockSpec((1, D_MODEL, tn), lambda l, j: (l, 0, j)),
            pl.BlockSpec((1, 1, tn), lambda l, j: (l, 0, j)),
        ],
        out_specs=pl.BlockSpec((1, SUBLANES, tn), lambda l, j: (l, 0, j)),
        compiler_params=_cp(("arbitrary", "arbitrary")),
        name="modulation",
    )(cc, w_mod, b_mod.reshape(depth, 1, 6 * D_MODEL))


def _inproj_kernel(x_ref, mod_ref, w_ref, o_ref):
    md = mod_ref[0]
    u = x_ref[...] * (1.0 + md[1:2]) + md[0:1]
    o_ref[...] = jnp.dot(u.astype(BF16), w_ref[...], preferred_element_type=F32).astype(BF16)


def _inproj(x, modv, w_bf16, n_lat_blocks):
    t = x.shape[0]
    return pl.pallas_call(
        _inproj_kernel,
        out_shape=jax.ShapeDtypeStruct((t, IN_COLS), BF16),
        grid=(IN_COLS // TN_IN, t // TM),
        in_specs=[
            pl.BlockSpec((TM, D_MODEL), lambda j, i: (i, 0)),
            pl.BlockSpec((1, 6, D_MODEL), lambda j, i: (i // n_lat_blocks, 0, 0)),
            pl.BlockSpec((D_MODEL, TN_IN), lambda j, i: (0, j)),
        ],
        out_specs=pl.BlockSpec((TM, TN_IN), lambda j, i: (i, j)),
        compiler_params=_cp(("arbitrary", "arbitrary")),
        name="inproj",
    )(x, modv, w_bf16)


def _scan_block(s, nb, reverse):
    return (nb - s) if reverse else (s + nb) % (nb + 1)


def _lru_kernel(prev_ref, cur_ref, next_ref, cw_ref, cb_ref, wr_ref, br_ref, wi_ref, bi_ref,
                sp_ref, o_ref, a_s, b_s, h_s, *, nb, reverse):
    s = pl.program_id(0)
    blk = _scan_block(s, nb, reverse)

    @pl.when(s == 0)
    def _():
        h_s[...] = jnp.zeros_like(h_s)

    has_prev = jnp.logical_and(blk != 0, blk != nb)
    has_next = jnp.logical_and(blk != nb - 1, blk != nb)
    prev = jnp.where(has_prev, prev_ref[...].astype(F32), 0.0)
    nxt = jnp.where(has_next, next_ref[...].astype(F32), 0.0)
    ext = jnp.concatenate([prev, cur_ref[...].astype(F32), nxt], axis=0)
    cw = cw_ref[...]
    xc = cb_ref[...] + jnp.zeros((TM, LRU_WIDTH), F32)
    for k in range(CONV_W):
        off = HALO - CONV_W // 2 + k
        xc = xc + ext[off:off + TM] * cw[k:k + 1]
    xb = xc.astype(BF16)
    r = _sigmoid(jnp.dot(xb, wr_ref[...], preferred_element_type=F32) + br_ref[...])
    i = _sigmoid(jnp.dot(xb, wi_ref[...], preferred_element_type=F32) + bi_ref[...])
    a = jnp.exp(-LRU_C * r * sp_ref[...])
    a_s[...] = a
    b_s[...] = jnp.sqrt(1.0 - a * a) * (i * xc)

    def step(n, h):
        t = (TM - 1 - n) if reverse else n
        h = a_s[pl.ds(t, 1), :] * h + b_s[pl.ds(t, 1), :]
        o_ref[pl.ds(t, 1), :] = h
        return h

    h_s[...] = lax.fori_loop(0, TM, step, h_s[...], unroll=8)


def _lru(hmat, nb, reverse, conv_w, conv_b, wr, br, wi, bi, sp):
    t = hmat.shape[0]
    nblk = nb + 1
    rh = TM // HALO
    last_h = t // HALO - 1

    def bmap(s):
        return _scan_block(s, nb, reverse)

    vec = lambda: pl.BlockSpec((1, LRU_WIDTH), lambda s: (0, 0))
    mat = lambda: pl.BlockSpec((LRU_WIDTH, LRU_WIDTH), lambda s: (0, 0))
    return pl.pallas_call(
        functools.partial(_lru_kernel, nb=nb, reverse=reverse),
        out_shape=jax.ShapeDtypeStruct((t, LRU_WIDTH), F32),
        grid=(nblk,),
        in_specs=[
            pl.BlockSpec((HALO, LRU_WIDTH), lambda s: (jnp.maximum(bmap(s) * rh - 1, 0), 0)),
            pl.BlockSpec((TM, LRU_WIDTH), lambda s: (bmap(s), 0)),
            pl.BlockSpec((HALO, LRU_WIDTH), lambda s: (jnp.minimum((bmap(s) + 1) * rh, last_h), 0)),
            pl.BlockSpec((CONV_W, LRU_WIDTH), lambda s: (0, 0)),
            vec(), mat(), vec(), mat(), vec(), vec(),
        ],
        out_specs=pl.BlockSpec((TM, LRU_WIDTH), lambda s: (bmap(s), 0)),
        scratch_shapes=[pltpu.VMEM((TM, LRU_WIDTH), F32), pltpu.VMEM((TM, LRU_WIDTH), F32),
                        pltpu.VMEM((1, LRU_WIDTH), F32)],
        compiler_params=_cp(("arbitrary",)),
        name="lru_bwd" if reverse else "lru_fwd",
    )(hmat, hmat, hmat, conv_w, conv_b, wr, br, wi, bi, sp)


def _cmul(ar, ai, br, bi):
    return ar * br - ai * bi, ar * bi + ai * br


def _s5_kernel(u_ref, p_ref, pt_ref, bm_ref, cm_ref, a_ref, pw_ref, o_ref, x_s, st_s, c_s):
    s = pl.program_id(0)
    n = S5_N

    @pl.when(s == 0)
    def _():
        c_s[...] = jnp.zeros_like(c_s)

    up = jnp.dot(p_ref[...], u_ref[...], preferred_element_type=F32).astype(BF16)
    nc = n // S5_CHUNKS
    for c in range(S5_CHUNKS):
        xc = jnp.dot(up[:, c * LANES:(c + 1) * LANES], bm_ref[c], preferred_element_type=F32)
        x_s[:, c * nc:(c + 1) * nc] = xc[:, 0:nc]
        x_s[:, n + c * nc:n + (c + 1) * nc] = xc[:, nc:2 * nc]
    ns = n // S5_SCAN_CHUNKS
    for c in range(S5_SCAN_CHUNKS):
        re = slice(c * ns, (c + 1) * ns)
        im = slice(n + c * ns, n + (c + 1) * ns)
        ar = jnp.broadcast_to(a_ref[0:1, re], (SUBLANES, ns))
        ai = jnp.broadcast_to(a_ref[1:2, re], (SUBLANES, ns))

        def local(m, carry, re=re, im=im, ar=ar, ai=ai):
            hr, hi = carry
            row = pl.multiple_of(m * SUBLANES, SUBLANES)
            pr, pi = _cmul(ar, ai, hr, hi)
            hr = pr + x_s[pl.ds(row, SUBLANES), re]
            hi = pi + x_s[pl.ds(row, SUBLANES), im]
            x_s[pl.ds(row, SUBLANES), re] = hr
            x_s[pl.ds(row, SUBLANES), im] = hi
            return hr, hi

        z = jnp.zeros((SUBLANES, ns), F32)
        fr, fi = lax.fori_loop(0, SEG, local, (z, z), unroll=4)

        qr = pw_ref[TM - 1:TM, re]
        qi = pw_ref[TM - 1:TM, im]
        sr = c_s[0:1, re]
        si = c_s[0:1, im]
        for j in range(SUBLANES):
            st_s[j:j + 1, re] = sr
            st_s[j:j + 1, im] = si
            pr, pi = _cmul(qr, qi, sr, si)
            sr = pr + fr[j:j + 1]
            si = pi + fi[j:j + 1]
        c_s[0:1, re] = sr
        c_s[0:1, im] = si

        str_ = st_s[:, re]
        sti = st_s[:, im]

        def fix(m, _, re=re, im=im, str_=str_, sti=sti):
            row = pl.multiple_of(m * SUBLANES, SUBLANES)
            pr, pi = _cmul(pw_ref[pl.ds(row, SUBLANES), re], pw_ref[pl.ds(row, SUBLANES), im], str_, sti)
            x_s[pl.ds(row, SUBLANES), re] = x_s[pl.ds(row, SUBLANES), re] + pr
            x_s[pl.ds(row, SUBLANES), im] = x_s[pl.ds(row, SUBLANES), im] + pi
            return 0

        lax.fori_loop(0, SEG, fix, 0, unroll=2)

    yp = jnp.concatenate(
        [jnp.dot(x_s[:, c * nc:(c + 1) * nc].astype(BF16), cm_ref[0, c], preferred_element_type=F32)
         + jnp.dot(x_s[:, n + c * nc:n + (c + 1) * nc].astype(BF16), cm_ref[1, c], preferred_element_type=F32)
         for c in range(S5_CHUNKS)], axis=1)
    y_hi = yp.astype(BF16)
    y_lo = (yp - y_hi.astype(F32)).astype(BF16)
    pt = pt_ref[...]
    o_ref[...] = (jnp.dot(pt, y_hi, preferred_element_type=F32)
                  + jnp.dot(pt, y_lo, preferred_element_type=F32))


def _s5_perm(reverse):
    p = np.zeros((TM, TM), np.float32)
    for m in range(SEG):
        for j in range(SUBLANES):
            pos = j * SEG + m
            p[m * SUBLANES + j, (TM - 1 - pos) if reverse else pos] = 1.0
    return p


def _s5(hmat, nb, reverse, bm_t, cm_t, a_row, pw):
    t = hmat.shape[0]
    p = _s5_perm(reverse)
    n2 = 2 * S5_N

    def bmap(s):
        return _scan_block(s, nb, reverse)

    return pl.pallas_call(
        _s5_kernel,
        out_shape=jax.ShapeDtypeStruct((t, S5_WIDTH), F32),
        grid=(nb + 1,),
        in_specs=[
            pl.BlockSpec((TM, S5_WIDTH), lambda s: (bmap(s), 2)),
            pl.BlockSpec((TM, TM), lambda s: (0, 0)),
            pl.BlockSpec((TM, TM), lambda s: (0, 0)),
            pl.BlockSpec((S5_CHUNKS, LANES, n2 // S5_CHUNKS), lambda s: (0, 0, 0)),
            pl.BlockSpec((2, S5_CHUNKS, S5_N // S5_CHUNKS, LANES), lambda s: (0, 0, 0, 0)),
            pl.BlockSpec((2, S5_N), lambda s: (0, 0)),
            pl.BlockSpec((TM, n2), lambda s: (0, 0)),
        ],
        out_specs=pl.BlockSpec((TM, S5_WIDTH), lambda s: (bmap(s), 0)),
        scratch_shapes=[pltpu.VMEM((TM, n2), F32), pltpu.VMEM((SUBLANES, n2), F32),
                        pltpu.VMEM((1, n2), F32)],
        compiler_params=_cp(("arbitrary",)),
        name="s5_bwd" if reverse else "s5_fwd",
    )(hmat, jnp.asarray(p, BF16), jnp.asarray(p.T, BF16), bm_t, cm_t, a_row, pw)


def _s5_params(a_re, a_im, log_dt, b_re, b_im, c_re, c_im):
    g, p = a_re.shape
    dt = jnp.exp(log_dt)[:, None]
    lr, li = a_re, a_im
    ea = jnp.exp(lr * dt)
    ab_r, ab_i = ea * jnp.cos(li * dt), ea * jnp.sin(li * dt)
    den = lr * lr + li * li
    co_r = ((ab_r - 1.0) * lr + ab_i * li) / den
    co_i = (ab_i * lr - (ab_r - 1.0) * li) / den
    bb_r = co_r[..., None] * b_re - co_i[..., None] * b_im
    bb_i = co_r[..., None] * b_im + co_i[..., None] * b_re
    gc = g // S5_CHUNKS
    eye = jnp.eye(gc, dtype=F32)

    def chunked(w, spec):
        return jnp.einsum(spec, w.reshape((S5_CHUNKS, gc) + w.shape[1:]), eye)

    bm_r = chunked(bb_r, 'kgpc,gh->kgchp').reshape(S5_CHUNKS, gc * S5_GROUP, gc * p)
    bm_i = chunked(bb_i, 'kgpc,gh->kgchp').reshape(S5_CHUNKS, gc * S5_GROUP, gc * p)
    bm_t = jnp.concatenate([bm_r, bm_i], axis=2).astype(BF16)
    cm_r = chunked(c_re, 'kgcp,gh->kgphc').reshape(S5_CHUNKS, gc * p, gc * S5_GROUP)
    cm_i = chunked(-c_im, 'kgcp,gh->kgphc').reshape(S5_CHUNKS, gc * p, gc * S5_GROUP)
    cm_t = jnp.stack([cm_r, cm_i], axis=0).astype(BF16)
    a_row = jnp.stack([ab_r.reshape(-1), ab_i.reshape(-1)], axis=0)
    k = jnp.arange(1, SEG + 1, dtype=F32)[:, None, None]
    ek = jnp.exp(k * (lr * dt))
    pw = jnp.concatenate([(ek * jnp.cos(k * (li * dt))).reshape(SEG, -1),
                          (ek * jnp.sin(k * (li * dt))).reshape(SEG, -1)], axis=1)
    return bm_t, cm_t, a_row, jnp.repeat(pw, SUBLANES, axis=0)


def _qkv_kernel(q_ref, k_ref, v_ref, cos_ref, sin_ref, qa_ref, qb_ref, ko_ref, vo_ref):
    w = DA_HEADS * 2 * DA_HEAD_DIM
    cos = jnp.concatenate([cos_ref[...]] * (w // LANES), axis=1)
    sin = jnp.concatenate([sin_ref[...]] * (w // LANES), axis=1)
    lane = lax.broadcasted_iota(jnp.int32, (TM, w), 1)
    first = (lane % (2 * ROPE_FREQS)) < ROPE_FREQS

    def rope(x):
        swapped = jnp.where(first, pltpu.roll(x, w - ROPE_FREQS, 1), pltpu.roll(x, ROPE_FREQS, 1))
        return x * cos + swapped * sin

    q = rope(q_ref[...].astype(F32)) * (DA_SCALE * LOG2_E)
    in_a = (lane % (2 * DA_HEAD_DIM)) < DA_HEAD_DIM
    qa_ref[...] = jnp.where(in_a, q, 0.0).T.astype(BF16)
    qb_ref[...] = jnp.where(in_a, 0.0, q).T.astype(BF16)
    ko_ref[...] = rope(k_ref[...].astype(F32)).astype(BF16)
    vt = v_ref[...].astype(F32).T
    ones = jnp.ones((VT_ROWS - DA_V_DIM, TM), F32)
    vo_ref[...] = jnp.concatenate(
        [blk for h in range(DA_HEADS) for blk in (vt[h * DA_V_DIM:(h + 1) * DA_V_DIM], ones)],
        axis=0).astype(BF16)


def _qkv_prep(hmat, cos_t, sin_t):
    t = hmat.shape[0]
    w = DA_HEADS * 2 * DA_HEAD_DIM
    col = lambda c: pl.BlockSpec((TM, w), lambda i: (i, c))
    tab = lambda: pl.BlockSpec((TM, LANES), lambda i: (i, 0))
    out_t = lambda rows: pl.BlockSpec((rows, TM), lambda i: (0, i))
    sds_t = jax.ShapeDtypeStruct((w, t), BF16)
    return pl.pallas_call(
        _qkv_kernel,
        out_shape=(sds_t, sds_t, jax.ShapeDtypeStruct((t, w), BF16),
                   jax.ShapeDtypeStruct((DA_HEADS * VT_ROWS, t), BF16)),
        grid=(t // TM,),
        in_specs=[col(3), col(4), col(5), tab(), tab()],
        out_specs=(out_t(w), out_t(w), pl.BlockSpec((TM, w), lambda i: (i, 0)), out_t(DA_HEADS * VT_ROWS)),
        compiler_params=_cp(("arbitrary",)),
        name="qkv_prep",
    )(hmat, hmat, hmat, cos_t, sin_t)


def _rope_tables(n_lat, n_ctx):
    pos = jnp.arange(n_lat)
    freqs = ROPE_BASE ** (-jnp.arange(ROPE_FREQS, dtype=F32) / ROPE_FREQS)
    ang_r = (pos // GRID_W).astype(F32)[:, None] * freqs
    ang_c = (pos % GRID_W).astype(F32)[:, None] * freqs
    cr, sr, cc, sc = jnp.cos(ang_r), jnp.sin(ang_r), jnp.cos(ang_c), jnp.sin(ang_c)
    cos64 = jnp.concatenate([cr, cr, cc, cc], axis=1)
    sin64 = jnp.concatenate([-sr, sr, -sc, sc], axis=1)
    cos_t = jnp.concatenate([cos64, cos64], axis=1)
    sin_t = jnp.concatenate([sin64, sin64], axis=1)
    cos_t = jnp.concatenate([cos_t, jnp.ones((n_ctx, LANES), F32)], axis=0)
    sin_t = jnp.concatenate([sin_t, jnp.zeros((n_ctx, LANES), F32)], axis=0)
    return cos_t, sin_t


def _attn_kernel(lam_ref, g_ref, qa_ref, qb_ref, k_ref, v_ref, prev_ref, o_ref, m_s, acc_s, a_s, s_s,
                 *, lam_init, nk):
    del prev_ref
    g = pl.program_id(1)
    kj = g % nk

    @pl.when(g == 0)
    def _():
        s_s[...] = jnp.zeros_like(s_s)
        acc_s[...] = jnp.zeros_like(acc_s)
        a_s[...] = jnp.zeros_like(a_s)
        m_s[...] = jnp.zeros_like(m_s)

    def step(cur):
        k = k_ref[...]
        vt = v_ref[...]
        for i, q_ref in enumerate((qa_ref, qb_ref)):
            m_old = m_s[i]
            p = jnp.exp2(s_s[1 - cur, i] - m_old[0:1, :]).astype(BF16)
            acc_s[i] = a_s[i][0:1, :] * acc_s[i] + jnp.dot(vt, p, preferred_element_type=F32)
            s = jnp.dot(k, q_ref[...], preferred_element_type=F32)
            s_s[cur, i] = s
            m_base = jnp.where(kj == 0, -jnp.inf, m_old)
            m_new = jnp.maximum(m_base, jnp.max(s, axis=0, keepdims=True))
            a_s[i] = jnp.exp2(m_base - m_new)
            m_s[i] = m_new

    def finish():
        lv = lam_ref[...]
        lam = (jnp.exp(jnp.sum(lv[0:1] * lv[1:2], axis=1, keepdims=True))
               - jnp.exp(jnp.sum(lv[2:3] * lv[3:4], axis=1, keepdims=True)) + lam_init)
        a0, a1 = acc_s[0], acc_s[1]
        o = (a0[0:DA_V_DIM] / a0[DA_V_DIM:DA_V_DIM + 1]
             - lam * (a1[0:DA_V_DIM] / a1[DA_V_DIM:DA_V_DIM + 1]))
        ms = jnp.mean(o * o, axis=0, keepdims=True)
        o_ref[...] = (o * lax.rsqrt(ms + RMS_EPS)).T * g_ref[...] * (1.0 - lam_init)

    for parity in range(2):
        pl.when(g % 2 == parity)(functools.partial(step, parity))
    pl.when(jnp.logical_and(kj == 0, g > 0))(finish)


def _attention(qa_t, qb_t, k, v_t, lam_vec, subln_g, lam_init, prev, *, q0, nq, tq, k0, nk, tk, name):
    t = k.shape[0]
    n_pairs = nq * nk

    def qmap(h, g):
        return (h, q0 + jnp.minimum(g // nk, nq - 1))

    return pl.pallas_call(
        functools.partial(_attn_kernel, lam_init=lam_init, nk=nk),
        out_shape=jax.ShapeDtypeStruct((t, DA_HEADS * DA_V_DIM), F32),
        grid=(DA_HEADS, n_pairs + 1),
        in_specs=[
            pl.BlockSpec((4, DA_HEAD_DIM), lambda h, g: (0, 0)),
            pl.BlockSpec((1, DA_V_DIM), lambda h, g: (0, 0)),
            pl.BlockSpec((LANES, tq), qmap),
            pl.BlockSpec((LANES, tq), qmap),
            pl.BlockSpec((tk, LANES), lambda h, g: (k0 + g % nk, h)),
            pl.BlockSpec((VT_ROWS, tk), lambda h, g: (h, k0 + (g + nk - 1) % nk)),
            pl.BlockSpec(memory_space=pl.ANY),
        ],
        out_specs=pl.BlockSpec((tq, LANES), lambda h, g: (q0 + jnp.maximum(g - 1, 0) // nk, h)),
        scratch_shapes=[pltpu.VMEM((2, SUBLANES, tq), F32), pltpu.VMEM((2, VT_ROWS, tq), F32),
                        pltpu.VMEM((2, SUBLANES, tq), F32), pltpu.VMEM((2, 2, tk, tq), F32)],
        input_output_aliases={6: 0},
        compiler_params=_cp(("arbitrary", "arbitrary")),
        name=name,
    )(lam_vec, subln_g, qa_t, qb_t, k, v_t, prev)


def _merge_kernel(x_ref, mod_ref, gt_ref, ga_ref, ub_ref, hf_ref, hb_ref, sf_ref, sb_ref, yc_ref,
                  sd_ref, wglu_ref, bglu_ref, wp_ref, bg_ref, wo_ref, lng_ref, lnb_ref,
                  x1_ref, v_ref, *, dn_alpha):
    md = mod_ref[0]
    ya = (hf_ref[...] + hb_ref[...]) * _gelu_tanh(ga_ref[...].astype(F32))
    y = sf_ref[...] + sb_ref[...] + sd_ref[...] * ub_ref[...].astype(F32)
    tg = jnp.dot(_gelu_tanh(y).astype(BF16), wglu_ref[...], preferred_element_type=F32) + bglu_ref[...]
    yb = tg[:, 0:S5_WIDTH] * _sigmoid(tg[:, S5_WIDTH:2 * S5_WIDTH])
    yc = yc_ref[...]
    bg = bg_ref[...]
    z = jnp.zeros((TM, D_MODEL), F32)
    for n, br in enumerate((ya, yb, yc)):
        gate = _sigmoid(gt_ref[:, n * D_MODEL:(n + 1) * D_MODEL].astype(F32) + bg[n:n + 1])
        z = z + gate * jnp.dot(br.astype(BF16), wp_ref[n], preferred_element_type=F32)
    m = jnp.dot(z.astype(BF16), wo_ref[...], preferred_element_type=F32)
    x1 = _layer_norm(dn_alpha * x_ref[...] + md[2:3] * m, lng_ref[...], lnb_ref[...])
    x1_ref[...] = x1
    v_ref[...] = x1 * (1.0 + md[4:5]) + md[3:4]


def _merge(x, modv, hmat, hf, hb, sf, sb, yc, s5_d, w_glu, b_glu, w_proj, b_gate, w_out, ln_g, ln_b,
           n_lat_blocks, dn_alpha):
    t = x.shape[0]
    row = lambda w, c=0: pl.BlockSpec((TM, w), lambda i: (i, c))
    const = lambda *shape: pl.BlockSpec(shape, lambda i: (0,) * len(shape))
    sds = jax.ShapeDtypeStruct((t, D_MODEL), F32)
    return pl.pallas_call(
        functools.partial(_merge_kernel, dn_alpha=dn_alpha),
        out_shape=(sds, sds),
        grid=(t // TM,),
        in_specs=[
            row(D_MODEL),
            pl.BlockSpec((1, 6, D_MODEL), lambda i: (i // n_lat_blocks, 0, 0)),
            row(N_BRANCH * D_MODEL, 1), row(BW, 1), row(BW, 2),
            row(BW), row(BW), row(BW), row(BW), row(BW),
            const(1, S5_WIDTH), const(S5_WIDTH, 2 * S5_WIDTH), const(1, 2 * S5_WIDTH),
            const(N_BRANCH, BW, D_MODEL), const(N_BRANCH, D_MODEL), const(D_MODEL, D_MODEL),
            const(1, D_MODEL), const(1, D_MODEL),
        ],
        out_specs=(row(D_MODEL), row(D_MODEL)),
        compiler_params=_cp(("arbitrary",)),
        name="merge",
    )(x, modv, hmat, hmat, hmat, hf, hb, sf, sb, yc, s5_d, w_glu, b_glu, w_proj, b_gate, w_out,
      ln_g, ln_b)


def _router_kernel(v_ref, w_ref, b_ref, tri_ref, ti_ref, gw_ref, rk_ref, cnt_ref, c_s):
    s = pl.program_id(0)

    @pl.when(s == 0)
    def _():
        c_s[...] = jnp.zeros_like(c_s)

    logits = jnp.dot(v_ref[...], w_ref[...], preferred_element_type=F32, precision=HIGHEST) + b_ref[...]
    lane = lax.broadcasted_iota(jnp.int32, (TM, N_EXPERTS), 1).astype(F32)
    work = logits
    vals, idxs = [], []
    hot = jnp.zeros((TM, N_EXPERTS), F32)
    for _ in range(TOP_K):
        mx = jnp.max(work, axis=1, keepdims=True)
        ix = jnp.min(jnp.where(work == mx, lane, float(N_EXPERTS)), axis=1, keepdims=True)
        sel = lane == ix
        hot = jnp.where(sel, 1.0, hot)
        work = jnp.where(sel, -jnp.inf, work)
        vals.append(mx)
        idxs.append(ix)
    es = [jnp.exp(vv - vals[0]) for vv in vals]
    den = es[0] + es[1] + es[2] + es[3]
    before = jnp.dot(tri_ref[...], hot.astype(BF16), preferred_element_type=F32) + c_s[0:1, :]
    col = lax.broadcasted_iota(jnp.int32, (TM, TOP_K), 1)
    ti = jnp.zeros((TM, TOP_K), F32)
    gw = jnp.zeros((TM, TOP_K), F32)
    rk = jnp.zeros((TM, TOP_K), F32)
    for kk in range(TOP_K):
        rank = jnp.sum(jnp.where(lane == idxs[kk], before, 0.0), axis=1, keepdims=True)
        ti = jnp.where(col == kk, idxs[kk], ti)
        gw = jnp.where(col == kk, es[kk] / den, gw)
        rk = jnp.where(col == kk, rank, rk)
    ti_ref[...] = ti.astype(jnp.int32)
    gw_ref[...] = gw
    rk_ref[...] = rk.astype(jnp.int32)
    tot = c_s[0:1, :] + jnp.sum(hot, axis=0, keepdims=True)
    c_s[...] = jnp.broadcast_to(tot, c_s.shape)
    cnt_ref[...] = jnp.broadcast_to(tot, cnt_ref.shape).astype(jnp.int32)


def _router(v, w_router, b_router):
    t = v.shape[0]
    tri = jnp.asarray(np.tril(np.ones((TM, TM), np.float32), -1), BF16)
    row4 = lambda: pl.BlockSpec((TM, TOP_K), lambda i: (i, 0))
    return pl.pallas_call(
        _router_kernel,
        out_shape=(jax.ShapeDtypeStruct((t, TOP_K), jnp.int32),
                   jax.ShapeDtypeStruct((t, TOP_K), F32),
                   jax.ShapeDtypeStruct((t, TOP_K), jnp.int32),
                   jax.ShapeDtypeStruct((SUBLANES, N_EXPERTS), jnp.int32)),
        grid=(t // TM,),
        in_specs=[
            pl.BlockSpec((TM, D_MODEL), lambda i: (i, 0)),
            pl.BlockSpec((D_MODEL, N_EXPERTS), lambda i: (0, 0)),
            pl.BlockSpec((1, N_EXPERTS), lambda i: (0, 0)),
            pl.BlockSpec((TM, TM), lambda i: (0, 0)),
        ],
        out_specs=(row4(), row4(), row4(), pl.BlockSpec((SUBLANES, N_EXPERTS), lambda i: (0, 0))),
        scratch_shapes=[pltpu.VMEM((SUBLANES, N_EXPERTS), F32)],
        compiler_params=_cp(("arbitrary",)),
        name="router",
    )(v, w_router, b_router, tri)


def _dispatch_kernel(dest_ref, v_ref, xs_in, xs_hbm, sem):
    del xs_in

    def row_copy(r, kk):
        return pltpu.make_async_copy(v_ref.at[pl.ds(r, 1)],
                                     xs_hbm.at[pl.ds(dest_ref[r * TOP_K + kk], 1)], sem)

    def start(r, _):
        for kk in range(TOP_K):
            row_copy(r, kk).start()
        return 0

    def wait(r, _):
        for kk in range(TOP_K):
            row_copy(r, kk).wait()
        return 0

    lax.fori_loop(0, TM, start, 0, unroll=8)
    lax.fori_loop(0, TM, wait, 0, unroll=8)


def _dispatch(dest_flat, v, n_slots):
    t = v.shape[0]
    xs0 = jnp.zeros((n_slots, D_MODEL), F32)
    return pl.pallas_call(
        _dispatch_kernel,
        out_shape=jax.ShapeDtypeStruct((n_slots, D_MODEL), F32),
        grid=(t // TM,),
        in_specs=[
            pl.BlockSpec((TM * TOP_K,), lambda i: (i,), memory_space=pltpu.SMEM),
            pl.BlockSpec((TM, D_MODEL), lambda i: (i, 0)),
            pl.BlockSpec(memory_space=pl.ANY),
        ],
        out_specs=pl.BlockSpec(memory_space=pl.ANY),
        scratch_shapes=[pltpu.SemaphoreType.DMA],
        input_output_aliases={2: 0},
        compiler_params=_cp(("arbitrary",)),
        name="moe_dispatch",
    )(dest_flat, v, xs0)


def _expert_kernel(be_ref, nu_ref, x_ref, wgu_ref, bgu_ref, wd_ref, bd_ref, o_ref, wgu_s, wd_s):
    b = pl.program_id(0)
    new_expert = jnp.logical_or(b == 0, be_ref[b] != be_ref[jnp.maximum(b - 1, 0)])

    @pl.when(jnp.logical_and(new_expert, b < nu_ref[0]))
    def _():
        wgu_s[...] = wgu_ref[0, 0].astype(BF16)
        wd_s[...] = wd_ref[0, 0].astype(BF16)

    @pl.when(b < nu_ref[0])
    def _():
        for r in range(0, MOE_BLK, MOE_SUB):
            rows = slice(r, r + MOE_SUB)
            h = jnp.dot(x_ref[rows, :].astype(BF16), wgu_s[...], preferred_element_type=F32) + bgu_ref[0]
            hg = jnp.minimum(h[:, 0:D_FF], SWIGLU_LIMIT)
            hl = jnp.clip(h[:, D_FF:2 * D_FF], -SWIGLU_LIMIT, SWIGLU_LIMIT)
            act = hg * _sigmoid(SWIGLU_ALPHA * hg) * (hl + 1.0)
            o_ref[rows, :] = jnp.dot(act.astype(BF16), wd_s[...], preferred_element_type=F32) + bd_ref[0]

    @pl.when(pl.program_id(0) >= nu_ref[0])
    def _():
        o_ref[...] = jnp.zeros_like(o_ref)


def _experts(blk_e, n_used, xs, layer, w_gu, b_gu, w_down, b_down):
    n_slots = xs.shape[0]
    nblk = n_slots // MOE_BLK
    grid_spec = pltpu.PrefetchScalarGridSpec(
        num_scalar_prefetch=2,
        grid=(nblk,),
        in_specs=[
            pl.BlockSpec((MOE_BLK, D_MODEL), lambda b, be, nu: (jnp.minimum(b, nu[0] - 1), 0)),
            pl.BlockSpec((1, 1, D_MODEL, 2 * D_FF), lambda b, be, nu: (layer, be[b], 0, 0)),
            pl.BlockSpec((1, 1, 2 * D_FF), lambda b, be, nu: (be[b], 0, 0)),
            pl.BlockSpec((1, 1, D_FF, D_MODEL), lambda b, be, nu: (layer, be[b], 0, 0)),
            pl.BlockSpec((1, 1, D_MODEL), lambda b, be, nu: (be[b], 0, 0)),
        ],
        out_specs=pl.BlockSpec((MOE_BLK, D_MODEL), lambda b, be, nu: (b, 0)),
        scratch_shapes=[pltpu.VMEM((D_MODEL, 2 * D_FF), BF16), pltpu.VMEM((D_FF, D_MODEL), BF16)],
    )
    return pl.pallas_call(
        _expert_kernel,
        out_shape=jax.ShapeDtypeStruct((n_slots, D_MODEL), F32),
        grid_spec=grid_spec,
        compiler_params=_cp(("arbitrary",)),
        name="moe_experts",
    )(blk_e, n_used, xs, w_gu, b_gu.reshape(N_EXPERTS, 1, 2 * D_FF), w_down,
      b_down.reshape(N_EXPERTS, 1, D_MODEL))


def _combine_kernel(dest_ref, ys_hbm, gw_ref, x_ref, mod_ref, lng_ref, lnb_ref, o_ref, g_s, sem,
                    *, dn_alpha):
    def row_copy(r, kk):
        return pltpu.make_async_copy(ys_hbm.at[pl.ds(dest_ref[r * TOP_K + kk], 1)],
                                     g_s.at[kk, pl.ds(r, 1)], sem)

    def start(r, _):
        for kk in range(TOP_K):
            row_copy(r, kk).start()
        return 0

    def wait(r, _):
        for kk in range(TOP_K):
            row_copy(r, kk).wait()
        return 0

    lax.fori_loop(0, TM, start, 0, unroll=8)
    lax.fori_loop(0, TM, wait, 0, unroll=8)
    gw = gw_ref[...]
    f = jnp.zeros((TM, D_MODEL), F32)
    for kk in range(TOP_K):
        f = f + gw[:, kk:kk + 1] * g_s[kk]
    md = mod_ref[0]
    o_ref[...] = _layer_norm(dn_alpha * x_ref[...] + md[5:6] * f, lng_ref[...], lnb_ref[...])


def _combine(dest_flat, ys, gate_w, x1, modv, ln_g, ln_b, n_lat_blocks, dn_alpha):
    t = x1.shape[0]
    return pl.pallas_call(
        functools.partial(_combine_kernel, dn_alpha=dn_alpha),
        out_shape=jax.ShapeDtypeStruct((t, D_MODEL), F32),
        grid=(t // TM,),
        in_specs=[
            pl.BlockSpec((TM * TOP_K,), lambda i: (i,), memory_space=pltpu.SMEM),
            pl.BlockSpec(memory_space=pl.ANY),
            pl.BlockSpec((TM, TOP_K), lambda i: (i, 0)),
            pl.BlockSpec((TM, D_MODEL), lambda i: (i, 0)),
            pl.BlockSpec((1, 6, D_MODEL), lambda i: (i // n_lat_blocks, 0, 0)),
            pl.BlockSpec((1, D_MODEL), lambda i: (0, 0)),
            pl.BlockSpec((1, D_MODEL), lambda i: (0, 0)),
        ],
        out_specs=pl.BlockSpec((TM, D_MODEL), lambda i: (i, 0)),
        scratch_shapes=[pltpu.VMEM((TOP_K, TM, D_MODEL), F32), pltpu.SemaphoreType.DMA],
        compiler_params=_cp(("arbitrary",)),
        name="moe_combine",
    )(dest_flat, ys, gate_w, x1, modv, ln_g, ln_b)


def _moe(v, x1, modv, layer, w_router, b_router, w_gu, b_gu, w_down, b_down, ln_g, ln_b, n_lat_blocks,
         dn_alpha):
    t = v.shape[0]
    top_i, gate_w, rank, counts = _router(v, w_router, b_router.reshape(1, N_EXPERTS))
    counts = counts[0]
    padded = (counts + MOE_BLK - 1) // MOE_BLK * MOE_BLK
    pend = jnp.cumsum(padded)
    pstart = pend - padded
    dest = (pstart[top_i] + rank).reshape(-1).astype(jnp.int32)
    n_slots = t * TOP_K + N_EXPERTS * MOE_BLK
    nblk = n_slots // MOE_BLK
    starts = jnp.arange(nblk, dtype=jnp.int32) * MOE_BLK
    blk_e = jnp.minimum(jnp.sum((pend[None, :] <= starts[:, None]).astype(jnp.int32), axis=1),
                        N_EXPERTS - 1).astype(jnp.int32)
    n_used = (pend[-1:] // MOE_BLK).astype(jnp.int32)
    xs = _dispatch(dest, v, n_slots)
    ys = _experts(blk_e, n_used, xs, layer, w_gu, b_gu, w_down, b_down)
    return _combine(dest, ys, gate_w, x1, modv, ln_g, ln_b, n_lat_blocks, dn_alpha)


def _block_diag(w):
    nbk, bs, _ = w.shape
    eye = jnp.eye(nbk, dtype=w.dtype)
    return jnp.einsum('hij,hg->higj', w, eye).reshape(nbk * bs, nbk * bs)


def _query_block(n_lat):
    for tq in (1024, 512):
        if n_lat % tq == 0:
            return tq
    return TM


def _key_block(t):
    for tk in (1280, 640, 256):
        if t % tk == 0:
            return tk
    raise ValueError("token count must be a multiple of 256")


def kernel(x, c, ctx, c_ctx, w_mod, b_mod, w_in, conv_w, conv_b, lru_wr, lru_br, lru_wi, lru_bi, lru_lam, s5_a_re, s5_a_im, s5_log_dt, s5_b_re, s5_b_im, s5_c_re, s5_c_im, s5_d, s5_w_glu, s5_b_glu, da_lam, da_subln_g, w_proj, b_gate, w_out, ln_g, ln_b, w_router, b_router, w_gu, b_gu, w_down, b_down):
    bsz, n_lat, d = x.shape
    n_ctx = ctx.shape[1]
    depth = w_mod.shape[0]
    assert bsz == 1 and d == D_MODEL and n_ctx == TM and n_lat % TM == 0 and n_lat % GRID_W == 0
    nb = n_lat // TM
    t = n_lat + n_ctx
    dn_alpha = (2 * depth) ** 0.25

    cc = jnp.zeros((SUBLANES, D_MODEL), F32).at[0].set(c[0]).at[1].set(c_ctx)
    mod = _modulation(cc, w_mod, b_mod)
    cos_t, sin_t = _rope_tables(n_lat, n_ctx)
    tq = _query_block(n_lat)
    tk = _key_block(t)

    xs = jnp.concatenate([x[0], ctx[0]], axis=0)
    for l in range(depth):
        modv = mod[l, 0:2].reshape(2, 6, D_MODEL)
        hmat = _inproj(xs, modv, w_in[l].astype(BF16), nb)

        sp = jax.nn.softplus(-lru_lam[l])
        hdir, sdir = [], []
        for dr in range(2):
            hdir.append(_lru(hmat, nb, dr == 1, conv_w[l], conv_b[l].reshape(1, -1),
                             _block_diag(lru_wr[l, dr]).astype(BF16), lru_br[l, dr].reshape(1, -1),
                             _block_diag(lru_wi[l, dr]).astype(BF16), lru_bi[l, dr].reshape(1, -1),
                             sp[dr].reshape(1, -1)))
            prm = _s5_params(s5_a_re[l, dr], s5_a_im[l, dr], s5_log_dt[l, dr], s5_b_re[l, dr],
                             s5_b_im[l, dr], s5_c_re[l, dr], s5_c_im[l, dr])
            sdir.append(_s5(hmat, nb, dr == 1, *prm))

        qa, qb, kk, vv = _qkv_prep(hmat, cos_t, sin_t)
        lam_init = 0.8 - 0.6 * math.exp(-0.3 * l)
        g_row = da_subln_g[l].reshape(1, DA_V_DIM)
        yc = jnp.zeros((t, DA_HEADS * DA_V_DIM), F32)
        yc = _attention(qa, qb, kk, vv, da_lam[l], g_row, lam_init, yc,
                        q0=0, nq=n_lat // tq, tq=tq, k0=0, nk=t // tk, tk=tk, name="attn_lat")
        yc = _attention(qa, qb, kk, vv, da_lam[l], g_row, lam_init, yc,
                        q0=nb, nq=1, tq=TM, k0=nb, nk=1, tk=TM, name="attn_ctx")

        x1, v = _merge(xs, modv, hmat, hdir[0], hdir[1], sdir[0], sdir[1], yc,
                       s5_d[l].reshape(1, -1), s5_w_glu[l].astype(BF16), s5_b_glu[l].reshape(1, -1),
                       w_proj[l].astype(BF16), b_gate[l], w_out[l].astype(BF16),
                       ln_g[l, 0].reshape(1, -1), ln_b[l, 0].reshape(1, -1), nb, dn_alpha)
        xs = _moe(v, x1, modv, l, w_router[l], b_router[l], w_gu, b_gu[l], w_down, b_down[l],
                  ln_g[l, 1].reshape(1, -1), ln_b[l, 1].reshape(1, -1), nb, dn_alpha)
    return xs[:n_lat][None]
```

```python
import functools
import math

import numpy as np
import jax
import jax.numpy as jnp
from jax import lax
from jax.experimental import pallas as pl
from jax.experimental.pallas import tpu as pltpu

F32 = jnp.float32
BF16 = jnp.bfloat16
HIGHEST = lax.Precision.HIGHEST

D_MODEL = 1024
GRID_W = 64
LRU_WIDTH = 512
LRU_BLOCKS = 8
CONV_W = 4
LRU_C = 8.0
S5_WIDTH = 512
S5_GROUP = 16
S5_GROUPS = S5_WIDTH // S5_GROUP
S5_STATE = 64
S5_N = S5_GROUPS * S5_STATE
DA_HEADS = 4
DA_HEAD_DIM = 64
DA_V_DIM = 128
DA_SCALE = DA_HEAD_DIM ** -0.5
ROPE_BASE = 10000.0
ROPE_FREQS = DA_HEAD_DIM // 4
N_BRANCH = 3
BW = 512
IN_COLS = 6 * BW + N_BRANCH * D_MODEL
N_EXPERTS = 32
TOP_K = 4
D_FF = D_MODEL
SWIGLU_ALPHA = 1.702
SWIGLU_LIMIT = 7.0
LN_EPS = 1e-5
RMS_EPS = 1e-6

SUBLANES = 8
LANES = 128
VMEM_LIMIT = 56 * 1024 * 1024

TM = 256
HALO = 16
SEG = TM // SUBLANES
S5_CHUNKS = S5_WIDTH // LANES
S5_SCAN_CHUNKS = 2
LOG2_E = math.log2(math.e)
MOE_BLK = 512
MOE_SUB = 256
TN_IN = 3072


def _cp(sem, vmem=VMEM_LIMIT):
    return pltpu.CompilerParams(dimension_semantics=sem, vmem_limit_bytes=vmem)


def _sigmoid(x):
    return 1.0 / (1.0 + jnp.exp(-x))


def _gelu_tanh(x):
    return 0.5 * x * (1.0 + jnp.tanh(math.sqrt(2.0 / math.pi) * (x + 0.044715 * (x * x * x))))


def _layer_norm(y, g, b):
    mu = jnp.mean(y, axis=-1, keepdims=True)
    yc = y - mu
    var = jnp.mean(yc * yc, axis=-1, keepdims=True)
    return yc * lax.rsqrt(var + LN_EPS) * g + b


def _mod_kernel(c_ref, w_ref, b_ref, o_ref):
    c = c_ref[...]
    s = c * _sigmoid(c)
    o_ref[0] = jnp.dot(s, w_ref[0], preferred_element_type=F32, precision=HIGHEST) + b_ref[0]


def _modulation(cc, w_mod, b_mod):
    depth = w_mod.shape[0]
    tn = 1536
    return pl.pallas_call(
        _mod_kernel,
        out_shape=jax.ShapeDtypeStruct((depth, SUBLANES, 6 * D_MODEL), F32),
        grid=(depth, 6 * D_MODEL // tn),
        in_specs=[
            pl.BlockSpec((SUBLANES, D_MODEL), lambda l, j: (0, 0)),
            pl.BlockSpec((1, D_MODEL, tn), lambda l, j: (l, 0, j)),
            pl.BlockSpec((1, 1, tn), lambda l, j: (l, 0, j)),
        ],
        out_specs=pl.BlockSpec((1, SUBLANES, tn), lambda l, j: (l, 0, j)),
        compiler_params=_cp(("arbitrary", "arbitrary")),
        name="modulation",
    )(cc, w_mod, b_mod.reshape(depth, 1, 6 * D_MODEL))


def _inproj_kernel(x_ref, mod_ref, w_ref, o_ref):
    md = mod_ref[0]
    u = x_ref[...] * (1.0 + md[1:2]) + md[0:1]
    o_ref[...] = jnp.dot(u.astype(BF16), w_ref[...], preferred_element_type=F32).astype(BF16)


def _inproj(x, modv, w_bf16, n_lat_blocks):
    t = x.shape[0]
    return pl.pallas_call(
        _inproj_kernel,
        out_shape=jax.ShapeDtypeStruct((t, IN_COLS), BF16),
        grid=(IN_COLS // TN_IN, t // TM),
        in_specs=[
            pl.BlockSpec((TM, D_MODEL), lambda j, i: (i, 0)),
            pl.BlockSpec((1, 6, D_MODEL), lambda j, i: (i // n_lat_blocks, 0, 0)),
            pl.BlockSpec((D_MODEL, TN_IN), lambda j, i: (0, j)),
        ],
        out_specs=pl.BlockSpec((TM, TN_IN), lambda j, i: (i, j)),
        compiler_params=_cp(("arbitrary", "arbitrary")),
        name="inproj",
    )(x, modv, w_bf16)


def _scan_block(s, nb, reverse):
    return (nb - s) if reverse else (s + nb) % (nb + 1)


def _lru_kernel(pf_ref, cf_ref, nf_ref, pb_ref, cb_ref, nb_ref, cw_ref, cbias_ref, wr_ref, br_ref, wi_ref,
                bi_ref, sp_ref, of_ref, ob_ref, a_s, b_s, h_s, *, nb):
    s = pl.program_id(0)

    @pl.when(s == 0)
    def _():
        h_s[...] = jnp.zeros_like(h_s)

    cw = cw_ref[...]
    for d, (prev_ref, cur_ref, next_ref) in enumerate(((pf_ref, cf_ref, nf_ref), (pb_ref, cb_ref, nb_ref))):
        blk = _scan_block(s, nb, d == 1)
        has_prev = jnp.logical_and(blk != 0, blk != nb)
        has_next = jnp.logical_and(blk != nb - 1, blk != nb)
        prev = jnp.where(has_prev, prev_ref[...].astype(F32), 0.0)
        nxt = jnp.where(has_next, next_ref[...].astype(F32), 0.0)
        ext = jnp.concatenate([prev, cur_ref[...].astype(F32), nxt], axis=0)
        xc = cbias_ref[...] + jnp.zeros((TM, LRU_WIDTH), F32)
        for k in range(CONV_W):
            off = HALO - CONV_W // 2 + k
            xc = xc + ext[off:off + TM] * cw[k:k + 1]
        xb = xc.astype(BF16)
        r = _sigmoid(jnp.dot(xb, wr_ref[d], preferred_element_type=F32) + br_ref[d])
        i = _sigmoid(jnp.dot(xb, wi_ref[d], preferred_element_type=F32) + bi_ref[d])
        a = jnp.exp(-LRU_C * r * sp_ref[d])
        a_s[d] = a
        b_s[d] = jnp.sqrt(1.0 - a * a) * (i * xc)

    def step(n, carry):
        hf, hb = carry
        tb = TM - 1 - n
        hf = a_s[0, pl.ds(n, 1), :] * hf + b_s[0, pl.ds(n, 1), :]
        of_ref[pl.ds(n, 1), :] = hf
        hb = a_s[1, pl.ds(tb, 1), :] * hb + b_s[1, pl.ds(tb, 1), :]
        ob_ref[pl.ds(tb, 1), :] = hb
        return hf, hb

    hf, hb = lax.fori_loop(0, TM, step, (h_s[0:1, :], h_s[1:2, :]), unroll=8)
    h_s[0:1, :] = hf
    h_s[1:2, :] = hb


def _lru(hmat, nb, conv_w, conv_b, wr, br, wi, bi, sp):
    t = hmat.shape[0]
    rh = TM // HALO
    last_h = t // HALO - 1

    def specs(reverse):
        def bmap(s):
            return _scan_block(s, nb, reverse)
        return [
            pl.BlockSpec((HALO, LRU_WIDTH), lambda s: (jnp.maximum(bmap(s) * rh - 1, 0), 0)),
            pl.BlockSpec((TM, LRU_WIDTH), lambda s: (bmap(s), 0)),
            pl.BlockSpec((HALO, LRU_WIDTH), lambda s: (jnp.minimum((bmap(s) + 1) * rh, last_h), 0)),
        ], pl.BlockSpec((TM, LRU_WIDTH), lambda s: (bmap(s), 0))

    in_f, out_f = specs(False)
    in_b, out_b = specs(True)
    vec = lambda: pl.BlockSpec((2, 1, LRU_WIDTH), lambda s: (0, 0, 0))
    mat = lambda: pl.BlockSpec((2, LRU_WIDTH, LRU_WIDTH), lambda s: (0, 0, 0))
    sds = jax.ShapeDtypeStruct((t, LRU_WIDTH), F32)
    return pl.pallas_call(
        functools.partial(_lru_kernel, nb=nb),
        out_shape=(sds, sds),
        grid=(nb + 1,),
        in_specs=in_f + in_b + [
            pl.BlockSpec((CONV_W, LRU_WIDTH), lambda s: (0, 0)),
            pl.BlockSpec((1, LRU_WIDTH), lambda s: (0, 0)),
            mat(), vec(), mat(), vec(), vec(),
        ],
        out_specs=(out_f, out_b),
        scratch_shapes=[pltpu.VMEM((2, TM, LRU_WIDTH), F32), pltpu.VMEM((2, TM, LRU_WIDTH), F32),
                        pltpu.VMEM((SUBLANES, LRU_WIDTH), F32)],
        compiler_params=_cp(("arbitrary",)),
        name="lru",
    )(hmat, hmat, hmat, hmat, hmat, hmat, conv_w, conv_b, wr, br, wi, bi, sp)


def _cmul(ar, ai, br, bi):
    return ar * br - ai * bi, ar * bi + ai * br


def _s5_kernel(u_ref, p_ref, pt_ref, bm_ref, cm_ref, a_ref, pw_ref, o_ref, x_s, st_s, c_s):
    s = pl.program_id(0)
    n = S5_N

    @pl.when(s == 0)
    def _():
        c_s[...] = jnp.zeros_like(c_s)

    up = jnp.dot(p_ref[...], u_ref[...], preferred_element_type=F32).astype(BF16)
    nc = n // S5_CHUNKS
    for c in range(S5_CHUNKS):
        xc = jnp.dot(up[:, c * LANES:(c + 1) * LANES], bm_ref[c], preferred_element_type=F32)
        x_s[:, c * nc:(c + 1) * nc] = xc[:, 0:nc]
        x_s[:, n + c * nc:n + (c + 1) * nc] = xc[:, nc:2 * nc]
    ns = n // S5_SCAN_CHUNKS
    for c in range(S5_SCAN_CHUNKS):
        re = slice(c * ns, (c + 1) * ns)
        im = slice(n + c * ns, n + (c + 1) * ns)
        ar = jnp.broadcast_to(a_ref[0:1, re], (SUBLANES, ns))
        ai = jnp.broadcast_to(a_ref[1:2, re], (SUBLANES, ns))

        def local(m, carry, re=re, im=im, ar=ar, ai=ai):
            hr, hi = carry
            row = pl.multiple_of(m * SUBLANES, SUBLANES)
            pr, pi = _cmul(ar, ai, hr, hi)
            hr = pr + x_s[pl.ds(row, SUBLANES), re]
            hi = pi + x_s[pl.ds(row, SUBLANES), im]
            x_s[pl.ds(row, SUBLANES), re] = hr
            x_s[pl.ds(row, SUBLANES), im] = hi
            return hr, hi

        z = jnp.zeros((SUBLANES, ns), F32)
        fr, fi = lax.fori_loop(0, SEG, local, (z, z), unroll=4)

        qr = pw_ref[TM - 1:TM, re]
        qi = pw_ref[TM - 1:TM, im]
        sr = c_s[0:1, re]
        si = c_s[0:1, im]
        for j in range(SUBLANES):
            st_s[j:j + 1, re] = sr
            st_s[j:j + 1, im] = si
            pr, pi = _cmul(qr, qi, sr, si)
            sr = pr + fr[j:j + 1]
            si = pi + fi[j:j + 1]
        c_s[0:1, re] = sr
        c_s[0:1, im] = si

        str_ = st_s[:, re]
        sti = st_s[:, im]

        def fix(m, _, re=re, im=im, str_=str_, sti=sti):
            row = pl.multiple_of(m * SUBLANES, SUBLANES)
            pr, pi = _cmul(pw_ref[pl.ds(row, SUBLANES), re], pw_ref[pl.ds(row, SUBLANES), im], str_, sti)
            x_s[pl.ds(row, SUBLANES), re] = x_s[pl.ds(row, SUBLANES), re] + pr
            x_s[pl.ds(row, SUBLANES), im] = x_s[pl.ds(row, SUBLANES), im] + pi
            return 0

        lax.fori_loop(0, SEG, fix, 0, unroll=2)

    yp = jnp.concatenate(
        [jnp.dot(x_s[:, c * nc:(c + 1) * nc].astype(BF16), cm_ref[0, c], preferred_element_type=F32)
         + jnp.dot(x_s[:, n + c * nc:n + (c + 1) * nc].astype(BF16), cm_ref[1, c], preferred_element_type=F32)
         for c in range(S5_CHUNKS)], axis=1)
    y_hi = yp.astype(BF16)
    y_lo = (yp - y_hi.astype(F32)).astype(BF16)
    pt = pt_ref[...]
    o_ref[...] = (jnp.dot(pt, y_hi, preferred_element_type=F32)
                  + jnp.dot(pt, y_lo, preferred_element_type=F32))


def _s5_perm(reverse):
    p = np.zeros((TM, TM), np.float32)
    for m in range(SEG):
        for j in range(SUBLANES):
            pos = j * SEG + m
            p[m * SUBLANES + j, (TM - 1 - pos) if reverse else pos] = 1.0
    return p


def _s5(hmat, nb, reverse, bm_t, cm_t, a_row, pw):
    t = hmat.shape[0]
    p = _s5_perm(reverse)
    n2 = 2 * S5_N

    def bmap(s):
        return _scan_block(s, nb, reverse)

    return pl.pallas_call(
        _s5_kernel,
        out_shape=jax.ShapeDtypeStruct((t, S5_WIDTH), F32),
        grid=(nb + 1,),
        in_specs=[
            pl.BlockSpec((TM, S5_WIDTH), lambda s: (bmap(s), 2)),
            pl.BlockSpec((TM, TM), lambda s: (0, 0)),
            pl.BlockSpec((TM, TM), lambda s: (0, 0)),
            pl.BlockSpec((S5_CHUNKS, LANES, n2 // S5_CHUNKS), lambda s: (0, 0, 0)),
            pl.BlockSpec((2, S5_CHUNKS, S5_N // S5_CHUNKS, LANES), lambda s: (0, 0, 0, 0)),
            pl.BlockSpec((2, S5_N), lambda s: (0, 0)),
            pl.BlockSpec((TM, n2), lambda s: (0, 0)),
        ],
        out_specs=pl.BlockSpec((TM, S5_WIDTH), lambda s: (bmap(s), 0)),
        scratch_shapes=[pltpu.VMEM((TM, n2), F32), pltpu.VMEM((SUBLANES, n2), F32),
                        pltpu.VMEM((1, n2), F32)],
        compiler_params=_cp(("arbitrary",)),
        name="s5_bwd" if reverse else "s5_fwd",
    )(hmat, jnp.asarray(p, BF16), jnp.asarray(p.T, BF16), bm_t, cm_t, a_row, pw)


def _s5_params(a_re, a_im, log_dt, b_re, b_im, c_re, c_im):
    g, p = a_re.shape
    dt = jnp.exp(log_dt)[:, None]
    lr, li = a_re, a_im
    ea = jnp.exp(lr * dt)
    ab_r, ab_i = ea * jnp.cos(li * dt), ea * jnp.sin(li * dt)
    den = lr * lr + li * li
    co_r = ((ab_r - 1.0) * lr + ab_i * li) / den
    co_i = (ab_i * lr - (ab_r - 1.0) * li) / den
    bb_r = co_r[..., None] * b_re - co_i[..., None] * b_im
    bb_i = co_r[..., None] * b_im + co_i[..., None] * b_re
    gc = g // S5_CHUNKS
    eye = jnp.eye(gc, dtype=F32)

    def chunked(w, spec):
        return jnp.einsum(spec, w.reshape((S5_CHUNKS, gc) + w.shape[1:]), eye)

    bm_r = chunked(bb_r, 'kgpc,gh->kgchp').reshape(S5_CHUNKS, gc * S5_GROUP, gc * p)
    bm_i = chunked(bb_i, 'kgpc,gh->kgchp').reshape(S5_CHUNKS, gc * S5_GROUP, gc * p)
    bm_t = jnp.concatenate([bm_r, bm_i], axis=2).astype(BF16)
    cm_r = chunked(c_re, 'kgcp,gh->kgphc').reshape(S5_CHUNKS, gc * p, gc * S5_GROUP)
    cm_i = chunked(-c_im, 'kgcp,gh->kgphc').reshape(S5_CHUNKS, gc * p, gc * S5_GROUP)
    cm_t = jnp.stack([cm_r, cm_i], axis=0).astype(BF16)
    a_row = jnp.stack([ab_r.reshape(-1), ab_i.reshape(-1)], axis=0)
    k = jnp.arange(1, SEG + 1, dtype=F32)[:, None, None]
    ek = jnp.exp(k * (lr * dt))
    pw = jnp.concatenate([(ek * jnp.cos(k * (li * dt))).reshape(SEG, -1),
                          (ek * jnp.sin(k * (li * dt))).reshape(SEG, -1)], axis=1)
    return bm_t, cm_t, a_row, jnp.repeat(pw, SUBLANES, axis=0)


def _qkv_kernel(q_ref, k_ref, v_ref, cos_ref, sin_ref, qa_ref, qb_ref, ko_ref, vo_ref):
    w = DA_HEADS * 2 * DA_HEAD_DIM
    cos = jnp.concatenate([cos_ref[...]] * (w // LANES), axis=1)
    sin = jnp.concatenate([sin_ref[...]] * (w // LANES), axis=1)
    lane = lax.broadcasted_iota(jnp.int32, (TM, w), 1)
    first = (lane % (2 * ROPE_FREQS)) < ROPE_FREQS

    def rope(x):
        swapped = jnp.where(first, pltpu.roll(x, w - ROPE_FREQS, 1), pltpu.roll(x, ROPE_FREQS, 1))
        return x * cos + swapped * sin

    q = rope(q_ref[...].astype(F32)) * (DA_SCALE * LOG2_E)
    in_a = (lane % (2 * DA_HEAD_DIM)) < DA_HEAD_DIM
    qa_ref[...] = jnp.where(in_a, q, 0.0).astype(BF16)
    qb_ref[...] = jnp.where(in_a, 0.0, q).astype(BF16)
    ko_ref[...] = rope(k_ref[...].astype(F32)).astype(BF16)
    v = v_ref[...]
    ones = jnp.ones((TM, DA_V_DIM), BF16)
    vo_ref[...] = jnp.concatenate(
        [blk for h in range(DA_HEADS) for blk in (v[:, h * DA_V_DIM:(h + 1) * DA_V_DIM], ones)], axis=1)


def _qkv_prep(hmat, cos_t, sin_t):
    t = hmat.shape[0]
    w = DA_HEADS * 2 * DA_HEAD_DIM
    col = lambda c: pl.BlockSpec((TM, w), lambda i: (i, c))
    tab = lambda: pl.BlockSpec((TM, LANES), lambda i: (i, 0))
    out = lambda: pl.BlockSpec((TM, w), lambda i: (i, 0))
    sds = jax.ShapeDtypeStruct((t, w), BF16)
    return pl.pallas_call(
        _qkv_kernel,
        out_shape=(sds, sds, sds, jax.ShapeDtypeStruct((t, 2 * w), BF16)),
        grid=(t // TM,),
        in_specs=[col(3), col(4), col(5), tab(), tab()],
        out_specs=(out(), out(), out(), pl.BlockSpec((TM, 2 * w), lambda i: (i, 0))),
        compiler_params=_cp(("arbitrary",)),
        name="qkv_prep",
    )(hmat, hmat, hmat, cos_t, sin_t)


def _rope_tables(n_lat, n_ctx):
    pos = jnp.arange(n_lat)
    freqs = ROPE_BASE ** (-jnp.arange(ROPE_FREQS, dtype=F32) / ROPE_FREQS)
    ang_r = (pos // GRID_W).astype(F32)[:, None] * freqs
    ang_c = (pos % GRID_W).astype(F32)[:, None] * freqs
    cr, sr, cc, sc = jnp.cos(ang_r), jnp.sin(ang_r), jnp.cos(ang_c), jnp.sin(ang_c)
    cos64 = jnp.concatenate([cr, cr, cc, cc], axis=1)
    sin64 = jnp.concatenate([-sr, sr, -sc, sc], axis=1)
    cos_t = jnp.concatenate([cos64, cos64], axis=1)
    sin_t = jnp.concatenate([sin64, sin64], axis=1)
    cos_t = jnp.concatenate([cos_t, jnp.ones((n_ctx, LANES), F32)], axis=0)
    sin_t = jnp.concatenate([sin_t, jnp.zeros((n_ctx, LANES), F32)], axis=0)
    return cos_t, sin_t


def _attn_kernel(lam_ref, g_ref, qa_ref, qb_ref, k_ref, v_ref, prev_ref, o_ref, m_s, acc_s, a_s, s_s,
                 *, lam_init, nk):
    del prev_ref
    g = pl.program_id(1)
    kj = g % nk

    @pl.when(g == 0)
    def _():
        s_s[...] = jnp.zeros_like(s_s)
        acc_s[...] = jnp.zeros_like(acc_s)
        a_s[...] = jnp.zeros_like(a_s)
        m_s[...] = jnp.zeros_like(m_s)

    def step(cur):
        k = k_ref[...]
        v = v_ref[...]
        for i, q_ref in enumerate((qa_ref, qb_ref)):
            m_old = m_s[i]
            p = jnp.exp2(s_s[1 - cur, i] - m_old[:, 0:1]).astype(BF16)
            alpha = a_s[i]
            acc_s[i] = (jnp.concatenate([alpha, alpha], axis=1) * acc_s[i]
                        + jnp.dot(p, v, preferred_element_type=F32))
            s = lax.dot_general(q_ref[...], k, (((1,), (1,)), ((), ())), preferred_element_type=F32)
            s_s[cur, i] = s
            m_base = jnp.where(kj == 0, -jnp.inf, m_old)
            m_new = jnp.maximum(m_base, jnp.max(s, axis=1, keepdims=True))
            a_s[i] = jnp.exp2(m_base - m_new)
            m_s[i] = m_new

    def finish():
        lv = lam_ref[...]
        lam = (jnp.exp(jnp.sum(lv[0:1] * lv[1:2], axis=1, keepdims=True))
               - jnp.exp(jnp.sum(lv[2:3] * lv[3:4], axis=1, keepdims=True)) + lam_init)
        a0, a1 = acc_s[0], acc_s[1]
        o = (a0[:, 0:DA_V_DIM] / a0[:, DA_V_DIM:2 * DA_V_DIM]
             - lam * (a1[:, 0:DA_V_DIM] / a1[:, DA_V_DIM:2 * DA_V_DIM]))
        ms = jnp.mean(o * o, axis=1, keepdims=True)
        o_ref[...] = o * lax.rsqrt(ms + RMS_EPS) * g_ref[...] * (1.0 - lam_init)

    for parity in range(2):
        pl.when(g % 2 == parity)(functools.partial(step, parity))
    pl.when(jnp.logical_and(kj == 0, g > 0))(finish)


def _attention(qa, qb, k, v, lam_vec, subln_g, lam_init, prev, *, q0, nq, tq, k0, nk, tk, name):
    t = qa.shape[0]
    n_pairs = nq * nk

    def qmap(h, g):
        return (q0 + jnp.minimum(g // nk, nq - 1), h)

    return pl.pallas_call(
        functools.partial(_attn_kernel, lam_init=lam_init, nk=nk),
        out_shape=jax.ShapeDtypeStruct((t, DA_HEADS * DA_V_DIM), F32),
        grid=(DA_HEADS, n_pairs + 1),
        in_specs=[
            pl.BlockSpec((4, DA_HEAD_DIM), lambda h, g: (0, 0)),
            pl.BlockSpec((1, DA_V_DIM), lambda h, g: (0, 0)),
            pl.BlockSpec((tq, LANES), qmap),
            pl.BlockSpec((tq, LANES), qmap),
            pl.BlockSpec((tk, LANES), lambda h, g: (k0 + g % nk, h)),
            pl.BlockSpec((tk, 2 * DA_V_DIM), lambda h, g: (k0 + (g + nk - 1) % nk, h)),
            pl.BlockSpec(memory_space=pl.ANY),
        ],
        out_specs=pl.BlockSpec((tq, LANES), lambda h, g: (q0 + jnp.maximum(g - 1, 0) // nk, h)),
        scratch_shapes=[pltpu.VMEM((2, tq, LANES), F32), pltpu.VMEM((2, tq, 2 * DA_V_DIM), F32),
                        pltpu.VMEM((2, tq, LANES), F32), pltpu.VMEM((2, 2, tq, tk), F32)],
        input_output_aliases={6: 0},
        compiler_params=_cp(("arbitrary", "arbitrary")),
        name=name,
    )(lam_vec, subln_g, qa, qb, k, v, prev)


def _merge_kernel(x_ref, mod_ref, gt_ref, ga_ref, ub_ref, hf_ref, hb_ref, sf_ref, sb_ref, yc_ref,
                  sd_ref, wglu_ref, bglu_ref, wp_ref, bg_ref, wo_ref, lng_ref, lnb_ref,
                  x1_ref, v_ref, *, dn_alpha):
    md = mod_ref[0]
    ya = (hf_ref[...] + hb_ref[...]) * _gelu_tanh(ga_ref[...].astype(F32))
    y = sf_ref[...] + sb_ref[...] + sd_ref[...] * ub_ref[...].astype(F32)
    tg = jnp.dot(_gelu_tanh(y).astype(BF16), wglu_ref[...], preferred_element_type=F32) + bglu_ref[...]
    yb = tg[:, 0:S5_WIDTH] * _sigmoid(tg[:, S5_WIDTH:2 * S5_WIDTH])
    yc = yc_ref[...]
    bg = bg_ref[...]
    z = jnp.zeros((TM, D_MODEL), F32)
    for n, br in enumerate((ya, yb, yc)):
        gate = _sigmoid(gt_ref[:, n * D_MODEL:(n + 1) * D_MODEL].astype(F32) + bg[n:n + 1])
        z = z + gate * jnp.dot(br.astype(BF16), wp_ref[n], preferred_element_type=F32)
    m = jnp.dot(z.astype(BF16), wo_ref[...], preferred_element_type=F32)
    x1 = _layer_norm(dn_alpha * x_ref[...] + md[2:3] * m, lng_ref[...], lnb_ref[...])
    x1_ref[...] = x1
    v_ref[...] = x1 * (1.0 + md[4:5]) + md[3:4]


def _merge(x, modv, hmat, hf, hb, sf, sb, yc, s5_d, w_glu, b_glu, w_proj, b_gate, w_out, ln_g, ln_b,
           n_lat_blocks, dn_alpha):
    t = x.shape[0]
    row = lambda w, c=0: pl.BlockSpec((TM, w), lambda i: (i, c))
    const = lambda *shape: pl.BlockSpec(shape, lambda i: (0,) * len(shape))
    sds = jax.ShapeDtypeStruct((t, D_MODEL), F32)
    return pl.pallas_call(
        functools.partial(_merge_kernel, dn_alpha=dn_alpha),
        out_shape=(sds, sds),
        grid=(t // TM,),
        in_specs=[
            row(D_MODEL),
            pl.BlockSpec((1, 6, D_MODEL), lambda i: (i // n_lat_blocks, 0, 0)),
            row(N_BRANCH * D_MODEL, 1), row(BW, 1), row(BW, 2),
            row(BW), row(BW), row(BW), row(BW), row(BW),
            const(1, S5_WIDTH), const(S5_WIDTH, 2 * S5_WIDTH), const(1, 2 * S5_WIDTH),
            const(N_BRANCH, BW, D_MODEL), const(N_BRANCH, D_MODEL), const(D_MODEL, D_MODEL),
            const(1, D_MODEL), const(1, D_MODEL),
        ],
        out_specs=(row(D_MODEL), row(D_MODEL)),
        compiler_params=_cp(("arbitrary",)),
        name="merge",
    )(x, modv, hmat, hmat, hmat, hf, hb, sf, sb, yc, s5_d, w_glu, b_glu, w_proj, b_gate, w_out,
      ln_g, ln_b)


def _router_kernel(v_ref, w_ref, b_ref, tri_ref, ti_ref, gw_ref, rk_ref, cnt_ref, c_s):
    s = pl.program_id(0)

    @pl.when(s == 0)
    def _():
        c_s[...] = jnp.zeros_like(c_s)

    logits = jnp.dot(v_ref[...], w_ref[...], preferred_element_type=F32, precision=HIGHEST) + b_ref[...]
    lane = lax.broadcasted_iota(jnp.int32, (TM, N_EXPERTS), 1).astype(F32)
    work = logits
    vals, idxs = [], []
    hot = jnp.zeros((TM, N_EXPERTS), F32)
    for _ in range(TOP_K):
        mx = jnp.max(work, axis=1, keepdims=True)
        ix = jnp.min(jnp.where(work == mx, lane, float(N_EXPERTS)), axis=1, keepdims=True)
        sel = lane == ix
        hot = jnp.where(sel, 1.0, hot)
        work = jnp.where(sel, -jnp.inf, work)
        vals.append(mx)
        idxs.append(ix)
    es = [jnp.exp(vv - vals[0]) for vv in vals]
    den = es[0] + es[1] + es[2] + es[3]
    before = jnp.dot(tri_ref[...], hot.astype(BF16), preferred_element_type=F32) + c_s[0:1, :]
    col = lax.broadcasted_iota(jnp.int32, (TM, TOP_K), 1)
    ti = jnp.zeros((TM, TOP_K), F32)
    gw = jnp.zeros((TM, TOP_K), F32)
    rk = jnp.zeros((TM, TOP_K), F32)
    for kk in range(TOP_K):
        rank = jnp.sum(jnp.where(lane == idxs[kk], before, 0.0), axis=1, keepdims=True)
        ti = jnp.where(col == kk, idxs[kk], ti)
        gw = jnp.where(col == kk, es[kk] / den, gw)
        rk = jnp.where(col == kk, rank, rk)
    ti_ref[...] = ti.astype(jnp.int32)
    gw_ref[...] = gw
    rk_ref[...] = rk.astype(jnp.int32)
    tot = c_s[0:1, :] + jnp.sum(hot, axis=0, keepdims=True)
    c_s[...] = jnp.broadcast_to(tot, c_s.shape)
    cnt_ref[...] = jnp.broadcast_to(tot, cnt_ref.shape).astype(jnp.int32)


def _router(v, w_router, b_router):
    t = v.shape[0]
    tri = jnp.asarray(np.tril(np.ones((TM, TM), np.float32), -1), BF16)
    row4 = lambda: pl.BlockSpec((TM, TOP_K), lambda i: (i, 0))
    return pl.pallas_call(
        _router_kernel,
        out_shape=(jax.ShapeDtypeStruct((t, TOP_K), jnp.int32),
                   jax.ShapeDtypeStruct((t, TOP_K), F32),
                   jax.ShapeDtypeStruct((t, TOP_K), jnp.int32),
                   jax.ShapeDtypeStruct((SUBLANES, N_EXPERTS), jnp.int32)),
        grid=(t // TM,),
        in_specs=[
            pl.BlockSpec((TM, D_MODEL), lambda i: (i, 0)),
            pl.BlockSpec((D_MODEL, N_EXPERTS), lambda i: (0, 0)),
            pl.BlockSpec((1, N_EXPERTS), lambda i: (0, 0)),
            pl.BlockSpec((TM, TM), lambda i: (0, 0)),
        ],
        out_specs=(row4(), row4(), row4(), pl.BlockSpec((SUBLANES, N_EXPERTS), lambda i: (0, 0))),
        scratch_shapes=[pltpu.VMEM((SUBLANES, N_EXPERTS), F32)],
        compiler_params=_cp(("arbitrary",)),
        name="router",
    )(v, w_router, b_router, tri)


def _dispatch_kernel(dest_ref, v_ref, xs_in, xs_hbm, sem):
    del xs_in

    def row_copy(r, kk):
        return pltpu.make_async_copy(v_ref.at[pl.ds(r, 1)],
                                     xs_hbm.at[pl.ds(dest_ref[r * TOP_K + kk], 1)], sem)

    def start(r, _):
        for kk in range(TOP_K):
            row_copy(r, kk).start()
        return 0

    def wait(r, _):
        for kk in range(TOP_K):
            row_copy(r, kk).wait()
        return 0

    lax.fori_loop(0, TM, start, 0, unroll=8)
    lax.fori_loop(0, TM, wait, 0, unroll=8)


def _dispatch(dest_flat, v, n_slots):
    t = v.shape[0]
    xs0 = jnp.zeros((n_slots, D_MODEL), F32)
    return pl.pallas_call(
        _dispatch_kernel,
        out_shape=jax.ShapeDtypeStruct((n_slots, D_MODEL), F32),
        grid=(t // TM,),
        in_specs=[
            pl.BlockSpec((TM * TOP_K,), lambda i: (i,), memory_space=pltpu.SMEM),
            pl.BlockSpec((TM, D_MODEL), lambda i: (i, 0)),
            pl.BlockSpec(memory_space=pl.ANY),
        ],
        out_specs=pl.BlockSpec(memory_space=pl.ANY),
        scratch_shapes=[pltpu.SemaphoreType.DMA],
        input_output_aliases={2: 0},
        compiler_params=_cp(("arbitrary",)),
        name="moe_dispatch",
    )(dest_flat, v, xs0)


def _expert_kernel(be_ref, nu_ref, x_ref, wgu_ref, bgu_ref, wd_ref, bd_ref, o_ref, wgu_s, wd_s):
    b = pl.program_id(0)
    new_expert = jnp.logical_or(b == 0, be_ref[b] != be_ref[jnp.maximum(b - 1, 0)])

    @pl.when(jnp.logical_and(new_expert, b < nu_ref[0]))
    def _():
        wgu_s[...] = wgu_ref[0, 0].astype(BF16)
        wd_s[...] = wd_ref[0, 0].astype(BF16)

    @pl.when(b < nu_ref[0])
    def _():
        for r in range(0, MOE_BLK, MOE_SUB):
            rows = slice(r, r + MOE_SUB)
            h = jnp.dot(x_ref[rows, :].astype(BF16), wgu_s[...], preferred_element_type=F32) + bgu_ref[0]
            hg = jnp.minimum(h[:, 0:D_FF], SWIGLU_LIMIT)
            hl = jnp.clip(h[:, D_FF:2 * D_FF], -SWIGLU_LIMIT, SWIGLU_LIMIT)
            act = hg * _sigmoid(SWIGLU_ALPHA * hg) * (hl + 1.0)
            o_ref[rows, :] = jnp.dot(act.astype(BF16), wd_s[...], preferred_element_type=F32) + bd_ref[0]

    @pl.when(pl.program_id(0) >= nu_ref[0])
    def _():
        o_ref[...] = jnp.zeros_like(o_ref)


def _experts(blk_e, n_used, xs, layer, w_gu, b_gu, w_down, b_down):
    n_slots = xs.shape[0]
    nblk = n_slots // MOE_BLK
    grid_spec = pltpu.PrefetchScalarGridSpec(
        num_scalar_prefetch=2,
        grid=(nblk,),
        in_specs=[
            pl.BlockSpec((MOE_BLK, D_MODEL), lambda b, be, nu: (jnp.minimum(b, nu[0] - 1), 0)),
            pl.BlockSpec((1, 1, D_MODEL, 2 * D_FF), lambda b, be, nu: (layer, be[b], 0, 0)),
            pl.BlockSpec((1, 1, 2 * D_FF), lambda b, be, nu: (be[b], 0, 0)),
            pl.BlockSpec((1, 1, D_FF, D_MODEL), lambda b, be, nu: (layer, be[b], 0, 0)),
            pl.BlockSpec((1, 1, D_MODEL), lambda b, be, nu: (be[b], 0, 0)),
        ],
        out_specs=pl.BlockSpec((MOE_BLK, D_MODEL), lambda b, be, nu: (b, 0)),
        scratch_shapes=[pltpu.VMEM((D_MODEL, 2 * D_FF), BF16), pltpu.VMEM((D_FF, D_MODEL), BF16)],
    )
    return pl.pallas_call(
        _expert_kernel,
        out_shape=jax.ShapeDtypeStruct((n_slots, D_MODEL), F32),
        grid_spec=grid_spec,
        compiler_params=_cp(("arbitrary",)),
        name="moe_experts",
    )(blk_e, n_used, xs, w_gu, b_gu.reshape(N_EXPERTS, 1, 2 * D_FF), w_down,
      b_down.reshape(N_EXPERTS, 1, D_MODEL))


def _combine_kernel(dest_ref, ys_hbm, gw_ref, x_ref, mod_ref, lng_ref, lnb_ref, o_ref, g_s, sem,
                    *, dn_alpha):
    def row_copy(r, kk):
        return pltpu.make_async_copy(ys_hbm.at[pl.ds(dest_ref[r * TOP_K + kk], 1)],
                                     g_s.at[kk, pl.ds(r, 1)], sem)

    def start(r, _):
        for kk in range(TOP_K):
            row_copy(r, kk).start()
        return 0

    def wait(r, _):
        for kk in range(TOP_K):
            row_copy(r, kk).wait()
        return 0

    lax.fori_loop(0, TM, start, 0, unroll=8)
    lax.fori_loop(0, TM, wait, 0, unroll=8)
    gw = gw_ref[...]
    f = jnp.zeros((TM, D_MODEL), F32)
    for kk in range(TOP_K):
        f = f + gw[:, kk:kk + 1] * g_s[kk]
    md = mod_ref[0]
    o_ref[...] = _layer_norm(dn_alpha * x_ref[...] + md[5:6] * f, lng_ref[...], lnb_ref[...])


def _combine(dest_flat, ys, gate_w, x1, modv, ln_g, ln_b, n_lat_blocks, dn_alpha):
    t = x1.shape[0]
    return pl.pallas_call(
        functools.partial(_combine_kernel, dn_alpha=dn_alpha),
        out_shape=jax.ShapeDtypeStruct((t, D_MODEL), F32),
        grid=(t // TM,),
        in_specs=[
            pl.BlockSpec((TM * TOP_K,), lambda i: (i,), memory_space=pltpu.SMEM),
            pl.BlockSpec(memory_space=pl.ANY),
            pl.BlockSpec((TM, TOP_K), lambda i: (i, 0)),
            pl.BlockSpec((TM, D_MODEL), lambda i: (i, 0)),
            pl.BlockSpec((1, 6, D_MODEL), lambda i: (i // n_lat_blocks, 0, 0)),
            pl.BlockSpec((1, D_MODEL), lambda i: (0, 0)),
            pl.BlockSpec((1, D_MODEL), lambda i: (0, 0)),
        ],
        out_specs=pl.BlockSpec((TM, D_MODEL), lambda i: (i, 0)),
        scratch_shapes=[pltpu.VMEM((TOP_K, TM, D_MODEL), F32), pltpu.SemaphoreType.DMA],
        compiler_params=_cp(("arbitrary",)),
        name="moe_combine",
    )(dest_flat, ys, gate_w, x1, modv, ln_g, ln_b)


def _moe(v, x1, modv, layer, w_router, b_router, w_gu, b_gu, w_down, b_down, ln_g, ln_b, n_lat_blocks,
         dn_alpha):
    t = v.shape[0]
    top_i, gate_w, rank, counts = _router(v, w_router, b_router.reshape(1, N_EXPERTS))
    counts = counts[0]
    padded = (counts + MOE_BLK - 1) // MOE_BLK * MOE_BLK
    pend = jnp.cumsum(padded)
    pstart = pend - padded
    dest = (pstart[top_i] + rank).reshape(-1).astype(jnp.int32)
    n_slots = t * TOP_K + N_EXPERTS * MOE_BLK
    nblk = n_slots // MOE_BLK
    starts = jnp.arange(nblk, dtype=jnp.int32) * MOE_BLK
    blk_e = jnp.minimum(jnp.sum((pend[None, :] <= starts[:, None]).astype(jnp.int32), axis=1),
                        N_EXPERTS - 1).astype(jnp.int32)
    n_used = (pend[-1:] // MOE_BLK).astype(jnp.int32)
    xs = _dispatch(dest, v, n_slots)
    ys = _experts(blk_e, n_used, xs, layer, w_gu, b_gu, w_down, b_down)
    return _combine(dest, ys, gate_w, x1, modv, ln_g, ln_b, n_lat_blocks, dn_alpha)


def _block_diag(w):
    nbk, bs, _ = w.shape
    eye = jnp.eye(nbk, dtype=w.dtype)
    return jnp.einsum('hij,hg->higj', w, eye).reshape(nbk * bs, nbk * bs)


def _query_block(n_lat):
    for tq in (1024, 512):
        if n_lat % tq == 0:
            return tq
    return TM


def _key_block(t):
    for tk in (1280, 640, 256):
        if t % tk == 0:
            return tk
    raise ValueError("token count must be a multiple of 256")


def kernel(x, c, ctx, c_ctx, w_mod, b_mod, w_in, conv_w, conv_b, lru_wr, lru_br, lru_wi, lru_bi, lru_lam, s5_a_re, s5_a_im, s5_log_dt, s5_b_re, s5_b_im, s5_c_re, s5_c_im, s5_d, s5_w_glu, s5_b_glu, da_lam, da_subln_g, w_proj, b_gate, w_out, ln_g, ln_b, w_router, b_router, w_gu, b_gu, w_down, b_down):
    bsz, n_lat, d = x.shape
    n_ctx = ctx.shape[1]
    depth = w_mod.shape[0]
    assert bsz == 1 and d == D_MODEL and n_ctx == TM and n_lat % TM == 0 and n_lat % GRID_W == 0
    nb = n_lat // TM
    t = n_lat + n_ctx
    dn_alpha = (2 * depth) ** 0.25

    cc = jnp.zeros((SUBLANES, D_MODEL), F32).at[0].set(c[0]).at[1].set(c_ctx)
    mod = _modulation(cc, w_mod, b_mod)
    cos_t, sin_t = _rope_tables(n_lat, n_ctx)
    tq = _query_block(n_lat)
    tk = _key_block(t)

    xs = jnp.concatenate([x[0], ctx[0]], axis=0)
    for l in range(depth):
        modv = mod[l, 0:2].reshape(2, 6, D_MODEL)
        hmat = _inproj(xs, modv, w_in[l].astype(BF16), nb)

        sp = jax.nn.softplus(-lru_lam[l])
        hdir = _lru(hmat, nb, conv_w[l], conv_b[l].reshape(1, -1),
                    jnp.stack([_block_diag(lru_wr[l, dr]) for dr in range(2)]).astype(BF16),
                    lru_br[l].reshape(2, 1, -1),
                    jnp.stack([_block_diag(lru_wi[l, dr]) for dr in range(2)]).astype(BF16),
                    lru_bi[l].reshape(2, 1, -1), sp.reshape(2, 1, -1))
        sdir = []
        for dr in range(2):
            prm = _s5_params(s5_a_re[l, dr], s5_a_im[l, dr], s5_log_dt[l, dr], s5_b_re[l, dr],
                             s5_b_im[l, dr], s5_c_re[l, dr], s5_c_im[l, dr])
            sdir.append(_s5(hmat, nb, dr == 1, *prm))

        qa, qb, kk, vv = _qkv_prep(hmat, cos_t, sin_t)
        lam_init = 0.8 - 0.6 * math.exp(-0.3 * l)
        g_row = da_subln_g[l].reshape(1, DA_V_DIM)
        yc = jnp.zeros((t, DA_HEADS * DA_V_DIM), F32)
        yc = _attention(qa, qb, kk, vv, da_lam[l], g_row, lam_init, yc,
                        q0=0, nq=n_lat // tq, tq=tq, k0=0, nk=t // tk, tk=tk, name="attn_lat")
        yc = _attention(qa, qb, kk, vv, da_lam[l], g_row, lam_init, yc,
                        q0=nb, nq=1, tq=TM, k0=nb, nk=1, tk=TM, name="attn_ctx")

        x1, v = _merge(xs, modv, hmat, hdir[0], hdir[1], sdir[0], sdir[1], yc,
                       s5_d[l].reshape(1, -1), s5_w_glu[l].astype(BF16), s5_b_glu[l].reshape(1, -1),
                       w_proj[l].astype(BF16), b_gate[l], w_out[l].astype(BF16),
                       ln_g[l, 0].reshape(1, -1), ln_b[l, 0].reshape(1, -1), nb, dn_alpha)
        xs = _moe(v, x1, modv, l, w_router[l], b_router[l], w_gu, b_gu[l], w_down, b_down[l],
                  ln_g[l, 1].reshape(1, -1), ln_b[l, 1].reshape(1, -1), nb, dn_alpha)
    return xs[:n_lat][None]
```

```python
import functools
import math

import numpy as np
import jax
import jax.numpy as jnp
from jax import lax
from jax.experimental import pallas as pl
from jax.experimental.pallas import tpu as pltpu

F32 = jnp.float32
BF16 = jnp.bfloat16
HIGHEST = lax.Precision.HIGHEST

D_MODEL = 1024
GRID_W = 64
LRU_WIDTH = 512
LRU_BLOCKS = 8
CONV_W = 4
LRU_C = 8.0
S5_WIDTH = 512
S5_GROUP = 16
S5_GROUPS = S5_WIDTH // S5_GROUP
S5_STATE = 64
S5_N = S5_GROUPS * S5_STATE
DA_HEADS = 4
DA_HEAD_DIM = 64
DA_V_DIM = 128
DA_SCALE = DA_HEAD_DIM ** -0.5
ROPE_BASE = 10000.0
ROPE_FREQS = DA_HEAD_DIM // 4
N_BRANCH = 3
BW = 512
IN_COLS = 6 * BW + N_BRANCH * D_MODEL
N_EXPERTS = 32
TOP_K = 4
D_FF = D_MODEL
SWIGLU_ALPHA = 1.702
SWIGLU_LIMIT = 7.0
LN_EPS = 1e-5
RMS_EPS = 1e-6

SUBLANES = 8
LANES = 128
VMEM_LIMIT = 56 * 1024 * 1024

TM = 256
HALO = 16
SEG = TM // SUBLANES
S5_CHUNKS = S5_WIDTH // LANES
S5_SCAN_CHUNKS = 2
LOG2_E = math.log2(math.e)
MOE_BLK = 512
MOE_SUB = 256
TN_IN = 3072


def _cp(sem, vmem=VMEM_LIMIT):
    return pltpu.CompilerParams(dimension_semantics=sem, vmem_limit_bytes=vmem)


def _sigmoid(x):
    return 1.0 / (1.0 + jnp.exp(-x))


def _gelu_tanh(x):
    return 0.5 * x * (1.0 + jnp.tanh(math.sqrt(2.0 / math.pi) * (x + 0.044715 * (x * x * x))))


def _layer_norm(y, g, b):
    mu = jnp.mean(y, axis=-1, keepdims=True)
    yc = y - mu
    var = jnp.mean(yc * yc, axis=-1, keepdims=True)
    return yc * lax.rsqrt(var + LN_EPS) * g + b


def _mod_kernel(c_ref, w_ref, b_ref, o_ref):
    c = c_ref[...]
    s = c * _sigmoid(c)
    o_ref[0] = jnp.dot(s, w_ref[0], preferred_element_type=F32, precision=HIGHEST) + b_ref[0]


def _modulation(cc, w_mod, b_mod):
    depth = w_mod.shape[0]
    tn = 1536
    return pl.pallas_call(
        _mod_kernel,
        out_shape=jax.ShapeDtypeStruct((depth, SUBLANES, 6 * D_MODEL), F32),
        grid=(depth, 6 * D_MODEL // tn),
        in_specs=[
            pl.BlockSpec((SUBLANES, D_MODEL), lambda l, j: (0, 0)),
            pl.BlockSpec((1, D_MODEL, tn), lambda l, j: (l, 0, j)),
            pl.BlockSpec((1, 1, tn), lambda l, j: (l, 0, j)),
        ],
        out_specs=pl.BlockSpec((1, SUBLANES, tn), lambda l, j: (l, 0, j)),
        compiler_params=_cp(("arbitrary", "arbitrary")),
        name="modulation",
    )(cc, w_mod, b_mod.reshape(depth, 1, 6 * D_MODEL))


def _inproj_kernel(x_ref, mod_ref, w_ref, o_ref):
    md = mod_ref[0]
    u = x_ref[...] * (1.0 + md[1:2]) + md[0:1]
    o_ref[...] = jnp.dot(u.astype(BF16), w_ref[...], preferred_element_type=F32).astype(BF16)


def _inproj(x, modv, w_bf16, n_lat_blocks):
    t = x.shape[0]
    return pl.pallas_call(
        _inproj_kernel,
        out_shape=jax.ShapeDtypeStruct((t, IN_COLS), BF16),
        grid=(IN_COLS // TN_IN, t // TM),
        in_specs=[
            pl.BlockSpec((TM, D_MODEL), lambda j, i: (i, 0)),
            pl.BlockSpec((1, 6, D_MODEL), lambda j, i: (i // n_lat_blocks, 0, 0)),
            pl.BlockSpec((D_MODEL, TN_IN), lambda j, i: (0, j)),
        ],
        out_specs=pl.BlockSpec((TM, TN_IN), lambda j, i: (i, j)),
        compiler_params=_cp(("arbitrary", "arbitrary")),
        name="inproj",
    )(x, modv, w_bf16)


def _scan_block(s, nb, reverse):
    return (nb - s) if reverse else (s + nb) % (nb + 1)


def _lru_kernel(pf_ref, cf_ref, nf_ref, pb_ref, cb_ref, nb_ref, cw_ref, cbias_ref, wr_ref, br_ref, wi_ref,
                bi_ref, sp_ref, of_ref, ob_ref, a_s, b_s, h_s, *, nb):
    s = pl.program_id(0)

    @pl.when(s == 0)
    def _():
        h_s[...] = jnp.zeros_like(h_s)

    cw = cw_ref[...]
    for d, (prev_ref, cur_ref, next_ref) in enumerate(((pf_ref, cf_ref, nf_ref), (pb_ref, cb_ref, nb_ref))):
        blk = _scan_block(s, nb, d == 1)
        has_prev = jnp.logical_and(blk != 0, blk != nb)
        has_next = jnp.logical_and(blk != nb - 1, blk != nb)
        prev = jnp.where(has_prev, prev_ref[...].astype(F32), 0.0)
        nxt = jnp.where(has_next, next_ref[...].astype(F32), 0.0)
        ext = jnp.concatenate([prev, cur_ref[...].astype(F32), nxt], axis=0)
        xc = cbias_ref[...] + jnp.zeros((TM, LRU_WIDTH), F32)
        for k in range(CONV_W):
            off = HALO - CONV_W // 2 + k
            xc = xc + ext[off:off + TM] * cw[k:k + 1]
        xb = xc.astype(BF16)
        r = _sigmoid(jnp.dot(xb, wr_ref[d], preferred_element_type=F32) + br_ref[d])
        i = _sigmoid(jnp.dot(xb, wi_ref[d], preferred_element_type=F32) + bi_ref[d])
        a = jnp.exp(-LRU_C * r * sp_ref[d])
        a_s[d] = a
        b_s[d] = jnp.sqrt(1.0 - a * a) * (i * xc)

    def step(n, carry):
        hf, hb = carry
        tb = TM - 1 - n
        hf = a_s[0, pl.ds(n, 1), :] * hf + b_s[0, pl.ds(n, 1), :]
        of_ref[pl.ds(n, 1), :] = hf
        hb = a_s[1, pl.ds(tb, 1), :] * hb + b_s[1, pl.ds(tb, 1), :]
        ob_ref[pl.ds(tb, 1), :] = hb
        return hf, hb

    hf, hb = lax.fori_loop(0, TM, step, (h_s[0:1, :], h_s[1:2, :]), unroll=8)
    h_s[0:1, :] = hf
    h_s[1:2, :] = hb


def _lru(hmat, nb, conv_w, conv_b, wr, br, wi, bi, sp):
    t = hmat.shape[0]
    rh = TM // HALO
    last_h = t // HALO - 1

    def specs(reverse):
        def bmap(s):
            return _scan_block(s, nb, reverse)
        return [
            pl.BlockSpec((HALO, LRU_WIDTH), lambda s: (jnp.maximum(bmap(s) * rh - 1, 0), 0)),
            pl.BlockSpec((TM, LRU_WIDTH), lambda s: (bmap(s), 0)),
            pl.BlockSpec((HALO, LRU_WIDTH), lambda s: (jnp.minimum((bmap(s) + 1) * rh, last_h), 0)),
        ], pl.BlockSpec((TM, LRU_WIDTH), lambda s: (bmap(s), 0))

    in_f, out_f = specs(False)
    in_b, out_b = specs(True)
    vec = lambda: pl.BlockSpec((2, 1, LRU_WIDTH), lambda s: (0, 0, 0))
    mat = lambda: pl.BlockSpec((2, LRU_WIDTH, LRU_WIDTH), lambda s: (0, 0, 0))
    sds = jax.ShapeDtypeStruct((t, LRU_WIDTH), F32)
    return pl.pallas_call(
        functools.partial(_lru_kernel, nb=nb),
        out_shape=(sds, sds),
        grid=(nb + 1,),
        in_specs=in_f + in_b + [
            pl.BlockSpec((CONV_W, LRU_WIDTH), lambda s: (0, 0)),
            pl.BlockSpec((1, LRU_WIDTH), lambda s: (0, 0)),
            mat(), vec(), mat(), vec(), vec(),
        ],
        out_specs=(out_f, out_b),
        scratch_shapes=[pltpu.VMEM((2, TM, LRU_WIDTH), F32), pltpu.VMEM((2, TM, LRU_WIDTH), F32),
                        pltpu.VMEM((SUBLANES, LRU_WIDTH), F32)],
        compiler_params=_cp(("arbitrary",)),
        name="lru",
    )(hmat, hmat, hmat, hmat, hmat, hmat, conv_w, conv_b, wr, br, wi, bi, sp)


def _cmul(ar, ai, br, bi):
    return ar * br - ai * bi, ar * bi + ai * br


def _s5_kernel(u_ref, p_ref, pt_ref, bm_ref, cm_ref, a_ref, pw_ref, o_ref, x_s, st_s, c_s):
    s = pl.program_id(0)
    n = S5_N

    @pl.when(s == 0)
    def _():
        c_s[...] = jnp.zeros_like(c_s)

    up = jnp.dot(p_ref[...], u_ref[...], preferred_element_type=F32).astype(BF16)
    nc = n // S5_CHUNKS
    for c in range(S5_CHUNKS):
        xc = jnp.dot(up[:, c * LANES:(c + 1) * LANES], bm_ref[c], preferred_element_type=F32)
        x_s[:, c * nc:(c + 1) * nc] = xc[:, 0:nc]
        x_s[:, n + c * nc:n + (c + 1) * nc] = xc[:, nc:2 * nc]
    ns = n // S5_SCAN_CHUNKS
    for c in range(S5_SCAN_CHUNKS):
        re = slice(c * ns, (c + 1) * ns)
        im = slice(n + c * ns, n + (c + 1) * ns)
        ar = jnp.broadcast_to(a_ref[0:1, re], (SUBLANES, ns))
        ai = jnp.broadcast_to(a_ref[1:2, re], (SUBLANES, ns))

        def local(m, carry, re=re, im=im, ar=ar, ai=ai):
            hr, hi = carry
            row = pl.multiple_of(m * SUBLANES, SUBLANES)
            pr, pi = _cmul(ar, ai, hr, hi)
            hr = pr + x_s[pl.ds(row, SUBLANES), re]
            hi = pi + x_s[pl.ds(row, SUBLANES), im]
            x_s[pl.ds(row, SUBLANES), re] = hr
            x_s[pl.ds(row, SUBLANES), im] = hi
            return hr, hi

        z = jnp.zeros((SUBLANES, ns), F32)
        fr, fi = lax.fori_loop(0, SEG, local, (z, z), unroll=4)

        qr = pw_ref[TM - 1:TM, re]
        qi = pw_ref[TM - 1:TM, im]
        sr = c_s[0:1, re]
        si = c_s[0:1, im]
        for j in range(SUBLANES):
            st_s[j:j + 1, re] = sr
            st_s[j:j + 1, im] = si
            pr, pi = _cmul(qr, qi, sr, si)
            sr = pr + fr[j:j + 1]
            si = pi + fi[j:j + 1]
        c_s[0:1, re] = sr
        c_s[0:1, im] = si

        str_ = st_s[:, re]
        sti = st_s[:, im]

        def fix(m, _, re=re, im=im, str_=str_, sti=sti):
            row = pl.multiple_of(m * SUBLANES, SUBLANES)
            pr, pi = _cmul(pw_ref[pl.ds(row, SUBLANES), re], pw_ref[pl.ds(row, SUBLANES), im], str_, sti)
            x_s[pl.ds(row, SUBLANES), re] = x_s[pl.ds(row, SUBLANES), re] + pr
            x_s[pl.ds(row, SUBLANES), im] = x_s[pl.ds(row, SUBLANES), im] + pi
            return 0

        lax.fori_loop(0, SEG, fix, 0, unroll=2)

    yp = jnp.concatenate(
        [jnp.dot(x_s[:, c * nc:(c + 1) * nc].astype(BF16), cm_ref[0, c], preferred_element_type=F32)
         + jnp.dot(x_s[:, n + c * nc:n + (c + 1) * nc].astype(BF16), cm_ref[1, c], preferred_element_type=F32)
         for c in range(S5_CHUNKS)], axis=1)
    y_hi = yp.astype(BF16)
    y_lo = (yp - y_hi.astype(F32)).astype(BF16)
    pt = pt_ref[...]
    o_ref[...] = (jnp.dot(pt, y_hi, preferred_element_type=F32)
                  + jnp.dot(pt, y_lo, preferred_element_type=F32))


def _s5_perm(reverse):
    p = np.zeros((TM, TM), np.float32)
    for m in range(SEG):
        for j in range(SUBLANES):
            pos = j * SEG + m
            p[m * SUBLANES + j, (TM - 1 - pos) if reverse else pos] = 1.0
    return p


def _s5(hmat, nb, reverse, bm_t, cm_t, a_row, pw):
    t = hmat.shape[0]
    p = _s5_perm(reverse)
    n2 = 2 * S5_N

    def bmap(s):
        return _scan_block(s, nb, reverse)

    return pl.pallas_call(
        _s5_kernel,
        out_shape=jax.ShapeDtypeStruct((t, S5_WIDTH), F32),
        grid=(nb + 1,),
        in_specs=[
            pl.BlockSpec((TM, S5_WIDTH), lambda s: (bmap(s), 2)),
            pl.BlockSpec((TM, TM), lambda s: (0, 0)),
            pl.BlockSpec((TM, TM), lambda s: (0, 0)),
            pl.BlockSpec((S5_CHUNKS, LANES, n2 // S5_CHUNKS), lambda s: (0, 0, 0)),
            pl.BlockSpec((2, S5_CHUNKS, S5_N // S5_CHUNKS, LANES), lambda s: (0, 0, 0, 0)),
            pl.BlockSpec((2, S5_N), lambda s: (0, 0)),
            pl.BlockSpec((TM, n2), lambda s: (0, 0)),
        ],
        out_specs=pl.BlockSpec((TM, S5_WIDTH), lambda s: (bmap(s), 0)),
        scratch_shapes=[pltpu.VMEM((TM, n2), F32), pltpu.VMEM((SUBLANES, n2), F32),
                        pltpu.VMEM((1, n2), F32)],
        compiler_params=_cp(("arbitrary",)),
        name="s5_bwd" if reverse else "s5_fwd",
    )(hmat, jnp.asarray(p, BF16), jnp.asarray(p.T, BF16), bm_t, cm_t, a_row, pw)


def _s5_params(a_re, a_im, log_dt, b_re, b_im, c_re, c_im):
    g, p = a_re.shape
    dt = jnp.exp(log_dt)[:, None]
    lr, li = a_re, a_im
    ea = jnp.exp(lr * dt)
    ab_r, ab_i = ea * jnp.cos(li * dt), ea * jnp.sin(li * dt)
    den = lr * lr + li * li
    co_r = ((ab_r - 1.0) * lr + ab_i * li) / den
    co_i = (ab_i * lr - (ab_r - 1.0) * li) / den
    bb_r = co_r[..., None] * b_re - co_i[..., None] * b_im
    bb_i = co_r[..., None] * b_im + co_i[..., None] * b_re
    gc = g // S5_CHUNKS
    eye = jnp.eye(gc, dtype=F32)

    def chunked(w, spec):
        return jnp.einsum(spec, w.reshape((S5_CHUNKS, gc) + w.shape[1:]), eye)

    bm_r = chunked(bb_r, 'kgpc,gh->kgchp').reshape(S5_CHUNKS, gc * S5_GROUP, gc * p)
    bm_i = chunked(bb_i, 'kgpc,gh->kgchp').reshape(S5_CHUNKS, gc * S5_GROUP, gc * p)
    bm_t = jnp.concatenate([bm_r, bm_i], axis=2).astype(BF16)
    cm_r = chunked(c_re, 'kgcp,gh->kgphc').reshape(S5_CHUNKS, gc * p, gc * S5_GROUP)
    cm_i = chunked(-c_im, 'kgcp,gh->kgphc').reshape(S5_CHUNKS, gc * p, gc * S5_GROUP)
    cm_t = jnp.stack([cm_r, cm_i], axis=0).astype(BF16)
    a_row = jnp.stack([ab_r.reshape(-1), ab_i.reshape(-1)], axis=0)
    k = jnp.arange(1, SEG + 1, dtype=F32)[:, None, None]
    ek = jnp.exp(k * (lr * dt))
    pw = jnp.concatenate([(ek * jnp.cos(k * (li * dt))).reshape(SEG, -1),
                          (ek * jnp.sin(k * (li * dt))).reshape(SEG, -1)], axis=1)
    return bm_t, cm_t, a_row, jnp.repeat(pw, SUBLANES, axis=0)


def _qkv_kernel(q_ref, k_ref, v_ref, cos_ref, sin_ref, qa_ref, qb_ref, ko_ref, vo_ref):
    w = DA_HEADS * 2 * DA_HEAD_DIM
    cos = jnp.concatenate([cos_ref[...]] * (w // LANES), axis=1)
    sin = jnp.concatenate([sin_ref[...]] * (w // LANES), axis=1)
    lane = lax.broadcasted_iota(jnp.int32, (TM, w), 1)
    first = (lane % (2 * ROPE_FREQS)) < ROPE_FREQS

    def rope(x):
        swapped = jnp.where(first, pltpu.roll(x, w - ROPE_FREQS, 1), pltpu.roll(x, ROPE_FREQS, 1))
        return x * cos + swapped * sin

    q = rope(q_ref[...].astype(F32)) * (DA_SCALE * LOG2_E)
    in_a = (lane % (2 * DA_HEAD_DIM)) < DA_HEAD_DIM
    qa_ref[...] = jnp.where(in_a, q, 0.0).astype(BF16)
    qb_ref[...] = jnp.where(in_a, 0.0, q).astype(BF16)
    ko_ref[...] = rope(k_ref[...].astype(F32)).astype(BF16)
    v = v_ref[...]
    ones = jnp.ones((TM, DA_V_DIM), BF16)
    vo_ref[...] = jnp.concatenate(
        [blk for h in range(DA_HEADS) for blk in (v[:, h * DA_V_DIM:(h + 1) * DA_V_DIM], ones)], axis=1)


def _qkv_prep(hmat, cos_t, sin_t):
    t = hmat.shape[0]
    w = DA_HEADS * 2 * DA_HEAD_DIM
    col = lambda c: pl.BlockSpec((TM, w), lambda i: (i, c))
    tab = lambda: pl.BlockSpec((TM, LANES), lambda i: (i, 0))
    out = lambda: pl.BlockSpec((TM, w), lambda i: (i, 0))
    sds = jax.ShapeDtypeStruct((t, w), BF16)
    return pl.pallas_call(
        _qkv_kernel,
        out_shape=(sds, sds, sds, jax.ShapeDtypeStruct((t, 2 * w), BF16)),
        grid=(t // TM,),
        in_specs=[col(3), col(4), col(5), tab(), tab()],
        out_specs=(out(), out(), out(), pl.BlockSpec((TM, 2 * w), lambda i: (i, 0))),
        compiler_params=_cp(("arbitrary",)),
        name="qkv_prep",
    )(hmat, hmat, hmat, cos_t, sin_t)


def _rope_tables(n_lat, n_ctx):
    n_rows = n_lat // GRID_W
    freqs = ROPE_BASE ** (-jnp.arange(ROPE_FREQS, dtype=F32) / ROPE_FREQS)
    ang_r = jnp.arange(n_rows).astype(F32)[:, None] * freqs
    ang_c = jnp.arange(GRID_W).astype(F32)[:, None] * freqs
    per_row = lambda a: jnp.repeat(a, GRID_W, axis=0)
    per_col = lambda a: jnp.tile(a, (n_rows, 1))
    cr, sr = per_row(jnp.cos(ang_r)), per_row(jnp.sin(ang_r))
    cc, sc = per_col(jnp.cos(ang_c)), per_col(jnp.sin(ang_c))
    cos64 = jnp.concatenate([cr, cr, cc, cc], axis=1)
    sin64 = jnp.concatenate([-sr, sr, -sc, sc], axis=1)
    cos_t = jnp.concatenate([cos64, cos64], axis=1)
    sin_t = jnp.concatenate([sin64, sin64], axis=1)
    cos_t = jnp.concatenate([cos_t, jnp.ones((n_ctx, LANES), F32)], axis=0)
    sin_t = jnp.concatenate([sin_t, jnp.zeros((n_ctx, LANES), F32)], axis=0)
    return cos_t, sin_t


def _attn_kernel(lam_ref, g_ref, qa_ref, qb_ref, k_ref, v_ref, prev_ref, o_ref, m_s, acc_s, a_s, s_s,
                 *, lam_init, nk):
    del prev_ref
    g = pl.program_id(1)
    kj = g % nk

    @pl.when(g == 0)
    def _():
        s_s[...] = jnp.zeros_like(s_s)
        acc_s[...] = jnp.zeros_like(acc_s)
        a_s[...] = jnp.zeros_like(a_s)
        m_s[...] = jnp.zeros_like(m_s)

    def step(cur):
        k = k_ref[...]
        v = v_ref[...]
        for i, q_ref in enumerate((qa_ref, qb_ref)):
            m_old = m_s[i]
            p = jnp.exp2(s_s[1 - cur, i] - m_old[:, 0:1]).astype(BF16)
            alpha = a_s[i]
            acc_s[i] = (jnp.concatenate([alpha, alpha], axis=1) * acc_s[i]
                        + jnp.dot(p, v, preferred_element_type=F32))
            s = lax.dot_general(q_ref[...], k, (((1,), (1,)), ((), ())), preferred_element_type=F32)
            s_s[cur, i] = s
            m_base = jnp.where(kj == 0, -jnp.inf, m_old)
            m_new = jnp.maximum(m_base, jnp.max(s, axis=1, keepdims=True))
            a_s[i] = jnp.exp2(m_base - m_new)
            m_s[i] = m_new

    def finish():
        lv = lam_ref[...]
        lam = (jnp.exp(jnp.sum(lv[0:1] * lv[1:2], axis=1, keepdims=True))
               - jnp.exp(jnp.sum(lv[2:3] * lv[3:4], axis=1, keepdims=True)) + lam_init)
        a0, a1 = acc_s[0], acc_s[1]
        o = (a0[:, 0:DA_V_DIM] / a0[:, DA_V_DIM:2 * DA_V_DIM]
             - lam * (a1[:, 0:DA_V_DIM] / a1[:, DA_V_DIM:2 * DA_V_DIM]))
        ms = jnp.mean(o * o, axis=1, keepdims=True)
        o_ref[...] = o * lax.rsqrt(ms + RMS_EPS) * g_ref[...] * (1.0 - lam_init)

    for parity in range(2):
        pl.when(g % 2 == parity)(functools.partial(step, parity))
    pl.when(jnp.logical_and(kj == 0, g > 0))(finish)


def _attention(qa, qb, k, v, lam_vec, subln_g, lam_init, prev, *, q0, nq, tq, k0, nk, tk, name):
    t = qa.shape[0]
    n_pairs = nq * nk

    def qmap(h, g):
        return (q0 + jnp.minimum(g // nk, nq - 1), h)

    return pl.pallas_call(
        functools.partial(_attn_kernel, lam_init=lam_init, nk=nk),
        out_shape=jax.ShapeDtypeStruct((t, DA_HEADS * DA_V_DIM), F32),
        grid=(DA_HEADS, n_pairs + 1),
        in_specs=[
            pl.BlockSpec((4, DA_HEAD_DIM), lambda h, g: (0, 0)),
            pl.BlockSpec((1, DA_V_DIM), lambda h, g: (0, 0)),
            pl.BlockSpec((tq, LANES), qmap),
            pl.BlockSpec((tq, LANES), qmap),
            pl.BlockSpec((tk, LANES), lambda h, g: (k0 + g % nk, h)),
            pl.BlockSpec((tk, 2 * DA_V_DIM), lambda h, g: (k0 + (g + nk - 1) % nk, h)),
            pl.BlockSpec(memory_space=pl.ANY),
        ],
        out_specs=pl.BlockSpec((tq, LANES), lambda h, g: (q0 + jnp.maximum(g - 1, 0) // nk, h)),
        scratch_shapes=[pltpu.VMEM((2, tq, LANES), F32), pltpu.VMEM((2, tq, 2 * DA_V_DIM), F32),
                        pltpu.VMEM((2, tq, LANES), F32), pltpu.VMEM((2, 2, tq, tk), F32)],
        input_output_aliases={6: 0},
        compiler_params=_cp(("arbitrary", "arbitrary")),
        name=name,
    )(lam_vec, subln_g, qa, qb, k, v, prev)


def _merge_kernel(x_ref, mod_ref, gt_ref, ga_ref, ub_ref, hf_ref, hb_ref, sf_ref, sb_ref, yc_ref,
                  sd_ref, wglu_ref, bglu_ref, wp_ref, bg_ref, wo_ref, lng_ref, lnb_ref,
                  x1_ref, v_ref, *, dn_alpha):
    md = mod_ref[0]
    ya = (hf_ref[...] + hb_ref[...]) * _gelu_tanh(ga_ref[...].astype(F32))
    y = sf_ref[...] + sb_ref[...] + sd_ref[...] * ub_ref[...].astype(F32)
    tg = jnp.dot(_gelu_tanh(y).astype(BF16), wglu_ref[...], preferred_element_type=F32) + bglu_ref[...]
    yb = tg[:, 0:S5_WIDTH] * _sigmoid(tg[:, S5_WIDTH:2 * S5_WIDTH])
    yc = yc_ref[...]
    bg = bg_ref[...]
    z = jnp.zeros((TM, D_MODEL), F32)
    for n, br in enumerate((ya, yb, yc)):
        gate = _sigmoid(gt_ref[:, n * D_MODEL:(n + 1) * D_MODEL].astype(F32) + bg[n:n + 1])
        z = z + gate * jnp.dot(br.astype(BF16), wp_ref[n], preferred_element_type=F32)
    m = jnp.dot(z.astype(BF16), wo_ref[...], preferred_element_type=F32)
    x1 = _layer_norm(dn_alpha * x_ref[...] + md[2:3] * m, lng_ref[...], lnb_ref[...])
    x1_ref[...] = x1
    v_ref[...] = x1 * (1.0 + md[4:5]) + md[3:4]


def _merge(x, modv, hmat, hf, hb, sf, sb, yc, s5_d, w_glu, b_glu, w_proj, b_gate, w_out, ln_g, ln_b,
           n_lat_blocks, dn_alpha):
    t = x.shape[0]
    row = lambda w, c=0: pl.BlockSpec((TM, w), lambda i: (i, c))
    const = lambda *shape: pl.BlockSpec(shape, lambda i: (0,) * len(shape))
    sds = jax.ShapeDtypeStruct((t, D_MODEL), F32)
    return pl.pallas_call(
        functools.partial(_merge_kernel, dn_alpha=dn_alpha),
        out_shape=(sds, sds),
        grid=(t // TM,),
        in_specs=[
            row(D_MODEL),
            pl.BlockSpec((1, 6, D_MODEL), lambda i: (i // n_lat_blocks, 0, 0)),
            row(N_BRANCH * D_MODEL, 1), row(BW, 1), row(BW, 2),
            row(BW), row(BW), row(BW), row(BW), row(BW),
            const(1, S5_WIDTH), const(S5_WIDTH, 2 * S5_WIDTH), const(1, 2 * S5_WIDTH),
            const(N_BRANCH, BW, D_MODEL), const(N_BRANCH, D_MODEL), const(D_MODEL, D_MODEL),
            const(1, D_MODEL), const(1, D_MODEL),
        ],
        out_specs=(row(D_MODEL), row(D_MODEL)),
        compiler_params=_cp(("arbitrary",)),
        name="merge",
    )(x, modv, hmat, hmat, hmat, hf, hb, sf, sb, yc, s5_d, w_glu, b_glu, w_proj, b_gate, w_out,
      ln_g, ln_b)


def _router_kernel(v_ref, w_ref, b_ref, tri_ref, ti_ref, gw_ref, rk_ref, cnt_ref, c_s):
    s = pl.program_id(0)

    @pl.when(s == 0)
    def _():
        c_s[...] = jnp.zeros_like(c_s)

    logits = jnp.dot(v_ref[...], w_ref[...], preferred_element_type=F32, precision=HIGHEST) + b_ref[...]
    lane = lax.broadcasted_iota(jnp.int32, (TM, N_EXPERTS), 1).astype(F32)
    work = logits
    vals, idxs = [], []
    hot = jnp.zeros((TM, N_EXPERTS), F32)
    for _ in range(TOP_K):
        mx = jnp.max(work, axis=1, keepdims=True)
        ix = jnp.min(jnp.where(work == mx, lane, float(N_EXPERTS)), axis=1, keepdims=True)
        sel = lane == ix
        hot = jnp.where(sel, 1.0, hot)
        work = jnp.where(sel, -jnp.inf, work)
        vals.append(mx)
        idxs.append(ix)
    es = [jnp.exp(vv - vals[0]) for vv in vals]
    den = es[0] + es[1] + es[2] + es[3]
    before = jnp.dot(tri_ref[...], hot.astype(BF16), preferred_element_type=F32) + c_s[0:1, :]
    col = lax.broadcasted_iota(jnp.int32, (TM, TOP_K), 1)
    ti = jnp.zeros((TM, TOP_K), F32)
    gw = jnp.zeros((TM, TOP_K), F32)
    rk = jnp.zeros((TM, TOP_K), F32)
    for kk in range(TOP_K):
        rank = jnp.sum(jnp.where(lane == idxs[kk], before, 0.0), axis=1, keepdims=True)
        ti = jnp.where(col == kk, idxs[kk], ti)
        gw = jnp.where(col == kk, es[kk] / den, gw)
        rk = jnp.where(col == kk, rank, rk)
    ti_ref[...] = ti.astype(jnp.int32)
    gw_ref[...] = gw
    rk_ref[...] = rk.astype(jnp.int32)
    tot = c_s[0:1, :] + jnp.sum(hot, axis=0, keepdims=True)
    c_s[...] = jnp.broadcast_to(tot, c_s.shape)
    cnt_ref[...] = jnp.broadcast_to(tot, cnt_ref.shape).astype(jnp.int32)


def _router(v, w_router, b_router):
    t = v.shape[0]
    tri = jnp.asarray(np.tril(np.ones((TM, TM), np.float32), -1), BF16)
    row4 = lambda: pl.BlockSpec((TM, TOP_K), lambda i: (i, 0))
    return pl.pallas_call(
        _router_kernel,
        out_shape=(jax.ShapeDtypeStruct((t, TOP_K), jnp.int32),
                   jax.ShapeDtypeStruct((t, TOP_K), F32),
                   jax.ShapeDtypeStruct((t, TOP_K), jnp.int32),
                   jax.ShapeDtypeStruct((SUBLANES, N_EXPERTS), jnp.int32)),
        grid=(t // TM,),
        in_specs=[
            pl.BlockSpec((TM, D_MODEL), lambda i: (i, 0)),
            pl.BlockSpec((D_MODEL, N_EXPERTS), lambda i: (0, 0)),
            pl.BlockSpec((1, N_EXPERTS), lambda i: (0, 0)),
            pl.BlockSpec((TM, TM), lambda i: (0, 0)),
        ],
        out_specs=(row4(), row4(), row4(), pl.BlockSpec((SUBLANES, N_EXPERTS), lambda i: (0, 0))),
        scratch_shapes=[pltpu.VMEM((SUBLANES, N_EXPERTS), F32)],
        compiler_params=_cp(("arbitrary",)),
        name="router",
    )(v, w_router, b_router, tri)


def _dispatch_kernel(dest_ref, v_ref, xs_in, xs_hbm, sem):
    del xs_in

    def row_copy(r, kk):
        return pltpu.make_async_copy(v_ref.at[pl.ds(r, 1)],
                                     xs_hbm.at[pl.ds(dest_ref[r * TOP_K + kk], 1)], sem)

    def start(r, _):
        for kk in range(TOP_K):
            row_copy(r, kk).start(priority=kk % 2)
        return 0

    def wait(r, _):
        for kk in range(TOP_K):
            row_copy(r, kk).wait()
        return 0

    lax.fori_loop(0, TM, start, 0, unroll=8)
    lax.fori_loop(0, TM, wait, 0, unroll=8)


def _dispatch(dest_flat, v, n_slots):
    t = v.shape[0]
    xs0 = jnp.zeros((n_slots, D_MODEL), F32)
    return pl.pallas_call(
        _dispatch_kernel,
        out_shape=jax.ShapeDtypeStruct((n_slots, D_MODEL), F32),
        grid=(t // TM,),
        in_specs=[
            pl.BlockSpec((TM * TOP_K,), lambda i: (i,), memory_space=pltpu.SMEM),
            pl.BlockSpec((TM, D_MODEL), lambda i: (i, 0)),
            pl.BlockSpec(memory_space=pl.ANY),
        ],
        out_specs=pl.BlockSpec(memory_space=pl.ANY),
        scratch_shapes=[pltpu.SemaphoreType.DMA],
        input_output_aliases={2: 0},
        compiler_params=_cp(("arbitrary",)),
        name="moe_dispatch",
    )(dest_flat, v, xs0)


def _expert_kernel(be_ref, nu_ref, x_ref, wgu_ref, bgu_ref, wd_ref, bd_ref, o_ref, wgu_s, wd_s):
    b = pl.program_id(0)
    new_expert = jnp.logical_or(b == 0, be_ref[b] != be_ref[jnp.maximum(b - 1, 0)])

    @pl.when(jnp.logical_and(new_expert, b < nu_ref[0]))
    def _():
        wgu_s[...] = wgu_ref[0, 0].astype(BF16)
        wd_s[...] = wd_ref[0, 0].astype(BF16)

    @pl.when(b < nu_ref[0])
    def _():
        for r in range(0, MOE_BLK, MOE_SUB):
            rows = slice(r, r + MOE_SUB)
            h = jnp.dot(x_ref[rows, :].astype(BF16), wgu_s[...], preferred_element_type=F32) + bgu_ref[0]
            hg = jnp.minimum(h[:, 0:D_FF], SWIGLU_LIMIT)
            hl = jnp.clip(h[:, D_FF:2 * D_FF], -SWIGLU_LIMIT, SWIGLU_LIMIT)
            act = hg * _sigmoid(SWIGLU_ALPHA * hg) * (hl + 1.0)
            o_ref[rows, :] = jnp.dot(act.astype(BF16), wd_s[...], preferred_element_type=F32) + bd_ref[0]

    @pl.when(pl.program_id(0) >= nu_ref[0])
    def _():
        o_ref[...] = jnp.zeros_like(o_ref)


def _experts(blk_e, n_used, xs, layer, w_gu, b_gu, w_down, b_down):
    n_slots = xs.shape[0]
    nblk = n_slots // MOE_BLK
    grid_spec = pltpu.PrefetchScalarGridSpec(
        num_scalar_prefetch=2,
        grid=(nblk,),
        in_specs=[
            pl.BlockSpec((MOE_BLK, D_MODEL), lambda b, be, nu: (jnp.minimum(b, nu[0] - 1), 0)),
            pl.BlockSpec((1, 1, D_MODEL, 2 * D_FF), lambda b, be, nu: (layer, be[b], 0, 0)),
            pl.BlockSpec((1, 1, 2 * D_FF), lambda b, be, nu: (be[b], 0, 0)),
            pl.BlockSpec((1, 1, D_FF, D_MODEL), lambda b, be, nu: (layer, be[b], 0, 0)),
            pl.BlockSpec((1, 1, D_MODEL), lambda b, be, nu: (be[b], 0, 0)),
        ],
        out_specs=pl.BlockSpec((MOE_BLK, D_MODEL), lambda b, be, nu: (b, 0)),
        scratch_shapes=[pltpu.VMEM((D_MODEL, 2 * D_FF), BF16), pltpu.VMEM((D_FF, D_MODEL), BF16)],
    )
    return pl.pallas_call(
        _expert_kernel,
        out_shape=jax.ShapeDtypeStruct((n_slots, D_MODEL), F32),
        grid_spec=grid_spec,
        compiler_params=_cp(("arbitrary",)),
        name="moe_experts",
    )(blk_e, n_used, xs, w_gu, b_gu.reshape(N_EXPERTS, 1, 2 * D_FF), w_down,
      b_down.reshape(N_EXPERTS, 1, D_MODEL))


def _combine_kernel(dest_ref, ys_hbm, gw_ref, x_ref, mod_ref, lng_ref, lnb_ref, o_ref, g_s, sem,
                    *, dn_alpha):
    def row_copy(r, kk):
        return pltpu.make_async_copy(ys_hbm.at[pl.ds(dest_ref[r * TOP_K + kk], 1)],
                                     g_s.at[kk, pl.ds(r, 1)], sem)

    def start(r, _):
        for kk in range(TOP_K):
            row_copy(r, kk).start(priority=kk % 2)
        return 0

    def wait(r, _):
        for kk in range(TOP_K):
            row_copy(r, kk).wait()
        return 0

    lax.fori_loop(0, TM, start, 0, unroll=8)
    lax.fori_loop(0, TM, wait, 0, unroll=8)
    gw = gw_ref[...]
    f = jnp.zeros((TM, D_MODEL), F32)
    for kk in range(TOP_K):
        f = f + gw[:, kk:kk + 1] * g_s[kk]
    md = mod_ref[0]
    o_ref[...] = _layer_norm(dn_alpha * x_ref[...] + md[5:6] * f, lng_ref[...], lnb_ref[...])


def _combine(dest_flat, ys, gate_w, x1, modv, ln_g, ln_b, n_lat_blocks, dn_alpha):
    t = x1.shape[0]
    return pl.pallas_call(
        functools.partial(_combine_kernel, dn_alpha=dn_alpha),
        out_shape=jax.ShapeDtypeStruct((t, D_MODEL), F32),
        grid=(t // TM,),
        in_specs=[
            pl.BlockSpec((TM * TOP_K,), lambda i: (i,), memory_space=pltpu.SMEM),
            pl.BlockSpec(memory_space=pl.ANY),
            pl.BlockSpec((TM, TOP_K), lambda i: (i, 0)),
            pl.BlockSpec((TM, D_MODEL), lambda i: (i, 0)),
            pl.BlockSpec((1, 6, D_MODEL), lambda i: (i // n_lat_blocks, 0, 0)),
            pl.BlockSpec((1, D_MODEL), lambda i: (0, 0)),
            pl.BlockSpec((1, D_MODEL), lambda i: (0, 0)),
        ],
        out_specs=pl.BlockSpec((TM, D_MODEL), lambda i: (i, 0)),
        scratch_shapes=[pltpu.VMEM((TOP_K, TM, D_MODEL), F32), pltpu.SemaphoreType.DMA],
        compiler_params=_cp(("arbitrary",)),
        name="moe_combine",
    )(dest_flat, ys, gate_w, x1, modv, ln_g, ln_b)


def _moe(v, x1, modv, layer, w_router, b_router, w_gu, b_gu, w_down, b_down, ln_g, ln_b, n_lat_blocks,
         dn_alpha):
    t = v.shape[0]
    top_i, gate_w, rank, counts = _router(v, w_router, b_router.reshape(1, N_EXPERTS))
    counts = counts[0]
    padded = (counts + MOE_BLK - 1) // MOE_BLK * MOE_BLK
    pend = jnp.cumsum(padded)
    pstart = pend - padded
    dest = (pstart[top_i] + rank).reshape(-1).astype(jnp.int32)
    n_slots = t * TOP_K + N_EXPERTS * MOE_BLK
    nblk = n_slots // MOE_BLK
    starts = jnp.arange(nblk, dtype=jnp.int32) * MOE_BLK
    blk_e = jnp.minimum(jnp.sum((pend[None, :] <= starts[:, None]).astype(jnp.int32), axis=1),
                        N_EXPERTS - 1).astype(jnp.int32)
    n_used = (pend[-1:] // MOE_BLK).astype(jnp.int32)
    xs = _dispatch(dest, v, n_slots)
    ys = _experts(blk_e, n_used, xs, layer, w_gu, b_gu, w_down, b_down)
    return _combine(dest, ys, gate_w, x1, modv, ln_g, ln_b, n_lat_blocks, dn_alpha)


def _block_diag(w):
    nbk, bs, _ = w.shape
    eye = jnp.eye(nbk, dtype=w.dtype)
    return jnp.einsum('hij,hg->higj', w, eye).reshape(nbk * bs, nbk * bs)


def _query_block(n_lat):
    for tq in (1024, 512):
        if n_lat % tq == 0:
            return tq
    return TM


def _key_block(t):
    for tk in (1280, 640, 256):
        if t % tk == 0:
            return tk
    raise ValueError("token count must be a multiple of 256")


def kernel(x, c, ctx, c_ctx, w_mod, b_mod, w_in, conv_w, conv_b, lru_wr, lru_br, lru_wi, lru_bi, lru_lam, s5_a_re, s5_a_im, s5_log_dt, s5_b_re, s5_b_im, s5_c_re, s5_c_im, s5_d, s5_w_glu, s5_b_glu, da_lam, da_subln_g, w_proj, b_gate, w_out, ln_g, ln_b, w_router, b_router, w_gu, b_gu, w_down, b_down):
    bsz, n_lat, d = x.shape
    n_ctx = ctx.shape[1]
    depth = w_mod.shape[0]
    assert bsz == 1 and d == D_MODEL and n_ctx == TM and n_lat % TM == 0 and n_lat % GRID_W == 0
    nb = n_lat // TM
    t = n_lat + n_ctx
    dn_alpha = (2 * depth) ** 0.25

    cc = jnp.zeros((SUBLANES, D_MODEL), F32).at[0].set(c[0]).at[1].set(c_ctx)
    mod = _modulation(cc, w_mod, b_mod)
    cos_t, sin_t = _rope_tables(n_lat, n_ctx)
    tq = _query_block(n_lat)
    tk = _key_block(t)

    xs = jnp.concatenate([x[0], ctx[0]], axis=0)
    for l in range(depth):
        modv = mod[l, 0:2].reshape(2, 6, D_MODEL)
        hmat = _inproj(xs, modv, w_in[l].astype(BF16), nb)

        sp = jax.nn.softplus(-lru_lam[l])
        hdir = _lru(hmat, nb, conv_w[l], conv_b[l].reshape(1, -1),
                    jnp.stack([_block_diag(lru_wr[l, dr]) for dr in range(2)]).astype(BF16),
                    lru_br[l].reshape(2, 1, -1),
                    jnp.stack([_block_diag(lru_wi[l, dr]) for dr in range(2)]).astype(BF16),
                    lru_bi[l].reshape(2, 1, -1), sp.reshape(2, 1, -1))
        sdir = []
        for dr in range(2):
            prm = _s5_params(s5_a_re[l, dr], s5_a_im[l, dr], s5_log_dt[l, dr], s5_b_re[l, dr],
                             s5_b_im[l, dr], s5_c_re[l, dr], s5_c_im[l, dr])
            sdir.append(_s5(hmat, nb, dr == 1, *prm))

        qa, qb, kk, vv = _qkv_prep(hmat, cos_t, sin_t)
        lam_init = 0.8 - 0.6 * math.exp(-0.3 * l)
        g_row = da_subln_g[l].reshape(1, DA_V_DIM)
        yc = jnp.zeros((t, DA_HEADS * DA_V_DIM), F32)
        yc = _attention(qa, qb, kk, vv, da_lam[l], g_row, lam_init, yc,
                        q0=0, nq=n_lat // tq, tq=tq, k0=0, nk=t // tk, tk=tk, name="attn_lat")
        yc = _attention(qa, qb, kk, vv, da_lam[l], g_row, lam_init, yc,
                        q0=nb, nq=1, tq=TM, k0=nb, nk=1, tk=TM, name="attn_ctx")

        x1, v = _merge(xs, modv, hmat, hdir[0], hdir[1], sdir[0], sdir[1], yc,
                       s5_d[l].reshape(1, -1), s5_w_glu[l].astype(BF16), s5_b_glu[l].reshape(1, -1),
                       w_proj[l].astype(BF16), b_gate[l], w_out[l].astype(BF16),
                       ln_g[l, 0].reshape(1, -1), ln_b[l, 0].reshape(1, -1), nb, dn_alpha)
        xs = _moe(v, x1, modv, l, w_router[l], b_router[l], w_gu, b_gu[l], w_down, b_down[l],
                  ln_g[l, 1].reshape(1, -1), ln_b[l, 1].reshape(1, -1), nb, dn_alpha)
    return xs[:n_lat][None]
```

```python
import functools
import math

import numpy as np
import jax
import jax.numpy as jnp
from jax import lax
from jax.experimental import pallas as pl
from jax.experimental.pallas import tpu as pltpu

F32 = jnp.float32
BF16 = jnp.bfloat16
HIGHEST = lax.Precision.HIGHEST

D_MODEL = 1024
GRID_W = 64
LRU_WIDTH = 512
LRU_BLOCKS = 8
CONV_W = 4
LRU_C = 8.0
S5_WIDTH = 512
S5_GROUP = 16
S5_GROUPS = S5_WIDTH // S5_GROUP
S5_STATE = 64
S5_N = S5_GROUPS * S5_STATE
DA_HEADS = 4
DA_HEAD_DIM = 64
DA_V_DIM = 128
DA_SCALE = DA_HEAD_DIM ** -0.5
ROPE_BASE = 10000.0
ROPE_FREQS = DA_HEAD_DIM // 4
N_BRANCH = 3
BW = 512
IN_COLS = 6 * BW + N_BRANCH * D_MODEL
N_EXPERTS = 32
TOP_K = 4
D_FF = D_MODEL
SWIGLU_ALPHA = 1.702
SWIGLU_LIMIT = 7.0
LN_EPS = 1e-5
RMS_EPS = 1e-6

SUBLANES = 8
LANES = 128
VMEM_LIMIT = 56 * 1024 * 1024

TM = 256
HALO = 16
SEG = TM // SUBLANES
S5_CHUNKS = S5_WIDTH // LANES
S5_SCAN_CHUNKS = 2
LOG2_E = math.log2(math.e)
MOE_BLK = 512
MOE_SUB = 256
TN_IN = 6144


def _cp(sem, vmem=VMEM_LIMIT):
    return pltpu.CompilerParams(dimension_semantics=sem, vmem_limit_bytes=vmem)


def _sigmoid(x):
    return 1.0 / (1.0 + jnp.exp(-x))


def _gelu_tanh(x):
    return 0.5 * x * (1.0 + jnp.tanh(math.sqrt(2.0 / math.pi) * (x + 0.044715 * (x * x * x))))


def _layer_norm(y, g, b):
    mu = jnp.mean(y, axis=-1, keepdims=True)
    yc = y - mu
    var = jnp.mean(yc * yc, axis=-1, keepdims=True)
    return yc * lax.rsqrt(var + LN_EPS) * g + b


def _mod_kernel(c_ref, w_ref, b_ref, o_ref):
    c = c_ref[...]
    s = c * _sigmoid(c)
    o_ref[0] = jnp.dot(s, w_ref[0], preferred_element_type=F32, precision=HIGHEST) + b_ref[0]


def _modulation(cc, w_mod, b_mod):
    depth = w_mod.shape[0]
    tn = 1536
    return pl.pallas_call(
        _mod_kernel,
        out_shape=jax.ShapeDtypeStruct((depth, SUBLANES, 6 * D_MODEL), F32),
        grid=(depth, 6 * D_MODEL // tn),
        in_specs=[
            pl.BlockSpec((SUBLANES, D_MODEL), lambda l, j: (0, 0)),
            pl.BlockSpec((1, D_MODEL, tn), lambda l, j: (l, 0, j)),
            pl.BlockSpec((1, 1, tn), lambda l, j: (l, 0, j)),
        ],
        out_specs=pl.BlockSpec((1, SUBLANES, tn), lambda l, j: (l, 0, j)),
        compiler_params=_cp(("arbitrary", "arbitrary")),
        name="modulation",
    )(cc, w_mod, b_mod.reshape(depth, 1, 6 * D_MODEL))


def _inproj_kernel(x_ref, mod_ref, w_ref, o_ref):
    md = mod_ref[0]
    u = x_ref[...] * (1.0 + md[1:2]) + md[0:1]
    o_ref[...] = jnp.dot(u.astype(BF16), w_ref[...], preferred_element_type=F32).astype(BF16)


def _inproj(x, modv, w_bf16, n_lat_blocks):
    t = x.shape[0]
    return pl.pallas_call(
        _inproj_kernel,
        out_shape=jax.ShapeDtypeStruct((t, IN_COLS), BF16),
        grid=(IN_COLS // TN_IN, t // TM),
        in_specs=[
            pl.BlockSpec((TM, D_MODEL), lambda j, i: (i, 0)),
            pl.BlockSpec((1, 6, D_MODEL), lambda j, i: (i // n_lat_blocks, 0, 0)),
            pl.BlockSpec((D_MODEL, TN_IN), lambda j, i: (0, j)),
        ],
        out_specs=pl.BlockSpec((TM, TN_IN), lambda j, i: (i, j)),
        compiler_params=_cp(("arbitrary", "arbitrary")),
        name="inproj",
    )(x, modv, w_bf16)


def _scan_block(s, nb, reverse):
    return (nb - s) if reverse else (s + nb) % (nb + 1)


def _lru_kernel(pf_ref, cf_ref, nf_ref, pb_ref, cb_ref, nb_ref, cw_ref, cbias_ref, wr_ref, br_ref, wi_ref,
                bi_ref, sp_ref, of_ref, ob_ref, a_s, b_s, h_s, *, nb):
    s = pl.program_id(0)

    @pl.when(s == 0)
    def _():
        h_s[...] = jnp.zeros_like(h_s)

    cw = cw_ref[...]
    for d, (prev_ref, cur_ref, next_ref) in enumerate(((pf_ref, cf_ref, nf_ref), (pb_ref, cb_ref, nb_ref))):
        blk = _scan_block(s, nb, d == 1)
        has_prev = jnp.logical_and(blk != 0, blk != nb)
        has_next = jnp.logical_and(blk != nb - 1, blk != nb)
        prev = jnp.where(has_prev, prev_ref[...].astype(F32), 0.0)
        nxt = jnp.where(has_next, next_ref[...].astype(F32), 0.0)
        ext = jnp.concatenate([prev, cur_ref[...].astype(F32), nxt], axis=0)
        xc = cbias_ref[...] + jnp.zeros((TM, LRU_WIDTH), F32)
        for k in range(CONV_W):
            off = HALO - CONV_W // 2 + k
            xc = xc + ext[off:off + TM] * cw[k:k + 1]
        xb = xc.astype(BF16)
        r = _sigmoid(jnp.dot(xb, wr_ref[d], preferred_element_type=F32) + br_ref[d])
        i = _sigmoid(jnp.dot(xb, wi_ref[d], preferred_element_type=F32) + bi_ref[d])
        a = jnp.exp(-LRU_C * r * sp_ref[d])
        a_s[d] = a
        b_s[d] = jnp.sqrt(1.0 - a * a) * (i * xc)

    def step(n, carry):
        hf, hb = carry
        tb = TM - 1 - n
        hf = a_s[0, pl.ds(n, 1), :] * hf + b_s[0, pl.ds(n, 1), :]
        of_ref[pl.ds(n, 1), :] = hf
        hb = a_s[1, pl.ds(tb, 1), :] * hb + b_s[1, pl.ds(tb, 1), :]
        ob_ref[pl.ds(tb, 1), :] = hb
        return hf, hb

    hf, hb = lax.fori_loop(0, TM, step, (h_s[0:1, :], h_s[1:2, :]), unroll=8)
    h_s[0:1, :] = hf
    h_s[1:2, :] = hb


def _lru(hmat, nb, conv_w, conv_b, wr, br, wi, bi, sp):
    t = hmat.shape[0]
    rh = TM // HALO
    last_h = t // HALO - 1

    def specs(reverse):
        def bmap(s):
            return _scan_block(s, nb, reverse)
        return [
            pl.BlockSpec((HALO, LRU_WIDTH), lambda s: (jnp.maximum(bmap(s) * rh - 1, 0), 0)),
            pl.BlockSpec((TM, LRU_WIDTH), lambda s: (bmap(s), 0)),
            pl.BlockSpec((HALO, LRU_WIDTH), lambda s: (jnp.minimum((bmap(s) + 1) * rh, last_h), 0)),
        ], pl.BlockSpec((TM, LRU_WIDTH), lambda s: (bmap(s), 0))

    in_f, out_f = specs(False)
    in_b, out_b = specs(True)
    vec = lambda: pl.BlockSpec((2, 1, LRU_WIDTH), lambda s: (0, 0, 0))
    mat = lambda: pl.BlockSpec((2, LRU_WIDTH, LRU_WIDTH), lambda s: (0, 0, 0))
    sds = jax.ShapeDtypeStruct((t, LRU_WIDTH), F32)
    return pl.pallas_call(
        functools.partial(_lru_kernel, nb=nb),
        out_shape=(sds, sds),
        grid=(nb + 1,),
        in_specs=in_f + in_b + [
            pl.BlockSpec((CONV_W, LRU_WIDTH), lambda s: (0, 0)),
            pl.BlockSpec((1, LRU_WIDTH), lambda s: (0, 0)),
            mat(), vec(), mat(), vec(), vec(),
        ],
        out_specs=(out_f, out_b),
        scratch_shapes=[pltpu.VMEM((2, TM, LRU_WIDTH), F32), pltpu.VMEM((2, TM, LRU_WIDTH), F32),
                        pltpu.VMEM((SUBLANES, LRU_WIDTH), F32)],
        compiler_params=_cp(("arbitrary",)),
        name="lru",
    )(hmat, hmat, hmat, hmat, hmat, hmat, conv_w, conv_b, wr, br, wi, bi, sp)


def _cmul(ar, ai, br, bi):
    return ar * br - ai * bi, ar * bi + ai * br


def _s5_kernel(u_ref, p_ref, pt_ref, bm_ref, cm_ref, a_ref, pw_ref, o_ref, x_s, st_s, c_s):
    s = pl.program_id(0)
    n = S5_N

    @pl.when(s == 0)
    def _():
        c_s[...] = jnp.zeros_like(c_s)

    up = jnp.dot(p_ref[...], u_ref[...], preferred_element_type=F32).astype(BF16)
    nc = n // S5_CHUNKS
    for c in range(S5_CHUNKS):
        xc = jnp.dot(up[:, c * LANES:(c + 1) * LANES], bm_ref[c], preferred_element_type=F32)
        x_s[:, c * nc:(c + 1) * nc] = xc[:, 0:nc]
        x_s[:, n + c * nc:n + (c + 1) * nc] = xc[:, nc:2 * nc]
    ns = n // S5_SCAN_CHUNKS
    for c in range(S5_SCAN_CHUNKS):
        re = slice(c * ns, (c + 1) * ns)
        im = slice(n + c * ns, n + (c + 1) * ns)
        ar = jnp.broadcast_to(a_ref[0:1, re], (SUBLANES, ns))
        ai = jnp.broadcast_to(a_ref[1:2, re], (SUBLANES, ns))

        def local(m, carry, re=re, im=im, ar=ar, ai=ai):
            hr, hi = carry
            row = pl.multiple_of(m * SUBLANES, SUBLANES)
            pr, pi = _cmul(ar, ai, hr, hi)
            hr = pr + x_s[pl.ds(row, SUBLANES), re]
            hi = pi + x_s[pl.ds(row, SUBLANES), im]
            x_s[pl.ds(row, SUBLANES), re] = hr
            x_s[pl.ds(row, SUBLANES), im] = hi
            return hr, hi

        z = jnp.zeros((SUBLANES, ns), F32)
        fr, fi = lax.fori_loop(0, SEG, local, (z, z), unroll=4)

        qr = pw_ref[TM - 1:TM, re]
        qi = pw_ref[TM - 1:TM, im]
        sr = c_s[0:1, re]
        si = c_s[0:1, im]
        for j in range(SUBLANES):
            st_s[j:j + 1, re] = sr
            st_s[j:j + 1, im] = si
            pr, pi = _cmul(qr, qi, sr, si)
            sr = pr + fr[j:j + 1]
            si = pi + fi[j:j + 1]
        c_s[0:1, re] = sr
        c_s[0:1, im] = si

        str_ = st_s[:, re]
        sti = st_s[:, im]

        def fix(m, _, re=re, im=im, str_=str_, sti=sti):
            row = pl.multiple_of(m * SUBLANES, SUBLANES)
            pr, pi = _cmul(pw_ref[pl.ds(row, SUBLANES), re], pw_ref[pl.ds(row, SUBLANES), im], str_, sti)
            x_s[pl.ds(row, SUBLANES), re] = x_s[pl.ds(row, SUBLANES), re] + pr
            x_s[pl.ds(row, SUBLANES), im] = x_s[pl.ds(row, SUBLANES), im] + pi
            return 0

        lax.fori_loop(0, SEG, fix, 0, unroll=2)

    yp = jnp.concatenate(
        [jnp.dot(x_s[:, c * nc:(c + 1) * nc].astype(BF16), cm_ref[0, c], preferred_element_type=F32)
         + jnp.dot(x_s[:, n + c * nc:n + (c + 1) * nc].astype(BF16), cm_ref[1, c], preferred_element_type=F32)
         for c in range(S5_CHUNKS)], axis=1)
    o_ref[...] = jnp.dot(pt_ref[...], yp.astype(BF16), preferred_element_type=F32).astype(BF16)


def _s5_perm(reverse):
    p = np.zeros((TM, TM), np.float32)
    for m in range(SEG):
        for j in range(SUBLANES):
            pos = j * SEG + m
            p[m * SUBLANES + j, (TM - 1 - pos) if reverse else pos] = 1.0
    return p


def _s5(hmat, nb, reverse, bm_t, cm_t, a_row, pw):
    t = hmat.shape[0]
    p = _s5_perm(reverse)
    n2 = 2 * S5_N

    def bmap(s):
        return _scan_block(s, nb, reverse)

    return pl.pallas_call(
        _s5_kernel,
        out_shape=jax.ShapeDtypeStruct((t, S5_WIDTH), BF16),
        grid=(nb + 1,),
        in_specs=[
            pl.BlockSpec((TM, S5_WIDTH), lambda s: (bmap(s), 2)),
            pl.BlockSpec((TM, TM), lambda s: (0, 0)),
            pl.BlockSpec((TM, TM), lambda s: (0, 0)),
            pl.BlockSpec((S5_CHUNKS, LANES, n2 // S5_CHUNKS), lambda s: (0, 0, 0)),
            pl.BlockSpec((2, S5_CHUNKS, S5_N // S5_CHUNKS, LANES), lambda s: (0, 0, 0, 0)),
            pl.BlockSpec((2, S5_N), lambda s: (0, 0)),
            pl.BlockSpec((TM, n2), lambda s: (0, 0)),
        ],
        out_specs=pl.BlockSpec((TM, S5_WIDTH), lambda s: (bmap(s), 0)),
        scratch_shapes=[pltpu.VMEM((TM, n2), F32), pltpu.VMEM((SUBLANES, n2), F32),
                        pltpu.VMEM((1, n2), F32)],
        compiler_params=_cp(("arbitrary",)),
        name="s5_bwd" if reverse else "s5_fwd",
    )(hmat, jnp.asarray(p, BF16), jnp.asarray(p.T, BF16), bm_t, cm_t, a_row, pw)


def _s5_params(a_re, a_im, log_dt, b_re, b_im, c_re, c_im):
    g, p = a_re.shape
    dt = jnp.exp(log_dt)[:, None]
    lr, li = a_re, a_im
    ea = jnp.exp(lr * dt)
    ab_r, ab_i = ea * jnp.cos(li * dt), ea * jnp.sin(li * dt)
    den = lr * lr + li * li
    co_r = ((ab_r - 1.0) * lr + ab_i * li) / den
    co_i = (ab_i * lr - (ab_r - 1.0) * li) / den
    bb_r = co_r[..., None] * b_re - co_i[..., None] * b_im
    bb_i = co_r[..., None] * b_im + co_i[..., None] * b_re
    gc = g // S5_CHUNKS
    eye = jnp.eye(gc, dtype=F32)

    def chunked(w, spec):
        return jnp.einsum(spec, w.reshape((S5_CHUNKS, gc) + w.shape[1:]), eye)

    bm_r = chunked(bb_r, 'kgpc,gh->kgchp').reshape(S5_CHUNKS, gc * S5_GROUP, gc * p)
    bm_i = chunked(bb_i, 'kgpc,gh->kgchp').reshape(S5_CHUNKS, gc * S5_GROUP, gc * p)
    bm_t = jnp.concatenate([bm_r, bm_i], axis=2).astype(BF16)
    cm_r = chunked(c_re, 'kgcp,gh->kgphc').reshape(S5_CHUNKS, gc * p, gc * S5_GROUP)
    cm_i = chunked(-c_im, 'kgcp,gh->kgphc').reshape(S5_CHUNKS, gc * p, gc * S5_GROUP)
    cm_t = jnp.stack([cm_r, cm_i], axis=0).astype(BF16)
    a_row = jnp.stack([ab_r.reshape(-1), ab_i.reshape(-1)], axis=0)
    k = jnp.arange(1, SEG + 1, dtype=F32)[:, None, None]
    ek = jnp.exp(k * (lr * dt))
    pw = jnp.concatenate([(ek * jnp.cos(k * (li * dt))).reshape(SEG, -1),
                          (ek * jnp.sin(k * (li * dt))).reshape(SEG, -1)], axis=1)
    return bm_t, cm_t, a_row, jnp.repeat(pw, SUBLANES, axis=0)


def _qkv_kernel(q_ref, k_ref, v_ref, cos_ref, sin_ref, qa_ref, qb_ref, ko_ref, vo_ref):
    w = DA_HEADS * 2 * DA_HEAD_DIM
    cos = jnp.concatenate([cos_ref[...]] * (w // LANES), axis=1)
    sin = jnp.concatenate([sin_ref[...]] * (w // LANES), axis=1)
    lane = lax.broadcasted_iota(jnp.int32, (TM, w), 1)
    first = (lane % (2 * ROPE_FREQS)) < ROPE_FREQS

    def rope(x):
        swapped = jnp.where(first, pltpu.roll(x, w - ROPE_FREQS, 1), pltpu.roll(x, ROPE_FREQS, 1))
        return x * cos + swapped * sin

    q = rope(q_ref[...].astype(F32)) * (DA_SCALE * LOG2_E)
    in_a = (lane % (2 * DA_HEAD_DIM)) < DA_HEAD_DIM
    qa_ref[...] = jnp.where(in_a, q, 0.0).astype(BF16)
    qb_ref[...] = jnp.where(in_a, 0.0, q).astype(BF16)
    ko_ref[...] = rope(k_ref[...].astype(F32)).astype(BF16)
    v = v_ref[...]
    ones = jnp.ones((TM, DA_V_DIM), BF16)
    vo_ref[...] = jnp.concatenate(
        [blk for h in range(DA_HEADS) for blk in (v[:, h * DA_V_DIM:(h + 1) * DA_V_DIM], ones)], axis=1)


def _qkv_prep(hmat, cos_t, sin_t):
    t = hmat.shape[0]
    w = DA_HEADS * 2 * DA_HEAD_DIM
    col = lambda c: pl.BlockSpec((TM, w), lambda i: (i, c))
    tab = lambda: pl.BlockSpec((TM, LANES), lambda i: (i, 0))
    out = lambda: pl.BlockSpec((TM, w), lambda i: (i, 0))
    sds = jax.ShapeDtypeStruct((t, w), BF16)
    return pl.pallas_call(
        _qkv_kernel,
        out_shape=(sds, sds, sds, jax.ShapeDtypeStruct((t, 2 * w), BF16)),
        grid=(t // TM,),
        in_specs=[col(3), col(4), col(5), tab(), tab()],
        out_specs=(out(), out(), out(), pl.BlockSpec((TM, 2 * w), lambda i: (i, 0))),
        compiler_params=_cp(("arbitrary",)),
        name="qkv_prep",
    )(hmat, hmat, hmat, cos_t, sin_t)


def _rope_tables(n_lat, n_ctx):
    n_rows = n_lat // GRID_W
    freqs = ROPE_BASE ** (-jnp.arange(ROPE_FREQS, dtype=F32) / ROPE_FREQS)
    ang_r = jnp.arange(n_rows).astype(F32)[:, None] * freqs
    ang_c = jnp.arange(GRID_W).astype(F32)[:, None] * freqs
    per_row = lambda a: jnp.repeat(a, GRID_W, axis=0)
    per_col = lambda a: jnp.tile(a, (n_rows, 1))
    cr, sr = per_row(jnp.cos(ang_r)), per_row(jnp.sin(ang_r))
    cc, sc = per_col(jnp.cos(ang_c)), per_col(jnp.sin(ang_c))
    cos64 = jnp.concatenate([cr, cr, cc, cc], axis=1)
    sin64 = jnp.concatenate([-sr, sr, -sc, sc], axis=1)
    cos_t = jnp.concatenate([cos64, cos64], axis=1)
    sin_t = jnp.concatenate([sin64, sin64], axis=1)
    cos_t = jnp.concatenate([cos_t, jnp.ones((n_ctx, LANES), F32)], axis=0)
    sin_t = jnp.concatenate([sin_t, jnp.zeros((n_ctx, LANES), F32)], axis=0)
    return cos_t, sin_t


def _attn_kernel(lam_ref, g_ref, qa_ref, qb_ref, k_ref, v_ref, prev_ref, o_ref, m_s, acc_s, a_s, s_s,
                 *, lam_init, nk):
    del prev_ref
    g = pl.program_id(1)
    kj = g % nk

    @pl.when(g == 0)
    def _():
        s_s[...] = jnp.zeros_like(s_s)
        acc_s[...] = jnp.zeros_like(acc_s)
        a_s[...] = jnp.zeros_like(a_s)
        m_s[...] = jnp.zeros_like(m_s)

    def step(cur):
        k = k_ref[...]
        v = v_ref[...]
        for i, q_ref in enumerate((qa_ref, qb_ref)):
            m_old = m_s[i]
            p = jnp.exp2(s_s[1 - cur, i] - m_old[:, 0:1]).astype(BF16)
            alpha = a_s[i]
            acc_s[i] = (jnp.concatenate([alpha, alpha], axis=1) * acc_s[i]
                        + jnp.dot(p, v, preferred_element_type=F32))
            s = lax.dot_general(q_ref[...], k, (((1,), (1,)), ((), ())), preferred_element_type=F32)
            s_s[cur, i] = s
            m_base = jnp.where(kj == 0, -jnp.inf, m_old)
            m_new = jnp.maximum(m_base, jnp.max(s, axis=1, keepdims=True))
            a_s[i] = jnp.exp2(m_base - m_new)
            m_s[i] = m_new

    def finish():
        lv = lam_ref[...]
        lam = (jnp.exp(jnp.sum(lv[0:1] * lv[1:2], axis=1, keepdims=True))
               - jnp.exp(jnp.sum(lv[2:3] * lv[3:4], axis=1, keepdims=True)) + lam_init)
        a0, a1 = acc_s[0], acc_s[1]
        o = (a0[:, 0:DA_V_DIM] / a0[:, DA_V_DIM:2 * DA_V_DIM]
             - lam * (a1[:, 0:DA_V_DIM] / a1[:, DA_V_DIM:2 * DA_V_DIM]))
        ms = jnp.mean(o * o, axis=1, keepdims=True)
        o_ref[...] = (o * lax.rsqrt(ms + RMS_EPS) * g_ref[...] * (1.0 - lam_init)).astype(BF16)

    for parity in range(2):
        pl.when(g % 2 == parity)(functools.partial(step, parity))
    pl.when(jnp.logical_and(kj == 0, g > 0))(finish)


def _attention(qa, qb, k, v, lam_vec, subln_g, lam_init, prev, *, q0, nq, tq, k0, nk, tk, name):
    t = qa.shape[0]
    n_pairs = nq * nk

    def qmap(h, g):
        return (q0 + jnp.minimum(g // nk, nq - 1), h)

    return pl.pallas_call(
        functools.partial(_attn_kernel, lam_init=lam_init, nk=nk),
        out_shape=jax.ShapeDtypeStruct((t, DA_HEADS * DA_V_DIM), BF16),
        grid=(DA_HEADS, n_pairs + 1),
        in_specs=[
            pl.BlockSpec((4, DA_HEAD_DIM), lambda h, g: (0, 0)),
            pl.BlockSpec((1, DA_V_DIM), lambda h, g: (0, 0)),
            pl.BlockSpec((tq, LANES), qmap),
            pl.BlockSpec((tq, LANES), qmap),
            pl.BlockSpec((tk, LANES), lambda h, g: (k0 + g % nk, h)),
            pl.BlockSpec((tk, 2 * DA_V_DIM), lambda h, g: (k0 + (g + nk - 1) % nk, h)),
            pl.BlockSpec(memory_space=pl.ANY),
        ],
        out_specs=pl.BlockSpec((tq, LANES), lambda h, g: (q0 + jnp.maximum(g - 1, 0) // nk, h)),
        scratch_shapes=[pltpu.VMEM((2, tq, LANES), F32), pltpu.VMEM((2, tq, 2 * DA_V_DIM), F32),
                        pltpu.VMEM((2, tq, LANES), F32), pltpu.VMEM((2, 2, tq, tk), F32)],
        input_output_aliases={6: 0},
        compiler_params=_cp(("arbitrary", "arbitrary")),
        name=name,
    )(lam_vec, subln_g, qa, qb, k, v, prev)


def _merge_kernel(x_ref, mod_ref, gt_ref, ga_ref, ub_ref, hf_ref, hb_ref, sf_ref, sb_ref, yc_ref,
                  sd_ref, wglu_ref, bglu_ref, wp_ref, bg_ref, wo_ref, lng_ref, lnb_ref,
                  x1_ref, v_ref, *, dn_alpha):
    md = mod_ref[0]
    ya = (hf_ref[...] + hb_ref[...]) * _gelu_tanh(ga_ref[...].astype(F32))
    y = sf_ref[...].astype(F32) + sb_ref[...].astype(F32) + sd_ref[...] * ub_ref[...].astype(F32)
    tg = jnp.dot(_gelu_tanh(y).astype(BF16), wglu_ref[...], preferred_element_type=F32) + bglu_ref[...]
    yb = tg[:, 0:S5_WIDTH] * _sigmoid(tg[:, S5_WIDTH:2 * S5_WIDTH])
    yc = yc_ref[...]
    bg = bg_ref[...]
    z = jnp.zeros((TM, D_MODEL), F32)
    for n, br in enumerate((ya, yb, yc)):
        gate = _sigmoid(gt_ref[:, n * D_MODEL:(n + 1) * D_MODEL].astype(F32) + bg[n:n + 1])
        z = z + gate * jnp.dot(br.astype(BF16), wp_ref[n], preferred_element_type=F32)
    m = jnp.dot(z.astype(BF16), wo_ref[...], preferred_element_type=F32)
    x1 = _layer_norm(dn_alpha * x_ref[...] + md[2:3] * m, lng_ref[...], lnb_ref[...])
    x1_ref[...] = x1
    v_ref[...] = x1 * (1.0 + md[4:5]) + md[3:4]


def _merge(x, modv, hmat, hf, hb, sf, sb, yc, s5_d, w_glu, b_glu, w_proj, b_gate, w_out, ln_g, ln_b,
           n_lat_blocks, dn_alpha):
    t = x.shape[0]
    row = lambda w, c=0: pl.BlockSpec((TM, w), lambda i: (i, c))
    const = lambda *shape: pl.BlockSpec(shape, lambda i: (0,) * len(shape))
    sds = jax.ShapeDtypeStruct((t, D_MODEL), F32)
    return pl.pallas_call(
        functools.partial(_merge_kernel, dn_alpha=dn_alpha),
        out_shape=(sds, sds),
        grid=(t // TM,),
        in_specs=[
            row(D_MODEL),
            pl.BlockSpec((1, 6, D_MODEL), lambda i: (i // n_lat_blocks, 0, 0)),
            row(N_BRANCH * D_MODEL, 1), row(BW, 1), row(BW, 2),
            row(BW), row(BW), row(BW), row(BW), row(BW),
            const(1, S5_WIDTH), const(S5_WIDTH, 2 * S5_WIDTH), const(1, 2 * S5_WIDTH),
            const(N_BRANCH, BW, D_MODEL), const(N_BRANCH, D_MODEL), const(D_MODEL, D_MODEL),
            const(1, D_MODEL), const(1, D_MODEL),
        ],
        out_specs=(row(D_MODEL), row(D_MODEL)),
        compiler_params=_cp(("arbitrary",)),
        name="merge",
    )(x, modv, hmat, hmat, hmat, hf, hb, sf, sb, yc, s5_d, w_glu, b_glu, w_proj, b_gate, w_out,
      ln_g, ln_b)


def _router_kernel(v_ref, w_ref, b_ref, tri_ref, ti_ref, gw_ref, rk_ref, cnt_ref, c_s):
    s = pl.program_id(0)

    @pl.when(s == 0)
    def _():
        c_s[...] = jnp.zeros_like(c_s)

    logits = jnp.dot(v_ref[...], w_ref[...], preferred_element_type=F32, precision=HIGHEST) + b_ref[...]
    lane = lax.broadcasted_iota(jnp.int32, (TM, N_EXPERTS), 1).astype(F32)
    work = logits
    vals, idxs = [], []
    hot = jnp.zeros((TM, N_EXPERTS), F32)
    for _ in range(TOP_K):
        mx = jnp.max(work, axis=1, keepdims=True)
        ix = jnp.min(jnp.where(work == mx, lane, float(N_EXPERTS)), axis=1, keepdims=True)
        sel = lane == ix
        hot = jnp.where(sel, 1.0, hot)
        work = jnp.where(sel, -jnp.inf, work)
        vals.append(mx)
        idxs.append(ix)
    es = [jnp.exp(vv - vals[0]) for vv in vals]
    den = es[0] + es[1] + es[2] + es[3]
    before = jnp.dot(tri_ref[...], hot.astype(BF16), preferred_element_type=F32) + c_s[0:1, :]
    col = lax.broadcasted_iota(jnp.int32, (TM, TOP_K), 1)
    ti = jnp.zeros((TM, TOP_K), F32)
    gw = jnp.zeros((TM, TOP_K), F32)
    rk = jnp.zeros((TM, TOP_K), F32)
    for kk in range(TOP_K):
        rank = jnp.sum(jnp.where(lane == idxs[kk], before, 0.0), axis=1, keepdims=True)
        ti = jnp.where(col == kk, idxs[kk], ti)
        gw = jnp.where(col == kk, es[kk] / den, gw)
        rk = jnp.where(col == kk, rank, rk)
    ti_ref[...] = ti.astype(jnp.int32)
    gw_ref[...] = gw
    rk_ref[...] = rk.astype(jnp.int32)
    tot = c_s[0:1, :] + jnp.sum(hot, axis=0, keepdims=True)
    c_s[...] = jnp.broadcast_to(tot, c_s.shape)
    cnt_ref[...] = jnp.broadcast_to(tot, cnt_ref.shape).astype(jnp.int32)


def _router(v, w_router, b_router):
    t = v.shape[0]
    tri = jnp.asarray(np.tril(np.ones((TM, TM), np.float32), -1), BF16)
    row4 = lambda: pl.BlockSpec((TM, TOP_K), lambda i: (i, 0))
    return pl.pallas_call(
        _router_kernel,
        out_shape=(jax.ShapeDtypeStruct((t, TOP_K), jnp.int32),
                   jax.ShapeDtypeStruct((t, TOP_K), F32),
                   jax.ShapeDtypeStruct((t, TOP_K), jnp.int32),
                   jax.ShapeDtypeStruct((SUBLANES, N_EXPERTS), jnp.int32)),
        grid=(t // TM,),
        in_specs=[
            pl.BlockSpec((TM, D_MODEL), lambda i: (i, 0)),
            pl.BlockSpec((D_MODEL, N_EXPERTS), lambda i: (0, 0)),
            pl.BlockSpec((1, N_EXPERTS), lambda i: (0, 0)),
            pl.BlockSpec((TM, TM), lambda i: (0, 0)),
        ],
        out_specs=(row4(), row4(), row4(), pl.BlockSpec((SUBLANES, N_EXPERTS), lambda i: (0, 0))),
        scratch_shapes=[pltpu.VMEM((SUBLANES, N_EXPERTS), F32)],
        compiler_params=_cp(("arbitrary",)),
        name="router",
    )(v, w_router, b_router, tri)


def _dispatch_kernel(dest_ref, v_ref, xs_in, xs_hbm, sem):
    del xs_in

    def row_copy(r, kk):
        return pltpu.make_async_copy(v_ref.at[pl.ds(r, 1)],
                                     xs_hbm.at[pl.ds(dest_ref[r * TOP_K + kk], 1)], sem)

    def start(r, _):
        for kk in range(TOP_K):
            row_copy(r, kk).start(priority=kk % 2)
        return 0

    def wait(r, _):
        for kk in range(TOP_K):
            row_copy(r, kk).wait()
        return 0

    lax.fori_loop(0, TM, start, 0, unroll=8)
    lax.fori_loop(0, TM, wait, 0, unroll=8)


def _dispatch(dest_flat, v, n_slots):
    t = v.shape[0]
    xs0 = jnp.zeros((n_slots, D_MODEL), F32)
    return pl.pallas_call(
        _dispatch_kernel,
        out_shape=jax.ShapeDtypeStruct((n_slots, D_MODEL), F32),
        grid=(t // TM,),
        in_specs=[
            pl.BlockSpec((TM * TOP_K,), lambda i: (i,), memory_space=pltpu.SMEM),
            pl.BlockSpec((TM, D_MODEL), lambda i: (i, 0)),
            pl.BlockSpec(memory_space=pl.ANY),
        ],
        out_specs=pl.BlockSpec(memory_space=pl.ANY),
        scratch_shapes=[pltpu.SemaphoreType.DMA],
        input_output_aliases={2: 0},
        compiler_params=_cp(("arbitrary",)),
        name="moe_dispatch",
    )(dest_flat, v, xs0)


def _expert_kernel(be_ref, nu_ref, x_ref, wgu_ref, bgu_ref, wd_ref, bd_ref, o_ref, wgu_s, wd_s):
    b = pl.program_id(0)
    new_expert = jnp.logical_or(b == 0, be_ref[b] != be_ref[jnp.maximum(b - 1, 0)])

    @pl.when(jnp.logical_and(new_expert, b < nu_ref[0]))
    def _():
        wgu_s[...] = wgu_ref[0, 0].astype(BF16)
        wd_s[...] = wd_ref[0, 0].astype(BF16)

    @pl.when(b < nu_ref[0])
    def _():
        for r in range(0, MOE_BLK, MOE_SUB):
            rows = slice(r, r + MOE_SUB)
            h = jnp.dot(x_ref[rows, :].astype(BF16), wgu_s[...], preferred_element_type=F32) + bgu_ref[0]
            hg = jnp.minimum(h[:, 0:D_FF], SWIGLU_LIMIT)
            hl = jnp.clip(h[:, D_FF:2 * D_FF], -SWIGLU_LIMIT, SWIGLU_LIMIT)
            act = hg * _sigmoid(SWIGLU_ALPHA * hg) * (hl + 1.0)
            o_ref[rows, :] = jnp.dot(act.astype(BF16), wd_s[...], preferred_element_type=F32) + bd_ref[0]

    @pl.when(pl.program_id(0) >= nu_ref[0])
    def _():
        o_ref[...] = jnp.zeros_like(o_ref)


def _experts(blk_e, n_used, xs, layer, w_gu, b_gu, w_down, b_down):
    n_slots = xs.shape[0]
    nblk = n_slots // MOE_BLK
    grid_spec = pltpu.PrefetchScalarGridSpec(
        num_scalar_prefetch=2,
        grid=(nblk,),
        in_specs=[
            pl.BlockSpec((MOE_BLK, D_MODEL), lambda b, be, nu: (jnp.minimum(b, nu[0] - 1), 0)),
            pl.BlockSpec((1, 1, D_MODEL, 2 * D_FF), lambda b, be, nu: (layer, be[b], 0, 0)),
            pl.BlockSpec((1, 1, 2 * D_FF), lambda b, be, nu: (be[b], 0, 0)),
            pl.BlockSpec((1, 1, D_FF, D_MODEL), lambda b, be, nu: (layer, be[b], 0, 0)),
            pl.BlockSpec((1, 1, D_MODEL), lambda b, be, nu: (be[b], 0, 0)),
        ],
        out_specs=pl.BlockSpec((MOE_BLK, D_MODEL), lambda b, be, nu: (b, 0)),
        scratch_shapes=[pltpu.VMEM((D_MODEL, 2 * D_FF), BF16), pltpu.VMEM((D_FF, D_MODEL), BF16)],
    )
    return pl.pallas_call(
        _expert_kernel,
        out_shape=jax.ShapeDtypeStruct((n_slots, D_MODEL), F32),
        grid_spec=grid_spec,
        compiler_params=_cp(("arbitrary",)),
        name="moe_experts",
    )(blk_e, n_used, xs, w_gu, b_gu.reshape(N_EXPERTS, 1, 2 * D_FF), w_down,
      b_down.reshape(N_EXPERTS, 1, D_MODEL))


def _combine_kernel(dest_ref, ys_hbm, gw_ref, x_ref, mod_ref, lng_ref, lnb_ref, o_ref, g_s, sem,
                    *, dn_alpha):
    def row_copy(r, kk):
        return pltpu.make_async_copy(ys_hbm.at[pl.ds(dest_ref[r * TOP_K + kk], 1)],
                                     g_s.at[kk, pl.ds(r, 1)], sem)

    def start(r, _):
        for kk in range(TOP_K):
            row_copy(r, kk).start(priority=kk % 2)
        return 0

    def wait(r, _):
        for kk in range(TOP_K):
            row_copy(r, kk).wait()
        return 0

    lax.fori_loop(0, TM, start, 0, unroll=8)
    lax.fori_loop(0, TM, wait, 0, unroll=8)
    gw = gw_ref[...]
    f = jnp.zeros((TM, D_MODEL), F32)
    for kk in range(TOP_K):
        f = f + gw[:, kk:kk + 1] * g_s[kk]
    md = mod_ref[0]
    o_ref[...] = _layer_norm(dn_alpha * x_ref[...] + md[5:6] * f, lng_ref[...], lnb_ref[...])


def _combine(dest_flat, ys, gate_w, x1, modv, ln_g, ln_b, n_lat_blocks, dn_alpha):
    t = x1.shape[0]
    return pl.pallas_call(
        functools.partial(_combine_kernel, dn_alpha=dn_alpha),
        out_shape=jax.ShapeDtypeStruct((t, D_MODEL), F32),
        grid=(t // TM,),
        in_specs=[
            pl.BlockSpec((TM * TOP_K,), lambda i: (i,), memory_space=pltpu.SMEM),
            pl.BlockSpec(memory_space=pl.ANY),
            pl.BlockSpec((TM, TOP_K), lambda i: (i, 0)),
            pl.BlockSpec((TM, D_MODEL), lambda i: (i, 0)),
            pl.BlockSpec((1, 6, D_MODEL), lambda i: (i // n_lat_blocks, 0, 0)),
            pl.BlockSpec((1, D_MODEL), lambda i: (0, 0)),
            pl.BlockSpec((1, D_MODEL), lambda i: (0, 0)),
        ],
        out_specs=pl.BlockSpec((TM, D_MODEL), lambda i: (i, 0)),
        scratch_shapes=[pltpu.VMEM((TOP_K, TM, D_MODEL), F32), pltpu.SemaphoreType.DMA],
        compiler_params=_cp(("arbitrary",)),
        name="moe_combine",
    )(dest_flat, ys, gate_w, x1, modv, ln_g, ln_b)


def _moe(v, x1, modv, layer, w_router, b_router, w_gu, b_gu, w_down, b_down, ln_g, ln_b, n_lat_blocks,
         dn_alpha):
    t = v.shape[0]
    top_i, gate_w, rank, counts = _router(v, w_router, b_router.reshape(1, N_EXPERTS))
    counts = counts[0]
    padded = (counts + MOE_BLK - 1) // MOE_BLK * MOE_BLK
    pend = jnp.cumsum(padded)
    pstart = pend - padded
    dest = (pstart[top_i] + rank).reshape(-1).astype(jnp.int32)
    n_slots = t * TOP_K + N_EXPERTS * MOE_BLK
    nblk = n_slots // MOE_BLK
    starts = jnp.arange(nblk, dtype=jnp.int32) * MOE_BLK
    blk_e = jnp.minimum(jnp.sum((pend[None, :] <= starts[:, None]).astype(jnp.int32), axis=1),
                        N_EXPERTS - 1).astype(jnp.int32)
    n_used = (pend[-1:] // MOE_BLK).astype(jnp.int32)
    xs = _dispatch(dest, v, n_slots)
    ys = _experts(blk_e, n_used, xs, layer, w_gu, b_gu, w_down, b_down)
    return _combine(dest, ys, gate_w, x1, modv, ln_g, ln_b, n_lat_blocks, dn_alpha)


def _block_diag(w):
    nbk, bs, _ = w.shape
    eye = jnp.eye(nbk, dtype=w.dtype)
    return jnp.einsum('hij,hg->higj', w, eye).reshape(nbk * bs, nbk * bs)


def _query_block(n_lat):
    for tq in (1024, 512):
        if n_lat % tq == 0:
            return tq
    return TM


def _key_block(t):
    for tk in (1280, 640, 256):
        if t % tk == 0:
            return tk
    raise ValueError("token count must be a multiple of 256")


def kernel(x, c, ctx, c_ctx, w_mod, b_mod, w_in, conv_w, conv_b, lru_wr, lru_br, lru_wi, lru_bi, lru_lam, s5_a_re, s5_a_im, s5_log_dt, s5_b_re, s5_b_im, s5_c_re, s5_c_im, s5_d, s5_w_glu, s5_b_glu, da_lam, da_subln_g, w_proj, b_gate, w_out, ln_g, ln_b, w_router, b_router, w_gu, b_gu, w_down, b_down):
    bsz, n_lat, d = x.shape
    n_ctx = ctx.shape[1]
    depth = w_mod.shape[0]
    assert bsz == 1 and d == D_MODEL and n_ctx == TM and n_lat % TM == 0 and n_lat % GRID_W == 0
    nb = n_lat // TM
    t = n_lat + n_ctx
    dn_alpha = (2 * depth) ** 0.25

    cc = jnp.zeros((SUBLANES, D_MODEL), F32).at[0].set(c[0]).at[1].set(c_ctx)
    mod = _modulation(cc, w_mod, b_mod)
    cos_t, sin_t = _rope_tables(n_lat, n_ctx)
    tq = _query_block(n_lat)
    tk = _key_block(t)

    xs = jnp.concatenate([x[0], ctx[0]], axis=0)
    for l in range(depth):
        modv = mod[l, 0:2].reshape(2, 6, D_MODEL)
        hmat = _inproj(xs, modv, w_in[l].astype(BF16), nb)

        sp = jax.nn.softplus(-lru_lam[l])
        hdir = _lru(hmat, nb, conv_w[l], conv_b[l].reshape(1, -1),
                    jnp.stack([_block_diag(lru_wr[l, dr]) for dr in range(2)]).astype(BF16),
                    lru_br[l].reshape(2, 1, -1),
                    jnp.stack([_block_diag(lru_wi[l, dr]) for dr in range(2)]).astype(BF16),
                    lru_bi[l].reshape(2, 1, -1), sp.reshape(2, 1, -1))
        sdir = []
        for dr in range(2):
            prm = _s5_params(s5_a_re[l, dr], s5_a_im[l, dr], s5_log_dt[l, dr], s5_b_re[l, dr],
                             s5_b_im[l, dr], s5_c_re[l, dr], s5_c_im[l, dr])
            sdir.append(_s5(hmat, nb, dr == 1, *prm))

        qa, qb, kk, vv = _qkv_prep(hmat, cos_t, sin_t)
        lam_init = 0.8 - 0.6 * math.exp(-0.3 * l)
        g_row = da_subln_g[l].reshape(1, DA_V_DIM)
        yc = jnp.zeros((t, DA_HEADS * DA_V_DIM), BF16)
        yc = _attention(qa, qb, kk, vv, da_lam[l], g_row, lam_init, yc,
                        q0=0, nq=n_lat // tq, tq=tq, k0=0, nk=t // tk, tk=tk, name="attn_lat")
        yc = _attention(qa, qb, kk, vv, da_lam[l], g_row, lam_init, yc,
                        q0=nb, nq=1, tq=TM, k0=nb, nk=1, tk=TM, name="attn_ctx")

        x1, v = _merge(xs, modv, hmat, hdir[0], hdir[1], sdir[0], sdir[1], yc,
                       s5_d[l].reshape(1, -1), s5_w_glu[l].astype(BF16), s5_b_glu[l].reshape(1, -1),
                       w_proj[l].astype(BF16), b_gate[l], w_out[l].astype(BF16),
                       ln_g[l, 0].reshape(1, -1), ln_b[l, 0].reshape(1, -1), nb, dn_alpha)
        xs = _moe(v, x1, modv, l, w_router[l], b_router[l], w_gu, b_gu[l], w_down, b_down[l],
                  ln_g[l, 1].reshape(1, -1), ln_b[l, 1].reshape(1, -1), nb, dn_alpha)
    return xs[:n_lat][None]
```

```python
import functools
import math

import numpy as np
import jax
import jax.numpy as jnp
from jax import lax
from jax.experimental import pallas as pl
from jax.experimental.pallas import tpu as pltpu

F32 = jnp.float32
BF16 = jnp.bfloat16
HIGHEST = lax.Precision.HIGHEST

D_MODEL = 1024
GRID_W = 64
LRU_WIDTH = 512
LRU_BLOCKS = 8
CONV_W = 4
LRU_C = 8.0
S5_WIDTH = 512
S5_GROUP = 16
S5_GROUPS = S5_WIDTH // S5_GROUP
S5_STATE = 64
S5_N = S5_GROUPS * S5_STATE
DA_HEADS = 4
DA_HEAD_DIM = 64
DA_V_DIM = 128
DA_SCALE = DA_HEAD_DIM ** -0.5
ROPE_BASE = 10000.0
ROPE_FREQS = DA_HEAD_DIM // 4
N_BRANCH = 3
BW = 512
IN_COLS = 6 * BW + N_BRANCH * D_MODEL
N_EXPERTS = 32
TOP_K = 4
D_FF = D_MODEL
SWIGLU_ALPHA = 1.702
SWIGLU_LIMIT = 7.0
LN_EPS = 1e-5
RMS_EPS = 1e-6

SUBLANES = 8
LANES = 128
VMEM_LIMIT = 56 * 1024 * 1024

TM = 256
HALO = 16
SEG = TM // SUBLANES
S5_CHUNKS = S5_WIDTH // LANES
S5_SCAN_CHUNKS = 2
LOG2_E = math.log2(math.e)
MOE_BLK = 512
MOE_SUB = 256
TN_IN = 6144


def _cp(sem, vmem=VMEM_LIMIT):
    return pltpu.CompilerParams(dimension_semantics=sem, vmem_limit_bytes=vmem)


def _sigmoid(x):
    return 1.0 / (1.0 + jnp.exp(-x))


def _gelu_tanh(x):
    return 0.5 * x * (1.0 + jnp.tanh(math.sqrt(2.0 / math.pi) * (x + 0.044715 * (x * x * x))))


def _layer_norm(y, g, b):
    mu = jnp.mean(y, axis=-1, keepdims=True)
    yc = y - mu
    var = jnp.mean(yc * yc, axis=-1, keepdims=True)
    return yc * lax.rsqrt(var + LN_EPS) * g + b


def _mod_kernel(c_ref, w_ref, b_ref, o_ref):
    c = c_ref[...]
    s = c * _sigmoid(c)
    o_ref[0] = jnp.dot(s, w_ref[0], preferred_element_type=F32, precision=HIGHEST) + b_ref[0]


def _modulation(cc, w_mod, b_mod):
    depth = w_mod.shape[0]
    tn = 1536
    return pl.pallas_call(
        _mod_kernel,
        out_shape=jax.ShapeDtypeStruct((depth, SUBLANES, 6 * D_MODEL), F32),
        grid=(depth, 6 * D_MODEL // tn),
        in_specs=[
            pl.BlockSpec((SUBLANES, D_MODEL), lambda l, j: (0, 0)),
            pl.BlockSpec((1, D_MODEL, tn), lambda l, j: (l, 0, j)),
            pl.BlockSpec((1, 1, tn), lambda l, j: (l, 0, j)),
        ],
        out_specs=pl.BlockSpec((1, SUBLANES, tn), lambda l, j: (l, 0, j)),
        compiler_params=_cp(("arbitrary", "arbitrary")),
        name="modulation",
    )(cc, w_mod, b_mod.reshape(depth, 1, 6 * D_MODEL))


def _inproj_kernel(x_ref, mod_ref, w_ref, o_ref):
    md = mod_ref[0]
    u = x_ref[...] * (1.0 + md[1:2]) + md[0:1]
    o_ref[...] = jnp.dot(u.astype(BF16), w_ref[...], preferred_element_type=F32).astype(BF16)


def _inproj(x, modv, w_bf16, n_lat_blocks):
    t = x.shape[0]
    return pl.pallas_call(
        _inproj_kernel,
        out_shape=jax.ShapeDtypeStruct((t, IN_COLS), BF16),
        grid=(IN_COLS // TN_IN, t // TM),
        in_specs=[
            pl.BlockSpec((TM, D_MODEL), lambda j, i: (i, 0)),
            pl.BlockSpec((1, 6, D_MODEL), lambda j, i: (i // n_lat_blocks, 0, 0)),
            pl.BlockSpec((D_MODEL, TN_IN), lambda j, i: (0, j)),
        ],
        out_specs=pl.BlockSpec((TM, TN_IN), lambda j, i: (i, j)),
        compiler_params=_cp(("arbitrary", "arbitrary")),
        name="inproj",
    )(x, modv, w_bf16)


def _scan_block(s, nb, reverse):
    return (nb - s) if reverse else (s + nb) % (nb + 1)


def _lru_kernel(pf_ref, cf_ref, nf_ref, pb_ref, cb_ref, nb_ref, cw_ref, cbias_ref, wr_ref, br_ref, wi_ref,
                bi_ref, sp_ref, of_ref, ob_ref, a_s, b_s, h_s, *, nb):
    s = pl.program_id(0)

    @pl.when(s == 0)
    def _():
        h_s[...] = jnp.zeros_like(h_s)

    cw = cw_ref[...]
    for d, (prev_ref, cur_ref, next_ref) in enumerate(((pf_ref, cf_ref, nf_ref), (pb_ref, cb_ref, nb_ref))):
        blk = _scan_block(s, nb, d == 1)
        has_prev = jnp.logical_and(blk != 0, blk != nb)
        has_next = jnp.logical_and(blk != nb - 1, blk != nb)
        prev = jnp.where(has_prev, prev_ref[...].astype(F32), 0.0)
        nxt = jnp.where(has_next, next_ref[...].astype(F32), 0.0)
        ext = jnp.concatenate([prev, cur_ref[...].astype(F32), nxt], axis=0)
        xc = cbias_ref[...] + jnp.zeros((TM, LRU_WIDTH), F32)
        for k in range(CONV_W):
            off = HALO - CONV_W // 2 + k
            xc = xc + ext[off:off + TM] * cw[k:k + 1]
        xb = xc.astype(BF16)
        r = _sigmoid(jnp.dot(xb, wr_ref[d], preferred_element_type=F32) + br_ref[d])
        i = _sigmoid(jnp.dot(xb, wi_ref[d], preferred_element_type=F32) + bi_ref[d])
        a = jnp.exp(-LRU_C * r * sp_ref[d])
        a_s[d] = a
        b_s[d] = jnp.sqrt(1.0 - a * a) * (i * xc)

    def step(n, carry):
        hf, hb = carry
        tb = TM - 1 - n
        hf = a_s[0, pl.ds(n, 1), :] * hf + b_s[0, pl.ds(n, 1), :]
        of_ref[pl.ds(n, 1), :] = hf
        hb = a_s[1, pl.ds(tb, 1), :] * hb + b_s[1, pl.ds(tb, 1), :]
        ob_ref[pl.ds(tb, 1), :] = hb
        return hf, hb

    hf, hb = lax.fori_loop(0, TM, step, (h_s[0:1, :], h_s[1:2, :]), unroll=8)
    h_s[0:1, :] = hf
    h_s[1:2, :] = hb


def _lru(hmat, nb, conv_w, conv_b, wr, br, wi, bi, sp):
    t = hmat.shape[0]
    rh = TM // HALO
    last_h = t // HALO - 1

    def specs(reverse):
        def bmap(s):
            return _scan_block(s, nb, reverse)
        return [
            pl.BlockSpec((HALO, LRU_WIDTH), lambda s: (jnp.maximum(bmap(s) * rh - 1, 0), 0)),
            pl.BlockSpec((TM, LRU_WIDTH), lambda s: (bmap(s), 0)),
            pl.BlockSpec((HALO, LRU_WIDTH), lambda s: (jnp.minimum((bmap(s) + 1) * rh, last_h), 0)),
        ], pl.BlockSpec((TM, LRU_WIDTH), lambda s: (bmap(s), 0))

    in_f, out_f = specs(False)
    in_b, out_b = specs(True)
    vec = lambda: pl.BlockSpec((2, 1, LRU_WIDTH), lambda s: (0, 0, 0))
    mat = lambda: pl.BlockSpec((2, LRU_WIDTH, LRU_WIDTH), lambda s: (0, 0, 0))
    sds = jax.ShapeDtypeStruct((t, LRU_WIDTH), F32)
    return pl.pallas_call(
        functools.partial(_lru_kernel, nb=nb),
        out_shape=(sds, sds),
        grid=(nb + 1,),
        in_specs=in_f + in_b + [
            pl.BlockSpec((CONV_W, LRU_WIDTH), lambda s: (0, 0)),
            pl.BlockSpec((1, LRU_WIDTH), lambda s: (0, 0)),
            mat(), vec(), mat(), vec(), vec(),
        ],
        out_specs=(out_f, out_b),
        scratch_shapes=[pltpu.VMEM((2, TM, LRU_WIDTH), F32), pltpu.VMEM((2, TM, LRU_WIDTH), F32),
                        pltpu.VMEM((SUBLANES, LRU_WIDTH), F32)],
        compiler_params=_cp(("arbitrary",)),
        name="lru",
    )(hmat, hmat, hmat, hmat, hmat, hmat, conv_w, conv_b, wr, br, wi, bi, sp)


def _cmul(ar, ai, br, bi):
    return ar * br - ai * bi, ar * bi + ai * br


def _s5_kernel(u_ref, p_ref, pt_ref, bm_ref, cm_ref, a_ref, pw_ref, o_ref, x_s, st_s, c_s):
    s = pl.program_id(0)
    n = S5_N

    @pl.when(s == 0)
    def _():
        c_s[...] = jnp.zeros_like(c_s)

    up = jnp.dot(p_ref[...], u_ref[...], preferred_element_type=F32).astype(BF16)
    nc = n // S5_CHUNKS
    for c in range(S5_CHUNKS):
        xc = jnp.dot(up[:, c * LANES:(c + 1) * LANES], bm_ref[c], preferred_element_type=F32)
        x_s[:, c * nc:(c + 1) * nc] = xc[:, 0:nc]
        x_s[:, n + c * nc:n + (c + 1) * nc] = xc[:, nc:2 * nc]
    ns = n // S5_SCAN_CHUNKS
    for c in range(S5_SCAN_CHUNKS):
        re = slice(c * ns, (c + 1) * ns)
        im = slice(n + c * ns, n + (c + 1) * ns)
        ar = jnp.broadcast_to(a_ref[0:1, re], (SUBLANES, ns))
        ai = jnp.broadcast_to(a_ref[1:2, re], (SUBLANES, ns))

        def local(m, carry, re=re, im=im, ar=ar, ai=ai):
            hr, hi = carry
            row = pl.multiple_of(m * SUBLANES, SUBLANES)
            pr, pi = _cmul(ar, ai, hr, hi)
            hr = pr + x_s[pl.ds(row, SUBLANES), re]
            hi = pi + x_s[pl.ds(row, SUBLANES), im]
            x_s[pl.ds(row, SUBLANES), re] = hr
            x_s[pl.ds(row, SUBLANES), im] = hi
            return hr, hi

        z = jnp.zeros((SUBLANES, ns), F32)
        fr, fi = lax.fori_loop(0, SEG, local, (z, z), unroll=4)

        qr = pw_ref[TM - 1:TM, re]
        qi = pw_ref[TM - 1:TM, im]
        sr = c_s[0:1, re]
        si = c_s[0:1, im]
        for j in range(SUBLANES):
            st_s[j:j + 1, re] = sr
            st_s[j:j + 1, im] = si
            pr, pi = _cmul(qr, qi, sr, si)
            sr = pr + fr[j:j + 1]
            si = pi + fi[j:j + 1]
        c_s[0:1, re] = sr
        c_s[0:1, im] = si

        str_ = st_s[:, re]
        sti = st_s[:, im]

        def fix(m, _, re=re, im=im, str_=str_, sti=sti):
            row = pl.multiple_of(m * SUBLANES, SUBLANES)
            pr, pi = _cmul(pw_ref[pl.ds(row, SUBLANES), re], pw_ref[pl.ds(row, SUBLANES), im], str_, sti)
            x_s[pl.ds(row, SUBLANES), re] = x_s[pl.ds(row, SUBLANES), re] + pr
            x_s[pl.ds(row, SUBLANES), im] = x_s[pl.ds(row, SUBLANES), im] + pi
            return 0

        lax.fori_loop(0, SEG, fix, 0, unroll=2)

    yp = jnp.concatenate(
        [jnp.dot(x_s[:, c * nc:(c + 1) * nc].astype(BF16), cm_ref[0, c], preferred_element_type=F32)
         + jnp.dot(x_s[:, n + c * nc:n + (c + 1) * nc].astype(BF16), cm_ref[1, c], preferred_element_type=F32)
         for c in range(S5_CHUNKS)], axis=1)
    o_ref[...] = jnp.dot(pt_ref[...], yp.astype(BF16), preferred_element_type=F32).astype(BF16)


def _s5_perm(reverse):
    p = np.zeros((TM, TM), np.float32)
    for m in range(SEG):
        for j in range(SUBLANES):
            pos = j * SEG + m
            p[m * SUBLANES + j, (TM - 1 - pos) if reverse else pos] = 1.0
    return p


def _s5(hmat, nb, reverse, bm_t, cm_t, a_row, pw):
    t = hmat.shape[0]
    p = _s5_perm(reverse)
    n2 = 2 * S5_N

    def bmap(s):
        return _scan_block(s, nb, reverse)

    return pl.pallas_call(
        _s5_kernel,
        out_shape=jax.ShapeDtypeStruct((t, S5_WIDTH), BF16),
        grid=(nb + 1,),
        in_specs=[
            pl.BlockSpec((TM, S5_WIDTH), lambda s: (bmap(s), 2)),
            pl.BlockSpec((TM, TM), lambda s: (0, 0)),
            pl.BlockSpec((TM, TM), lambda s: (0, 0)),
            pl.BlockSpec((S5_CHUNKS, LANES, n2 // S5_CHUNKS), lambda s: (0, 0, 0)),
            pl.BlockSpec((2, S5_CHUNKS, S5_N // S5_CHUNKS, LANES), lambda s: (0, 0, 0, 0)),
            pl.BlockSpec((2, S5_N), lambda s: (0, 0)),
            pl.BlockSpec((TM, n2), lambda s: (0, 0)),
        ],
        out_specs=pl.BlockSpec((TM, S5_WIDTH), lambda s: (bmap(s), 0)),
        scratch_shapes=[pltpu.VMEM((TM, n2), F32), pltpu.VMEM((SUBLANES, n2), F32),
                        pltpu.VMEM((1, n2), F32)],
        compiler_params=_cp(("arbitrary",)),
        name="s5_bwd" if reverse else "s5_fwd",
    )(hmat, jnp.asarray(p, BF16), jnp.asarray(p.T, BF16), bm_t, cm_t, a_row, pw)


def _s5_params(a_re, a_im, log_dt, b_re, b_im, c_re, c_im):
    g, p = a_re.shape
    dt = jnp.exp(log_dt)[:, None]
    lr, li = a_re, a_im
    ea = jnp.exp(lr * dt)
    ab_r, ab_i = ea * jnp.cos(li * dt), ea * jnp.sin(li * dt)
    den = lr * lr + li * li
    co_r = ((ab_r - 1.0) * lr + ab_i * li) / den
    co_i = (ab_i * lr - (ab_r - 1.0) * li) / den
    bb_r = co_r[..., None] * b_re - co_i[..., None] * b_im
    bb_i = co_r[..., None] * b_im + co_i[..., None] * b_re
    gc = g // S5_CHUNKS
    eye = jnp.eye(gc, dtype=F32)

    def chunked(w, spec):
        return jnp.einsum(spec, w.reshape((S5_CHUNKS, gc) + w.shape[1:]), eye)

    bm_r = chunked(bb_r, 'kgpc,gh->kgchp').reshape(S5_CHUNKS, gc * S5_GROUP, gc * p)
    bm_i = chunked(bb_i, 'kgpc,gh->kgchp').reshape(S5_CHUNKS, gc * S5_GROUP, gc * p)
    bm_t = jnp.concatenate([bm_r, bm_i], axis=2).astype(BF16)
    cm_r = chunked(c_re, 'kgcp,gh->kgphc').reshape(S5_CHUNKS, gc * p, gc * S5_GROUP)
    cm_i = chunked(-c_im, 'kgcp,gh->kgphc').reshape(S5_CHUNKS, gc * p, gc * S5_GROUP)
    cm_t = jnp.stack([cm_r, cm_i], axis=0).astype(BF16)
    a_row = jnp.stack([ab_r.reshape(-1), ab_i.reshape(-1)], axis=0)
    k = jnp.arange(1, SEG + 1, dtype=F32)[:, None, None]
    ek = jnp.exp(k * (lr * dt))
    pw = jnp.concatenate([(ek * jnp.cos(k * (li * dt))).reshape(SEG, -1),
                          (ek * jnp.sin(k * (li * dt))).reshape(SEG, -1)], axis=1)
    return bm_t, cm_t, a_row, jnp.repeat(pw, SUBLANES, axis=0)


def _qkv_kernel(q_ref, k_ref, v_ref, cos_ref, sin_ref, qa_ref, qb_ref, ko_ref, vo_ref):
    w = DA_HEADS * 2 * DA_HEAD_DIM
    cos = jnp.concatenate([cos_ref[...]] * (w // LANES), axis=1)
    sin = jnp.concatenate([sin_ref[...]] * (w // LANES), axis=1)
    lane = lax.broadcasted_iota(jnp.int32, (TM, w), 1)
    first = (lane % (2 * ROPE_FREQS)) < ROPE_FREQS

    def rope(x):
        swapped = jnp.where(first, pltpu.roll(x, w - ROPE_FREQS, 1), pltpu.roll(x, ROPE_FREQS, 1))
        return x * cos + swapped * sin

    q = rope(q_ref[...].astype(F32)) * (DA_SCALE * LOG2_E)
    in_a = (lane % (2 * DA_HEAD_DIM)) < DA_HEAD_DIM
    qa_ref[...] = jnp.where(in_a, q, 0.0).astype(BF16)
    qb_ref[...] = jnp.where(in_a, 0.0, q).astype(BF16)
    ko_ref[...] = rope(k_ref[...].astype(F32)).astype(BF16)
    v = v_ref[...]
    ones = jnp.ones((TM, DA_V_DIM), BF16)
    vo_ref[...] = jnp.concatenate(
        [blk for h in range(DA_HEADS) for blk in (v[:, h * DA_V_DIM:(h + 1) * DA_V_DIM], ones)], axis=1)


def _qkv_prep(hmat, cos_t, sin_t):
    t = hmat.shape[0]
    w = DA_HEADS * 2 * DA_HEAD_DIM
    col = lambda c: pl.BlockSpec((TM, w), lambda i: (i, c))
    tab = lambda: pl.BlockSpec((TM, LANES), lambda i: (i, 0))
    out = lambda: pl.BlockSpec((TM, w), lambda i: (i, 0))
    sds = jax.ShapeDtypeStruct((t, w), BF16)
    return pl.pallas_call(
        _qkv_kernel,
        out_shape=(sds, sds, sds, jax.ShapeDtypeStruct((t, 2 * w), BF16)),
        grid=(t // TM,),
        in_specs=[col(3), col(4), col(5), tab(), tab()],
        out_specs=(out(), out(), out(), pl.BlockSpec((TM, 2 * w), lambda i: (i, 0))),
        compiler_params=_cp(("arbitrary",)),
        name="qkv_prep",
    )(hmat, hmat, hmat, cos_t, sin_t)


def _rope_tables(n_lat, n_ctx):
    n_rows = n_lat // GRID_W
    freqs = ROPE_BASE ** (-jnp.arange(ROPE_FREQS, dtype=F32) / ROPE_FREQS)
    ang_r = jnp.arange(n_rows).astype(F32)[:, None] * freqs
    ang_c = jnp.arange(GRID_W).astype(F32)[:, None] * freqs
    per_row = lambda a: jnp.repeat(a, GRID_W, axis=0)
    per_col = lambda a: jnp.tile(a, (n_rows, 1))
    cr, sr = per_row(jnp.cos(ang_r)), per_row(jnp.sin(ang_r))
    cc, sc = per_col(jnp.cos(ang_c)), per_col(jnp.sin(ang_c))
    cos64 = jnp.concatenate([cr, cr, cc, cc], axis=1)
    sin64 = jnp.concatenate([-sr, sr, -sc, sc], axis=1)
    cos_t = jnp.concatenate([cos64, cos64], axis=1)
    sin_t = jnp.concatenate([sin64, sin64], axis=1)
    cos_t = jnp.concatenate([cos_t, jnp.ones((n_ctx, LANES), F32)], axis=0)
    sin_t = jnp.concatenate([sin_t, jnp.zeros((n_ctx, LANES), F32)], axis=0)
    return cos_t, sin_t


def _attn_kernel(lam_ref, g_ref, qa_ref, qb_ref, k_ref, v_ref, prev_ref, o_ref, m_s, acc_s, a_s, s_s,
                 *, lam_init, nk):
    del prev_ref
    g = pl.program_id(1)
    kj = g % nk

    @pl.when(g == 0)
    def _():
        s_s[...] = jnp.zeros_like(s_s)
        acc_s[...] = jnp.zeros_like(acc_s)
        a_s[...] = jnp.zeros_like(a_s)
        m_s[...] = jnp.zeros_like(m_s)

    def step(cur):
        k = k_ref[...]
        v = v_ref[...]
        for i, q_ref in enumerate((qa_ref, qb_ref)):
            m_old = m_s[i]
            p = jnp.exp2(s_s[1 - cur, i] - m_old[:, 0:1]).astype(BF16)
            alpha = a_s[i]
            acc_s[i] = (jnp.concatenate([alpha, alpha], axis=1) * acc_s[i]
                        + jnp.dot(p, v, preferred_element_type=F32))
            s = lax.dot_general(q_ref[...], k, (((1,), (1,)), ((), ())), preferred_element_type=F32)
            s_s[cur, i] = s
            m_base = jnp.where(kj == 0, -jnp.inf, m_old)
            m_new = jnp.maximum(m_base, jnp.max(s, axis=1, keepdims=True))
            a_s[i] = jnp.exp2(m_base - m_new)
            m_s[i] = m_new

    def finish():
        lv = lam_ref[...]
        lam = (jnp.exp(jnp.sum(lv[0:1] * lv[1:2], axis=1, keepdims=True))
               - jnp.exp(jnp.sum(lv[2:3] * lv[3:4], axis=1, keepdims=True)) + lam_init)
        a0, a1 = acc_s[0], acc_s[1]
        o = (a0[:, 0:DA_V_DIM] / a0[:, DA_V_DIM:2 * DA_V_DIM]
             - lam * (a1[:, 0:DA_V_DIM] / a1[:, DA_V_DIM:2 * DA_V_DIM]))
        ms = jnp.mean(o * o, axis=1, keepdims=True)
        o_ref[...] = (o * lax.rsqrt(ms + RMS_EPS) * g_ref[...] * (1.0 - lam_init)).astype(BF16)

    for parity in range(2):
        pl.when(g % 2 == parity)(functools.partial(step, parity))
    pl.when(jnp.logical_and(kj == 0, g > 0))(finish)


def _attention(qa, qb, k, v, lam_vec, subln_g, lam_init, prev, *, q0, nq, tq, k0, nk, tk, name):
    t = qa.shape[0]
    n_pairs = nq * nk

    def qmap(h, g):
        return (q0 + jnp.minimum(g // nk, nq - 1), h)

    return pl.pallas_call(
        functools.partial(_attn_kernel, lam_init=lam_init, nk=nk),
        out_shape=jax.ShapeDtypeStruct((t, DA_HEADS * DA_V_DIM), BF16),
        grid=(DA_HEADS, n_pairs + 1),
        in_specs=[
            pl.BlockSpec((4, DA_HEAD_DIM), lambda h, g: (0, 0)),
            pl.BlockSpec((1, DA_V_DIM), lambda h, g: (0, 0)),
            pl.BlockSpec((tq, LANES), qmap),
            pl.BlockSpec((tq, LANES), qmap),
            pl.BlockSpec((tk, LANES), lambda h, g: (k0 + g % nk, h)),
            pl.BlockSpec((tk, 2 * DA_V_DIM), lambda h, g: (k0 + (g + nk - 1) % nk, h)),
            pl.BlockSpec(memory_space=pl.ANY),
        ],
        out_specs=pl.BlockSpec((tq, LANES), lambda h, g: (q0 + jnp.maximum(g - 1, 0) // nk, h)),
        scratch_shapes=[pltpu.VMEM((2, tq, LANES), F32), pltpu.VMEM((2, tq, 2 * DA_V_DIM), F32),
                        pltpu.VMEM((2, tq, LANES), F32), pltpu.VMEM((2, 2, tq, tk), F32)],
        input_output_aliases={6: 0},
        compiler_params=_cp(("arbitrary", "arbitrary")),
        name=name,
    )(lam_vec, subln_g, qa, qb, k, v, prev)


def _merge_kernel(x_ref, mod_ref, gt_ref, ga_ref, ub_ref, hf_ref, hb_ref, sf_ref, sb_ref, yc_ref,
                  sd_ref, wglu_ref, bglu_ref, wp_ref, bg_ref, wo_ref, lng_ref, lnb_ref,
                  x1_ref, v_ref, *, dn_alpha):
    md = mod_ref[0]
    ya = (hf_ref[...] + hb_ref[...]) * _gelu_tanh(ga_ref[...].astype(F32))
    y = sf_ref[...].astype(F32) + sb_ref[...].astype(F32) + sd_ref[...] * ub_ref[...].astype(F32)
    tg = jnp.dot(_gelu_tanh(y).astype(BF16), wglu_ref[...], preferred_element_type=F32) + bglu_ref[...]
    yb = tg[:, 0:S5_WIDTH] * _sigmoid(tg[:, S5_WIDTH:2 * S5_WIDTH])
    yc = yc_ref[...]
    bg = bg_ref[...]
    z = jnp.zeros((TM, D_MODEL), F32)
    for n, br in enumerate((ya, yb, yc)):
        gate = _sigmoid(gt_ref[:, n * D_MODEL:(n + 1) * D_MODEL].astype(F32) + bg[n:n + 1])
        z = z + gate * jnp.dot(br.astype(BF16), wp_ref[n], preferred_element_type=F32)
    m = jnp.dot(z.astype(BF16), wo_ref[...], preferred_element_type=F32)
    x1 = _layer_norm(dn_alpha * x_ref[...] + md[2:3] * m, lng_ref[...], lnb_ref[...])
    x1_ref[...] = x1
    v_ref[...] = x1 * (1.0 + md[4:5]) + md[3:4]


def _merge(x, modv, hmat, hf, hb, sf, sb, yc, s5_d, w_glu, b_glu, w_proj, b_gate, w_out, ln_g, ln_b,
           n_lat_blocks, dn_alpha):
    t = x.shape[0]
    row = lambda w, c=0: pl.BlockSpec((TM, w), lambda i: (i, c))
    const = lambda *shape: pl.BlockSpec(shape, lambda i: (0,) * len(shape))
    sds = jax.ShapeDtypeStruct((t, D_MODEL), F32)
    return pl.pallas_call(
        functools.partial(_merge_kernel, dn_alpha=dn_alpha),
        out_shape=(sds, sds),
        grid=(t // TM,),
        in_specs=[
            row(D_MODEL),
            pl.BlockSpec((1, 6, D_MODEL), lambda i: (i // n_lat_blocks, 0, 0)),
            row(N_BRANCH * D_MODEL, 1), row(BW, 1), row(BW, 2),
            row(BW), row(BW), row(BW), row(BW), row(BW),
            const(1, S5_WIDTH), const(S5_WIDTH, 2 * S5_WIDTH), const(1, 2 * S5_WIDTH),
            const(N_BRANCH, BW, D_MODEL), const(N_BRANCH, D_MODEL), const(D_MODEL, D_MODEL),
            const(1, D_MODEL), const(1, D_MODEL),
        ],
        out_specs=(row(D_MODEL), row(D_MODEL)),
        compiler_params=_cp(("arbitrary",)),
        name="merge",
    )(x, modv, hmat, hmat, hmat, hf, hb, sf, sb, yc, s5_d, w_glu, b_glu, w_proj, b_gate, w_out,
      ln_g, ln_b)


def _router_kernel(v_ref, w_ref, b_ref, tri_ref, ti_ref, gw_ref, rk_ref, cnt_ref, c_s):
    s = pl.program_id(0)

    @pl.when(s == 0)
    def _():
        c_s[...] = jnp.zeros_like(c_s)

    logits = lax.dot_general(w_ref[...], v_ref[...], (((0,), (1,)), ((), ())),
                             preferred_element_type=F32, precision=HIGHEST) + b_ref[...]
    row = lax.broadcasted_iota(jnp.int32, (N_EXPERTS, TM), 0).astype(F32)
    work = logits
    vals, idxs = [], []
    hot = jnp.zeros((N_EXPERTS, TM), F32)
    for _ in range(TOP_K):
        mx = jnp.max(work, axis=0, keepdims=True)
        ix = jnp.min(jnp.where(work == mx, row, float(N_EXPERTS)), axis=0, keepdims=True)
        sel = row == ix
        hot = jnp.where(sel, 1.0, hot)
        work = jnp.where(sel, -jnp.inf, work)
        vals.append(mx)
        idxs.append(ix)
    es = [jnp.exp(vv - vals[0]) for vv in vals]
    den = es[0] + es[1] + es[2] + es[3]
    before = jnp.dot(hot.astype(BF16), tri_ref[...], preferred_element_type=F32) + c_s[:, 0:1]
    ranks = [jnp.sum(jnp.where(row == idxs[kk], before, 0.0), axis=0, keepdims=True) for kk in range(TOP_K)]
    ti_ref[...] = jnp.concatenate(idxs, axis=0).astype(jnp.int32)
    gw_ref[...] = jnp.concatenate([e / den for e in es], axis=0)
    rk_ref[...] = jnp.concatenate(ranks, axis=0).astype(jnp.int32)
    tot = c_s[:, 0:1] + jnp.sum(hot, axis=1, keepdims=True)
    c_s[...] = jnp.broadcast_to(tot, c_s.shape)
    cnt_ref[...] = jnp.broadcast_to(tot, cnt_ref.shape).astype(jnp.int32)


def _router(v, w_router, b_router):
    t = v.shape[0]
    tri = jnp.asarray(np.triu(np.ones((TM, TM), np.float32), 1), BF16)
    row4 = lambda: pl.BlockSpec((TOP_K, TM), lambda i: (0, i))
    ti, gw, rk, cnt = pl.pallas_call(
        _router_kernel,
        out_shape=(jax.ShapeDtypeStruct((TOP_K, t), jnp.int32),
                   jax.ShapeDtypeStruct((TOP_K, t), F32),
                   jax.ShapeDtypeStruct((TOP_K, t), jnp.int32),
                   jax.ShapeDtypeStruct((N_EXPERTS, LANES), jnp.int32)),
        grid=(t // TM,),
        in_specs=[
            pl.BlockSpec((TM, D_MODEL), lambda i: (i, 0)),
            pl.BlockSpec((D_MODEL, N_EXPERTS), lambda i: (0, 0)),
            pl.BlockSpec((N_EXPERTS, 1), lambda i: (0, 0)),
            pl.BlockSpec((TM, TM), lambda i: (0, 0)),
        ],
        out_specs=(row4(), row4(), row4(), pl.BlockSpec((N_EXPERTS, LANES), lambda i: (0, 0))),
        scratch_shapes=[pltpu.VMEM((N_EXPERTS, LANES), F32)],
        compiler_params=_cp(("arbitrary",)),
        name="router",
    )(v, w_router, b_router.reshape(N_EXPERTS, 1), tri)
    return ti.T, gw.T, rk.T, cnt[:, 0][None]


def _dispatch_kernel(dest_ref, v_ref, xs_in, xs_hbm, sem):
    del xs_in

    def row_copy(r, kk):
        return pltpu.make_async_copy(v_ref.at[pl.ds(r, 1)],
                                     xs_hbm.at[pl.ds(dest_ref[r * TOP_K + kk], 1)], sem)

    def start(r, _):
        for kk in range(TOP_K):
            row_copy(r, kk).start(priority=kk % 2)
        return 0

    def wait(r, _):
        for kk in range(TOP_K):
            row_copy(r, kk).wait()
        return 0

    lax.fori_loop(0, TM, start, 0, unroll=8)
    lax.fori_loop(0, TM, wait, 0, unroll=8)


def _dispatch(dest_flat, v, n_slots):
    t = v.shape[0]
    xs0 = jnp.zeros((n_slots, D_MODEL), F32)
    return pl.pallas_call(
        _dispatch_kernel,
        out_shape=jax.ShapeDtypeStruct((n_slots, D_MODEL), F32),
        grid=(t // TM,),
        in_specs=[
            pl.BlockSpec((TM * TOP_K,), lambda i: (i,), memory_space=pltpu.SMEM),
            pl.BlockSpec((TM, D_MODEL), lambda i: (i, 0)),
            pl.BlockSpec(memory_space=pl.ANY),
        ],
        out_specs=pl.BlockSpec(memory_space=pl.ANY),
        scratch_shapes=[pltpu.SemaphoreType.DMA],
        input_output_aliases={2: 0},
        compiler_params=_cp(("arbitrary",)),
        name="moe_dispatch",
    )(dest_flat, v, xs0)


def _expert_kernel(be_ref, nu_ref, x_ref, wgu_ref, bgu_ref, wd_ref, bd_ref, o_ref, wgu_s, wd_s):
    b = pl.program_id(0)
    new_expert = jnp.logical_or(b == 0, be_ref[b] != be_ref[jnp.maximum(b - 1, 0)])

    @pl.when(jnp.logical_and(new_expert, b < nu_ref[0]))
    def _():
        wgu_s[...] = wgu_ref[0, 0].astype(BF16)
        wd_s[...] = wd_ref[0, 0].astype(BF16)

    @pl.when(b < nu_ref[0])
    def _():
        for r in range(0, MOE_BLK, MOE_SUB):
            rows = slice(r, r + MOE_SUB)
            h = jnp.dot(x_ref[rows, :].astype(BF16), wgu_s[...], preferred_element_type=F32) + bgu_ref[0]
            hg = jnp.minimum(h[:, 0:D_FF], SWIGLU_LIMIT)
            hl = jnp.clip(h[:, D_FF:2 * D_FF], -SWIGLU_LIMIT, SWIGLU_LIMIT)
            act = hg * _sigmoid(SWIGLU_ALPHA * hg) * (hl + 1.0)
            o_ref[rows, :] = jnp.dot(act.astype(BF16), wd_s[...], preferred_element_type=F32) + bd_ref[0]

    @pl.when(pl.program_id(0) >= nu_ref[0])
    def _():
        o_ref[...] = jnp.zeros_like(o_ref)


def _experts(blk_e, n_used, xs, layer, w_gu, b_gu, w_down, b_down):
    n_slots = xs.shape[0]
    nblk = n_slots // MOE_BLK
    grid_spec = pltpu.PrefetchScalarGridSpec(
        num_scalar_prefetch=2,
        grid=(nblk,),
        in_specs=[
            pl.BlockSpec((MOE_BLK, D_MODEL), lambda b, be, nu: (jnp.minimum(b, nu[0] - 1), 0)),
            pl.BlockSpec((1, 1, D_MODEL, 2 * D_FF), lambda b, be, nu: (layer, be[b], 0, 0)),
            pl.BlockSpec((1, 1, 2 * D_FF), lambda b, be, nu: (be[b], 0, 0)),
            pl.BlockSpec((1, 1, D_FF, D_MODEL), lambda b, be, nu: (layer, be[b], 0, 0)),
            pl.BlockSpec((1, 1, D_MODEL), lambda b, be, nu: (be[b], 0, 0)),
        ],
        out_specs=pl.BlockSpec((MOE_BLK, D_MODEL), lambda b, be, nu: (b, 0)),
        scratch_shapes=[pltpu.VMEM((D_MODEL, 2 * D_FF), BF16), pltpu.VMEM((D_FF, D_MODEL), BF16)],
    )
    return pl.pallas_call(
        _expert_kernel,
        out_shape=jax.ShapeDtypeStruct((n_slots, D_MODEL), F32),
        grid_spec=grid_spec,
        compiler_params=_cp(("arbitrary",)),
        name="moe_experts",
    )(blk_e, n_used, xs, w_gu, b_gu.reshape(N_EXPERTS, 1, 2 * D_FF), w_down,
      b_down.reshape(N_EXPERTS, 1, D_MODEL))


def _combine_kernel(dest_ref, ys_hbm, gw_ref, x_ref, mod_ref, lng_ref, lnb_ref, o_ref, g_s, sem,
                    *, dn_alpha):
    def row_copy(r, kk):
        return pltpu.make_async_copy(ys_hbm.at[pl.ds(dest_ref[r * TOP_K + kk], 1)],
                                     g_s.at[kk, pl.ds(r, 1)], sem)

    def start(r, _):
        for kk in range(TOP_K):
            row_copy(r, kk).start(priority=kk % 2)
        return 0

    def wait(r, _):
        for kk in range(TOP_K):
            row_copy(r, kk).wait()
        return 0

    lax.fori_loop(0, TM, start, 0, unroll=8)
    lax.fori_loop(0, TM, wait, 0, unroll=8)
    gw = gw_ref[...]
    f = jnp.zeros((TM, D_MODEL), F32)
    for kk in range(TOP_K):
        f = f + gw[:, kk:kk + 1] * g_s[kk]
    md = mod_ref[0]
    o_ref[...] = _layer_norm(dn_alpha * x_ref[...] + md[5:6] * f, lng_ref[...], lnb_ref[...])


def _combine(dest_flat, ys, gate_w, x1, modv, ln_g, ln_b, n_lat_blocks, dn_alpha):
    t = x1.shape[0]
    return pl.pallas_call(
        functools.partial(_combine_kernel, dn_alpha=dn_alpha),
        out_shape=jax.ShapeDtypeStruct((t, D_MODEL), F32),
        grid=(t // TM,),
        in_specs=[
            pl.BlockSpec((TM * TOP_K,), lambda i: (i,), memory_space=pltpu.SMEM),
            pl.BlockSpec(memory_space=pl.ANY),
            pl.BlockSpec((TM, TOP_K), lambda i: (i, 0)),
            pl.BlockSpec((TM, D_MODEL), lambda i: (i, 0)),
            pl.BlockSpec((1, 6, D_MODEL), lambda i: (i // n_lat_blocks, 0, 0)),
            pl.BlockSpec((1, D_MODEL), lambda i: (0, 0)),
            pl.BlockSpec((1, D_MODEL), lambda i: (0, 0)),
        ],
        out_specs=pl.BlockSpec((TM, D_MODEL), lambda i: (i, 0)),
        scratch_shapes=[pltpu.VMEM((TOP_K, TM, D_MODEL), F32), pltpu.SemaphoreType.DMA],
        compiler_params=_cp(("arbitrary",)),
        name="moe_combine",
    )(dest_flat, ys, gate_w, x1, modv, ln_g, ln_b)


def _moe(v, x1, modv, layer, w_router, b_router, w_gu, b_gu, w_down, b_down, ln_g, ln_b, n_lat_blocks,
         dn_alpha):
    t = v.shape[0]
    top_i, gate_w, rank, counts = _router(v, w_router, b_router.reshape(1, N_EXPERTS))
    counts = counts[0]
    padded = (counts + MOE_BLK - 1) // MOE_BLK * MOE_BLK
    pend = jnp.cumsum(padded)
    pstart = pend - padded
    dest = (pstart[top_i] + rank).reshape(-1).astype(jnp.int32)
    n_slots = t * TOP_K + N_EXPERTS * MOE_BLK
    nblk = n_slots // MOE_BLK
    starts = jnp.arange(nblk, dtype=jnp.int32) * MOE_BLK
    blk_e = jnp.minimum(jnp.sum((pend[None, :] <= starts[:, None]).astype(jnp.int32), axis=1),
                        N_EXPERTS - 1).astype(jnp.int32)
    n_used = (pend[-1:] // MOE_BLK).astype(jnp.int32)
    xs = _dispatch(dest, v, n_slots)
    ys = _experts(blk_e, n_used, xs, layer, w_gu, b_gu, w_down, b_down)
    return _combine(dest, ys, gate_w, x1, modv, ln_g, ln_b, n_lat_blocks, dn_alpha)


def _block_diag(w):
    nbk, bs, _ = w.shape
    eye = jnp.eye(nbk, dtype=w.dtype)
    return jnp.einsum('hij,hg->higj', w, eye).reshape(nbk * bs, nbk * bs)


def _query_block(n_lat):
    for tq in (1024, 512):
        if n_lat % tq == 0:
            return tq
    return TM


def _key_block(t):
    for tk in (1280, 640, 256):
        if t % tk == 0:
            return tk
    raise ValueError("token count must be a multiple of 256")


def kernel(x, c, ctx, c_ctx, w_mod, b_mod, w_in, conv_w, conv_b, lru_wr, lru_br, lru_wi, lru_bi, lru_lam, s5_a_re, s5_a_im, s5_log_dt, s5_b_re, s5_b_im, s5_c_re, s5_c_im, s5_d, s5_w_glu, s5_b_glu, da_lam, da_subln_g, w_proj, b_gate, w_out, ln_g, ln_b, w_router, b_router, w_gu, b_gu, w_down, b_down):
    bsz, n_lat, d = x.shape
    n_ctx = ctx.shape[1]
    depth = w_mod.shape[0]
    assert bsz == 1 and d == D_MODEL and n_ctx == TM and n_lat % TM == 0 and n_lat % GRID_W == 0
    nb = n_lat // TM
    t = n_lat + n_ctx
    dn_alpha = (2 * depth) ** 0.25

    cc = jnp.zeros((SUBLANES, D_MODEL), F32).at[0].set(c[0]).at[1].set(c_ctx)
    mod = _modulation(cc, w_mod, b_mod)
    cos_t, sin_t = _rope_tables(n_lat, n_ctx)
    tq = _query_block(n_lat)
    tk = _key_block(t)

    xs = jnp.concatenate([x[0], ctx[0]], axis=0)
    for l in range(depth):
        modv = mod[l, 0:2].reshape(2, 6, D_MODEL)
        hmat = _inproj(xs, modv, w_in[l].astype(BF16), nb)

        sp = jax.nn.softplus(-lru_lam[l])
        hdir = _lru(hmat, nb, conv_w[l], conv_b[l].reshape(1, -1),
                    jnp.stack([_block_diag(lru_wr[l, dr]) for dr in range(2)]).astype(BF16),
                    lru_br[l].reshape(2, 1, -1),
                    jnp.stack([_block_diag(lru_wi[l, dr]) for dr in range(2)]).astype(BF16),
                    lru_bi[l].reshape(2, 1, -1), sp.reshape(2, 1, -1))
        sdir = []
        for dr in range(2):
            prm = _s5_params(s5_a_re[l, dr], s5_a_im[l, dr], s5_log_dt[l, dr], s5_b_re[l, dr],
                             s5_b_im[l, dr], s5_c_re[l, dr], s5_c_im[l, dr])
            sdir.append(_s5(hmat, nb, dr == 1, *prm))

        qa, qb, kk, vv = _qkv_prep(hmat, cos_t, sin_t)
        lam_init = 0.8 - 0.6 * math.exp(-0.3 * l)
        g_row = da_subln_g[l].reshape(1, DA_V_DIM)
        yc = jnp.zeros((t, DA_HEADS * DA_V_DIM), BF16)
        yc = _attention(qa, qb, kk, vv, da_lam[l], g_row, lam_init, yc,
                        q0=0, nq=n_lat // tq, tq=tq, k0=0, nk=t // tk, tk=tk, name="attn_lat")
        yc = _attention(qa, qb, kk, vv, da_lam[l], g_row, lam_init, yc,
                        q0=nb, nq=1, tq=TM, k0=nb, nk=1, tk=TM, name="attn_ctx")

        x1, v = _merge(xs, modv, hmat, hdir[0], hdir[1], sdir[0], sdir[1], yc,
                       s5_d[l].reshape(1, -1), s5_w_glu[l].astype(BF16), s5_b_glu[l].reshape(1, -1),
                       w_proj[l].astype(BF16), b_gate[l], w_out[l].astype(BF16),
                       ln_g[l, 0].reshape(1, -1), ln_b[l, 0].reshape(1, -1), nb, dn_alpha)
        xs = _moe(v, x1, modv, l, w_router[l], b_router[l], w_gu, b_gu[l], w_down, b_down[l],
                  ln_g[l, 1].reshape(1, -1), ln_b[l, 1].reshape(1, -1), nb, dn_alpha)
    return xs[:n_lat][None]
```
